```python
import math
import jax
import jax.numpy as jnp
from jax import lax
import numpy as np

D_MODEL = 1024
BATCH = 8
SEQ = 2048
DEPTH = 2

MEM_LEN = 256
ROPE_THETA = 500000.0
ROPE_FRAC = 4
LN_EPS = 1e-5
NEG_INF = -1e30
DEEPNORM_ALPHA = (2 * DEPTH) ** 0.25
DEEPNORM_BETA = (8 * DEPTH) ** -0.25

A_HEADS = 4
A_QK_DIM = D_MODEL // 32
A_V_DIM = 2 * A_QK_DIM
A_QBLK = 128
B_HEADS = 6
B_HEAD_DIM = D_MODEL // 16
B_GROUPS = ((128, 1), (512, 4), (2048, 16))
C_HEADS = 4
C_K_DIM = 3 * D_MODEL // 64
C_V_DIM = 3 * D_MODEL // 32
C_GATE_RANK = 16
C_GATE_TAU = 16.0
C_CHUNK = 16

IN_SPLITS = (
    A_HEADS * 2 * A_QK_DIM, A_HEADS * 2 * A_QK_DIM, A_HEADS * A_V_DIM,
    B_HEADS * B_HEAD_DIM, B_HEADS * B_HEAD_DIM, B_HEADS * B_HEAD_DIM,
    C_HEADS * C_K_DIM, C_HEADS * C_K_DIM, C_HEADS * C_V_DIM, C_HEADS * C_V_DIM,
    C_GATE_RANK,
)
IN_BOUNDS = tuple(int(c) for c in np.cumsum((0,) + IN_SPLITS))
IN_COLS = IN_BOUNDS[-1]
VALUE_SLOTS = (2, 5, 8)

XA_HEADS = 4
XA_HEAD_DIM = D_MODEL // XA_HEADS
F_DENSE = ((8 * D_MODEL // 3 + 127) // 128) * 128
N_EXPERTS = 8
TOP_K = 2
F_MOE = 7 * D_MODEL // 2
N_DENSE = (DEPTH + 1) // 2
N_MOE = DEPTH // 2

kernel_name = 'hybrid_diff_dilated_gla_deepnorm_moe'


def layer_norm(x, g, b):
    xf = x.astype(jnp.float32)
    mu = xf.mean(-1, keepdims=True)
    var = jnp.square(xf - mu).mean(-1, keepdims=True)
    return ((xf - mu) * lax.rsqrt(var + LN_EPS) * g + b).astype(x.dtype)


def rms_norm(x, g):
    xf = x.astype(jnp.float32)
    return (xf * lax.rsqrt(jnp.mean(xf * xf, -1, keepdims=True) + LN_EPS) * g).astype(x.dtype)


def apply_partial_rope(x, positions):
    rot = x.shape[-1] // ROPE_FRAC
    inv_freq = ROPE_THETA ** (-jnp.arange(0, rot, 2, dtype=jnp.float32) / rot)
    ang = positions.astype(jnp.float32)[..., None] * inv_freq
    bshape = ang.shape[:2] + (1,) * (x.ndim - 3) + ang.shape[-1:]
    cos = jnp.cos(ang).reshape(bshape).astype(x.dtype)
    sin = jnp.sin(ang).reshape(bshape).astype(x.dtype)
    x1, x2, xp = x[..., :rot // 2], x[..., rot // 2:rot], x[..., rot:]
    return jnp.concatenate([x1 * cos - x2 * sin, x2 * cos + x1 * sin, xp], axis=-1)


def diff_attention(q, k, v, lam, lam_init, norm_g):
    bsz, seq, heads, _, dqk = q.shape
    nqb = seq // A_QBLK
    scale = dqk ** -0.5
    q_blocks = q.reshape(bsz, nqb, A_QBLK, heads, 2, dqk).transpose(1, 0, 2, 3, 4, 5)
    starts = jnp.arange(nqb, dtype=jnp.int32) * A_QBLK
    kpos = jnp.arange(seq, dtype=jnp.int32)

    def one_block(args):
        qb, start = args
        s = jnp.einsum('bqhmd,bkhmd->bhmqk', qb, k, preferred_element_type=jnp.float32) * scale
        qpos = start + jnp.arange(A_QBLK, dtype=jnp.int32)
        causal = kpos[None, :] <= qpos[:, None]
        p = jax.nn.softmax(jnp.where(causal, s, NEG_INF), axis=-1)
        a = p[:, :, 0] - lam * p[:, :, 1]
        return jnp.einsum('bhqk,bkhd->bqhd', a.astype(v.dtype), v)

    o = lax.map(one_block, (q_blocks, starts))
    o = o.transpose(1, 0, 2, 3, 4).reshape(bsz, seq, heads, v.shape[-1])
    return rms_norm(o, norm_g) * (1.0 - lam_init)


def dilated_group(q, k, v, window, dilation):
    bsz, seq, heads, dh = q.shape
    span = window // dilation
    sub_len = seq // dilation
    nb = -(-sub_len // span)
    pad = nb * span - sub_len

    def to_blocks(t):
        t = t.reshape(bsz, sub_len, dilation, heads, dh).transpose(0, 2, 3, 1, 4)
        t = jnp.pad(t, ((0, 0), (0, 0), (0, 0), (0, pad), (0, 0)))
        return t.reshape(bsz, dilation, heads, nb, span, dh)

    def with_prev(t):
        prev = jnp.pad(t, ((0, 0), (0, 0), (0, 0), (1, 0), (0, 0), (0, 0)))[:, :, :, :-1]
        return jnp.concatenate([prev, t], axis=4)

    qb = to_blocks(q)
    kk = with_prev(to_blocks(k))
    vv = with_prev(to_blocks(v))
    s = jnp.einsum('brhnqd,brhnkd->brhnqk', qb, kk, preferred_element_type=jnp.float32) * dh ** -0.5
    qi = jnp.arange(span)[:, None]
    ki = jnp.arange(2 * span)[None, :]
    dist = qi + span - ki
    band = (dist >= 0) & (dist <= span)
    has_prev = (jnp.arange(nb) > 0)[:, None, None] | (ki >= span)[None]
    valid = band[None] & has_prev
    s = jnp.where(valid, s, NEG_INF)
    m = s.max(-1, keepdims=True)
    p = jnp.exp(s - m)
    l = p.sum(-1, keepdims=True)
    o = jnp.einsum('brhnqk,brhnkd->brhnqd', (p / l).astype(v.dtype), vv)
    lse = (m + jnp.log(l))[..., 0]

    def from_blocks(t):
        t = t.reshape((bsz, dilation, heads, nb * span) + t.shape[5:])[:, :, :, :sub_len]
        t = jnp.moveaxis(t, 3, 1)
        return t.reshape((bsz, seq, heads) + t.shape[4:])

    return from_blocks(o), from_blocks(lse)


def dilated_attention(q, k, v):
    outs, lses = zip(*[dilated_group(q, k, v, w, r) for (w, r) in B_GROUPS])
    wts = jax.nn.softmax(jnp.stack(lses, axis=0), axis=0)
    return jnp.einsum('gbsh,gbshd->bshd', wts.astype(q.dtype), jnp.stack(outs, axis=0))


def gla_chunked(q, k, v, log_a):
    bsz, seq, heads, dk = q.shape
    dv = v.shape[-1]
    n = seq // C_CHUNK

    def to_chunks(t):
        return t.astype(jnp.float32).reshape(bsz, n, C_CHUNK, heads, t.shape[-1]).transpose(0, 3, 1, 2, 4)

    q, k, v, log_a = map(to_chunks, (q, k, v, log_a))
    q = q * dk ** -0.5
    b = jnp.cumsum(log_a, axis=3)
    causal = jnp.tril(jnp.ones((C_CHUNK, C_CHUNK), dtype=bool))
    diff = b[:, :, :, :, None, :] - b[:, :, :, None, :, :]
    decay = jnp.exp(jnp.where(causal[:, :, None], diff, NEG_INF))
    scores = jnp.einsum('bhnid,bhnjd,bhnijd->bhnij', q, k, decay)
    o_intra = jnp.einsum('bhnij,bhnjv->bhniv', scores, v)
    b_last = b[:, :, :, -1]
    u = jnp.einsum('bhncd,bhncv->bhndv', k * jnp.exp(b_last[:, :, :, None] - b), v)

    def step(state, inp):
        g, u_n = inp
        return jnp.exp(g)[..., None] * state + u_n, state

    init = jnp.zeros((bsz, heads, dk, dv), jnp.float32)
    _, s_prev = lax.scan(step, init, (jnp.moveaxis(b_last, 2, 0), jnp.moveaxis(u, 2, 0)))
    s_prev = jnp.moveaxis(s_prev, 0, 2)
    o_inter = jnp.einsum('bhncd,bhndv->bhncv', q * jnp.exp(b), s_prev)
    return (o_intra + o_inter).transpose(0, 2, 3, 1, 4).reshape(bsz, seq, heads, dv)


def hybrid_mixer(x, positions, w_in, lam_q1, lam_k1, lam_q2, lam_k2, lam_init,
                 diff_g, gla_wa2, gla_ba, gla_g, w_out):
    bsz, seq, _ = x.shape
    h = x @ w_in
    qa, ka, va, qb, kb, vb, qc, kc, vc, gc, rc = jnp.split(h, IN_BOUNDS[1:-1], axis=-1)
    qa = apply_partial_rope(qa.reshape(bsz, seq, A_HEADS, 2, A_QK_DIM), positions)
    ka = apply_partial_rope(ka.reshape(bsz, seq, A_HEADS, 2, A_QK_DIM), positions)
    va = va.reshape(bsz, seq, A_HEADS, A_V_DIM)
    f32 = jnp.float32
    lam = (jnp.exp(jnp.sum(lam_q1.astype(f32) * lam_k1.astype(f32)))
           - jnp.exp(jnp.sum(lam_q2.astype(f32) * lam_k2.astype(f32))) + lam_init)
    oa = diff_attention(qa, ka, va, lam, lam_init, diff_g)
    qb = apply_partial_rope(qb.reshape(bsz, seq, B_HEADS, B_HEAD_DIM), positions)
    kb = apply_partial_rope(kb.reshape(bsz, seq, B_HEADS, B_HEAD_DIM), positions)
    vb = vb.reshape(bsz, seq, B_HEADS, B_HEAD_DIM)
    ob = dilated_attention(qb, kb, vb)
    log_a = jax.nn.log_sigmoid((rc @ gla_wa2 + gla_ba).astype(f32)) / C_GATE_TAU
    oc = gla_chunked(qc.reshape(bsz, seq, C_HEADS, C_K_DIM), kc.reshape(bsz, seq, C_HEADS, C_K_DIM),
                     vc.reshape(bsz, seq, C_HEADS, C_V_DIM), log_a.reshape(bsz, seq, C_HEADS, C_K_DIM))
    oc = rms_norm(oc, gla_g).astype(x.dtype) * jax.nn.silu(gc.reshape(bsz, seq, C_HEADS, C_V_DIM))
    o = jnp.concatenate([oa.reshape(bsz, seq, -1), ob.reshape(bsz, seq, -1),
                         oc.reshape(bsz, seq, -1)], axis=-1)
    return o @ w_out


def memory_cross_attention(x, mem, wq, wkv, wo):
    bsz, seq, _ = x.shape
    q = (x @ wq).reshape(bsz, seq, XA_HEADS, XA_HEAD_DIM)
    k, v = jnp.split(mem @ wkv, 2, axis=-1)
    k = k.reshape(bsz, mem.shape[1], XA_HEADS, XA_HEAD_DIM)
    v = v.reshape(bsz, mem.shape[1], XA_HEADS, XA_HEAD_DIM)
    s = jnp.einsum('bqhd,bkhd->bhqk', q, k, preferred_element_type=jnp.float32) * XA_HEAD_DIM ** -0.5
    p = jax.nn.softmax(s, axis=-1)
    o = jnp.einsum('bhqk,bkhd->bqhd', p.astype(v.dtype), v).reshape(bsz, seq, D_MODEL)
    return o @ wo


def swiglu(x, w13, w2):
    gate, up = jnp.split(x @ w13, 2, axis=-1)
    return (jax.nn.silu(gate) * up) @ w2


def moe_swiglu(x, router, w13, w2):
    logits = (x @ router).astype(jnp.float32)
    top_vals, top_idx = lax.top_k(logits, TOP_K)
    top_w = jax.nn.softmax(top_vals, axis=-1)
    gates = jnp.einsum('bsk,bske->bse', top_w, jax.nn.one_hot(top_idx, N_EXPERTS, dtype=jnp.float32))
    out = jnp.zeros_like(x)
    for e in range(N_EXPERTS):
        out = out + gates[..., e:e + 1].astype(x.dtype) * swiglu(x, w13[e], w2[e])
    return out


def setup_inputs(seed: int = 0) -> dict:
    key = jax.random.key(seed)
    ks = jax.random.split(key, 28)

    def nrm(k, shape, scale):
        return jax.random.normal(k, shape, jnp.float32) * scale

    def gain(k, shape):
        return 1.0 + 0.02 * jax.random.normal(k, shape, jnp.float32)

    def bias(k, shape):
        return 0.02 * jax.random.normal(k, shape, jnp.float32)

    col_scale = np.ones((IN_COLS,), np.float32)
    for slot in VALUE_SLOTS:
        col_scale[IN_BOUNDS[slot]:IN_BOUNDS[slot + 1]] = DEEPNORM_BETA
    offset = jax.random.randint(ks[2], (BATCH, 1), 0, 4096, dtype=jnp.int32)
    positions = (offset + jnp.arange(SEQ, dtype=jnp.int32)[None, :]).astype(jnp.int32)
    ds = D_MODEL ** -0.5
    return {
        'x': nrm(ks[0], (BATCH, SEQ, D_MODEL), 1.0),
        'mem': nrm(ks[1], (BATCH, MEM_LEN, D_MODEL), 1.0),
        'positions': positions,
        'w_in': nrm(ks[3], (DEPTH, D_MODEL, IN_COLS), ds) * jnp.asarray(col_scale),
        'lam_q1': nrm(ks[4], (DEPTH, A_QK_DIM), 0.1),
        'lam_k1': nrm(ks[5], (DEPTH, A_QK_DIM), 0.1),
        'lam_q2': nrm(ks[6], (DEPTH, A_QK_DIM), 0.1),
        'lam_k2': nrm(ks[7], (DEPTH, A_QK_DIM), 0.1),
        'diff_norm_g': gain(ks[8], (DEPTH, A_V_DIM)),
        'gla_wa2': nrm(ks[9], (DEPTH, C_GATE_RANK, C_HEADS * C_K_DIM), C_GATE_RANK ** -0.5),
        'gla_ba': nrm(ks[10], (DEPTH, C_HEADS * C_K_DIM), 0.1),
        'gla_norm_g': gain(ks[11], (DEPTH, C_V_DIM)),
        'w_out': nrm(ks[12], (DEPTH, D_MODEL, D_MODEL), ds * DEEPNORM_BETA),
        'ln_mix_g': gain(ks[13], (DEPTH, D_MODEL)),
        'ln_mix_b': bias(ks[14], (DEPTH, D_MODEL)),
        'xa_wq': nrm(ks[15], (DEPTH, D_MODEL, D_MODEL), ds),
        'xa_wkv': jnp.concatenate([nrm(ks[16], (DEPTH, D_MODEL, D_MODEL), ds),
                                   nrm(ks[17], (DEPTH, D_MODEL, D_MODEL), ds * DEEPNORM_BETA)], axis=-1),
        'xa_wo': nrm(ks[18], (DEPTH, D_MODEL, D_MODEL), ds * DEEPNORM_BETA),
        'ln_xa_g': gain(ks[19], (DEPTH, D_MODEL)),
        'ln_xa_b': bias(ks[20], (DEPTH, D_MODEL)),
        'ffd_w13': nrm(ks[21], (N_DENSE, D_MODEL, 2 * F_DENSE), ds * DEEPNORM_BETA),
        'ffd_w2': nrm(ks[22], (N_DENSE, F_DENSE, D_MODEL), F_DENSE ** -0.5 * DEEPNORM_BETA),
        'moe_router': nrm(ks[23], (N_MOE, D_MODEL, N_EXPERTS), ds),
        'moe_w13': nrm(ks[24], (N_MOE, N_EXPERTS, D_MODEL, 2 * F_MOE), ds * DEEPNORM_BETA),
        'moe_w2': nrm(ks[25], (N_MOE, N_EXPERTS, F_MOE, D_MODEL), F_MOE ** -0.5 * DEEPNORM_BETA),
        'ln_ffn_g': gain(ks[26], (DEPTH, D_MODEL)),
        'ln_ffn_b': bias(ks[27], (DEPTH, D_MODEL)),
    }


def reference(x, mem, positions, w_in, lam_q1, lam_k1, lam_q2, lam_k2, diff_norm_g, gla_wa2, gla_ba,
              gla_norm_g, w_out, ln_mix_g, ln_mix_b, xa_wq, xa_wkv, xa_wo, ln_xa_g, ln_xa_b,
              ffd_w13, ffd_w2, moe_router, moe_w13, moe_w2, ln_ffn_g, ln_ffn_b):
    for l in range(DEPTH):
        lam_init = 0.8 - 0.6 * math.exp(-0.3 * l)
        mix = hybrid_mixer(x, positions, w_in[l], lam_q1[l], lam_k1[l], lam_q2[l], lam_k2[l], lam_init,
                           diff_norm_g[l], gla_wa2[l], gla_ba[l], gla_norm_g[l], w_out[l])
        x = layer_norm(DEEPNORM_ALPHA * x + mix, ln_mix_g[l], ln_mix_b[l])
        xa = memory_cross_attention(x, mem, xa_wq[l], xa_wkv[l], xa_wo[l])
        x = layer_norm(DEEPNORM_ALPHA * x + xa, ln_xa_g[l], ln_xa_b[l])
        if l % 2 == 0:
            f = swiglu(x, ffd_w13[l // 2], ffd_w2[l // 2])
        else:
            f = moe_swiglu(x, moe_router[l // 2], moe_w13[l // 2], moe_w2[l // 2])
        x = layer_norm(DEEPNORM_ALPHA * x + f, ln_ffn_g[l], ln_ffn_b[l])
    return x
```

```python
import functools
import math

import numpy as np
import jax
import jax.numpy as jnp
from jax import lax
from jax.experimental import pallas as pl
from jax.experimental.pallas import tpu as pltpu

F32 = jnp.float32
BF16 = jnp.bfloat16

LANES = 128
VMEM_LIMIT_BYTES = 56 * 1024 * 1024

D_MODEL = 1024
DEPTH = 2
ROPE_THETA = 500000.0
ROPE_FRAC = 4
LN_EPS = 1e-5
NEG_INF = -1e30
ALPHA = (2 * DEPTH) ** 0.25

A_HEADS, A_QK, A_V = 4, 32, 64
B_HEADS, B_DIM = 6, 64
B_SPAN = 128
B_DILATIONS = (1, 4, 16)
C_HEADS, C_K, C_V = 4, 48, 96
C_KP, C_VP = 64, 128
C_RANK, C_TAU, C_CHUNK = 16, 16.0, 16
XA_HEADS, XA_DIM = 4, 256
N_EXPERTS, TOP_K = 8, 2

QA, KA, QB, KB, VA, VB, RC, QC, KC, VC, GC, H_COLS = (
    0, 256, 512, 896, 1280, 1536, 1920, 2048, 2304, 2560, 3072, 3584)
IN_TN = 256
ROPE_A_TILES = (QB - QA) // IN_TN
ROPE_B_TILES = (VA - QA) // IN_TN


def _cparams(sem):
    return pltpu.CompilerParams(dimension_semantics=sem, vmem_limit_bytes=VMEM_LIMIT_BYTES)


def _layer_norm(y, g, b):
    mu = jnp.mean(y, axis=-1, keepdims=True)
    yc = y - mu
    var = jnp.mean(yc * yc, axis=-1, keepdims=True)
    return yc * lax.rsqrt(var + LN_EPS) * g + b


def _dot_nt(a, b):
    return lax.dot_general(a, b, (((1,), (1,)), ((), ())), preferred_element_type=F32)


def _inproj_kernel(x_ref, w_ref, ca_ref, sa_ref, cb_ref, sb_ref, o_ref):
    j = pl.program_id(1)
    acc = jnp.dot(x_ref[...], w_ref[...], preferred_element_type=F32)
    n = acc.shape[1]
    lane = lax.broadcasted_iota(jnp.int32, acc.shape, 1)

    def rope(c_ref, s_ref, half, period):
        c = jnp.concatenate([c_ref[...]] * (n // LANES), axis=1)
        s = jnp.concatenate([s_ref[...]] * (n // LANES), axis=1)
        up = pltpu.roll(acc, n - half, 1)
        dn = pltpu.roll(acc, half, 1)
        rot = jnp.where(lane % period < half, up, dn)
        return acc * c + rot * s

    @pl.when(j < ROPE_A_TILES)
    def _():
        o_ref[...] = rope(ca_ref, sa_ref, A_QK // ROPE_FRAC // 2, A_QK).astype(o_ref.dtype)

    @pl.when((j >= ROPE_A_TILES) & (j < ROPE_B_TILES))
    def _():
        o_ref[...] = rope(cb_ref, sb_ref, B_DIM // ROPE_FRAC // 2, B_DIM).astype(o_ref.dtype)

    @pl.when(j >= ROPE_B_TILES)
    def _():
        o_ref[...] = acc.astype(o_ref.dtype)


def _inproj(xb, w, tabs, tm=1024):
    t = xb.shape[0]
    tab_spec = pl.BlockSpec((tm, LANES), lambda i, j: (i, 0))
    return pl.pallas_call(
        _inproj_kernel,
        out_shape=jax.ShapeDtypeStruct((t, H_COLS), BF16),
        grid=(t // tm, H_COLS // IN_TN),
        in_specs=[pl.BlockSpec((tm, D_MODEL), lambda i, j: (i, 0)),
                  pl.BlockSpec((D_MODEL, IN_TN), lambda i, j: (0, j)),
                  tab_spec, tab_spec, tab_spec, tab_spec],
        out_specs=pl.BlockSpec((tm, IN_TN), lambda i, j: (i, j)),
        compiler_params=_cparams(("parallel", "arbitrary")),
        name="inproj_rope",
    )(xb, w, *tabs)


def _diff_kernel(lam_ref, q_ref, k_ref, v_ref, g_ref, o_ref, *, tq, out_scale):
    i = pl.program_id(2)
    q = q_ref[0]
    lane = lax.broadcasted_iota(jnp.int32, (tq, LANES), 1)
    row = lax.broadcasted_iota(jnp.int32, (tq, tq), 0)
    col = lax.broadcasted_iota(jnp.int32, (tq, tq), 1)
    scale = A_QK ** -0.5
    lam = lam_ref[0, 0]

    def attend(c):
        qm = jnp.where(lane // A_QK == c, q, jnp.zeros_like(q))

        def step(j, carry, masked):
            m, l, acc = carry
            start = pl.multiple_of(j * tq, tq)
            kb = k_ref[0, pl.ds(start, tq), :]
            vb = v_ref[0, pl.ds(start, tq), :]
            s = _dot_nt(qm, kb) * scale
            if masked:
                s = jnp.where(col <= row, s, NEG_INF)
            m_new = jnp.maximum(m, jnp.max(s, axis=-1, keepdims=True))
            a = jnp.exp(m - m_new)
            p = jnp.exp(s - m_new)
            l_new = a * l + jnp.sum(p, axis=-1, keepdims=True)
            acc_new = a * acc + jnp.dot(p.astype(BF16), vb, preferred_element_type=F32)
            return m_new, l_new, acc_new

        init = (jnp.full((tq, 1), NEG_INF, F32), jnp.zeros((tq, 1), F32), jnp.zeros((tq, LANES), F32))
        carry = lax.fori_loop(0, i, lambda j, c_: step(j, c_, False), init)
        m, l, acc = step(i, carry, True)
        return acc / l

    o_h0 = attend(0) - lam * attend(1)
    o_h1 = attend(2) - lam * attend(3)
    lo = lane < A_V
    o = jnp.where(lo, o_h0, o_h1)
    sq = o * o
    s_lo = jnp.sum(jnp.where(lo, sq, 0.0), axis=-1, keepdims=True)
    s_all = jnp.sum(sq, axis=-1, keepdims=True)
    ms = jnp.where(lo, s_lo, s_all - s_lo) * (1.0 / A_V)
    o_ref[0] = (o * lax.rsqrt(ms + LN_EPS) * g_ref[...] * out_scale).astype(o_ref.dtype)


def _diff_attention(h3, lam, g, lam_init, tq=128):
    bsz, seq, _ = h3.shape
    kern = functools.partial(_diff_kernel, tq=tq, out_scale=1.0 - lam_init)
    return pl.pallas_call(
        kern,
        out_shape=jax.ShapeDtypeStruct((bsz, seq, A_HEADS * A_V), BF16),
        grid=(bsz, A_HEADS // 2, seq // tq),
        in_specs=[pl.BlockSpec(memory_space=pltpu.SMEM),
                  pl.BlockSpec((1, tq, LANES), lambda b, p, i: (b, i, QA // LANES + p)),
                  pl.BlockSpec((1, seq, LANES), lambda b, p, i: (b, 0, KA // LANES + p)),
                  pl.BlockSpec((1, seq, LANES), lambda b, p, i: (b, 0, VA // LANES + p)),
                  pl.BlockSpec((1, LANES), lambda b, p, i: (0, 0))],
        out_specs=pl.BlockSpec((1, tq, LANES), lambda b, p, i: (b, i, p)),
        compiler_params=_cparams(("parallel", "parallel", "arbitrary")),
        name="diff_attention",
    )(lam, h3, h3, h3, g)


def _dil_kernel(q_ref, k_ref, v_ref, o_ref, qf, kf, vf, m_s, l_s, a_s, *, seq):
    qf[...] = q_ref[0].astype(F32) * (B_DIM ** -0.5)
    kf[...] = k_ref[0].astype(F32)
    vf[...] = v_ref[0].astype(F32)
    span = B_SPAN
    lane = lax.broadcasted_iota(jnp.int32, (span, LANES), 1)
    lo = lane < B_DIM
    qi = lax.broadcasted_iota(jnp.int32, (span, 2 * span), 0)
    ki = lax.broadcasted_iota(jnp.int32, (span, 2 * span), 1)
    band = (ki >= qi) & (ki <= qi + span)

    for g, r in enumerate(B_DILATIONS):
        nblk = seq // (span * r)

        def body(t, carry, g=g, r=r, nblk=nblk):
            rho = t // nblk
            n = t % nblk
            cur = rho + r * span * n
            prev = rho + r * span * jnp.maximum(n - 1, 0)
            rows_c = pl.ds(cur, span, stride=r) if r > 1 else pl.ds(cur, span)
            rows_p = pl.ds(prev, span, stride=r) if r > 1 else pl.ds(prev, span)
            qb = qf[rows_c, :]
            kcat = jnp.concatenate([kf[rows_p, :], kf[rows_c, :]], axis=0).astype(BF16)
            vcat = jnp.concatenate([vf[rows_p, :], vf[rows_c, :]], axis=0).astype(BF16)
            valid = band & ((ki >= span) | (n > 0))
            stats = []
            for head_lo in (True, False):
                qm = jnp.where(lo == head_lo, qb, 0.0).astype(BF16)
                s = jnp.where(valid, _dot_nt(qm, kcat), NEG_INF)
                mh = jnp.max(s, axis=-1, keepdims=True)
                p = jnp.exp(s - mh)
                lh = jnp.sum(p, axis=-1, keepdims=True)
                ah = jnp.dot(p.astype(BF16), vcat, preferred_element_type=F32)
                stats.append((mh, lh, ah))
            m_g = jnp.where(lo, stats[0][0], stats[1][0])
            l_g = jnp.where(lo, stats[0][1], stats[1][1])
            a_g = jnp.where(lo, stats[0][2], stats[1][2])
            if g == 0:
                m_s[rows_c, :] = m_g
                l_s[rows_c, :] = l_g
                a_s[rows_c, :] = a_g
            else:
                m_o = m_s[rows_c, :]
                m_n = jnp.maximum(m_o, m_g)
                w_o = jnp.exp(m_o - m_n)
                w_g = jnp.exp(m_g - m_n)
                m_s[rows_c, :] = m_n
                l_s[rows_c, :] = w_o * l_s[rows_c, :] + w_g * l_g
                a_s[rows_c, :] = w_o * a_s[rows_c, :] + w_g * a_g
            return carry

        lax.fori_loop(0, seq // span, body, 0)

    o_ref[0] = (a_s[...] / l_s[...]).astype(o_ref.dtype)


def _dilated_attention(h3):
    bsz, seq, _ = h3.shape
    blk = lambda off: pl.BlockSpec((1, seq, LANES), lambda b, p: (b, 0, off // LANES + p))
    return pl.pallas_call(
        functools.partial(_dil_kernel, seq=seq),
        out_shape=jax.ShapeDtypeStruct((bsz, seq, B_HEADS * B_DIM), BF16),
        grid=(bsz, B_HEADS // 2),
        in_specs=[blk(QB), blk(KB), blk(VB)],
        out_specs=pl.BlockSpec((1, seq, LANES), lambda b, p: (b, 0, p)),
        scratch_shapes=[pltpu.VMEM((seq, LANES), F32)] * 6,
        compiler_params=_cparams(("parallel", "parallel")),
        name="dilated_attention",
    )(h3, h3, h3)


def _gla_kernel(q_ref, k_ref, v_ref, gate_ref, rc_ref, wa_ref, ba_ref, gn_ref, o_ref,
                st, b_s, qs_s, k_s, v_s, qt_s, kt_s, gl_s, o_s, *, tb):
    t = pl.program_id(2)

    @pl.when(t == 0)
    def _():
        st[...] = jnp.zeros_like(st)

    ch = C_CHUNK
    z = jnp.dot(rc_ref[0], wa_ref[0], preferred_element_type=F32) + ba_ref[0]
    la = (jnp.minimum(z, 0.0) - jnp.log1p(jnp.exp(-jnp.abs(z)))) * (1.0 / C_TAU)
    ri = lax.broadcasted_iota(jnp.int32, (tb, tb), 0)
    ci = lax.broadcasted_iota(jnp.int32, (tb, tb), 1)
    same = (ri // ch) == (ci // ch)
    tri = jnp.where(same & (ci <= ri), 1.0, 0.0).astype(F32)
    blk = jnp.where(same, 1.0, 0.0).astype(F32)
    b = jnp.dot(tri, la, preferred_element_type=F32, precision=lax.Precision.HIGHEST)
    bl = jnp.dot(blk, la, preferred_element_type=F32, precision=lax.Precision.HIGHEST)
    qs = q_ref[0].astype(F32) * (C_K ** -0.5)
    kf = k_ref[0].astype(F32)
    b_s[...] = b
    qs_s[...] = qs
    k_s[...] = kf
    v_s[...] = v_ref[0].astype(F32)
    qt_s[...] = qs * jnp.exp(b)
    kt_s[...] = kf * jnp.exp(bl - b)
    gl_s[...] = jnp.exp(bl)

    dl = lax.broadcasted_iota(jnp.int32, (LANES, 2 * C_VP), 0)
    vl = lax.broadcasted_iota(jnp.int32, (LANES, 2 * C_VP), 1)
    red = jnp.where(dl // C_KP == vl // C_VP, 1.0, 0.0).astype(BF16)
    sv = lax.broadcasted_iota(jnp.int32, (2 * C_VP, LANES), 0)
    sd = lax.broadcasted_iota(jnp.int32, (2 * C_VP, LANES), 1)
    own = sv // C_VP == sd // C_KP
    rowi = lax.broadcasted_iota(jnp.int32, (ch, LANES), 0)

    def chunk(c, carry):
        rows = pl.ds(pl.multiple_of(c * ch, ch), ch)
        s_prev = st[...]
        o_inter = _dot_nt(qt_s[rows, :].astype(BF16), s_prev.astype(BF16))
        bq = b_s[rows, :]
        qq = qs_s[rows, :]
        kk = k_s[rows, :]
        vv = v_s[rows, :]
        terms = []
        for j in range(ch):
            e = jnp.exp(jnp.where(rowi >= j, bq - bq[j:j + 1, :], NEG_INF))
            terms.append((qq * e * kk[j:j + 1, :]).astype(BF16))
        sc = jnp.dot(jnp.concatenate(terms, axis=0), red, preferred_element_type=F32)
        o_intra = sc[0:ch, :] * vv[0:1, :]
        for j in range(1, ch):
            o_intra = o_intra + sc[j * ch:(j + 1) * ch, :] * vv[j:j + 1, :]
        o_s[rows, :] = o_inter + o_intra
        ut = lax.dot_general(vv.astype(BF16), kt_s[rows, :].astype(BF16), (((0,), (0,)), ((), ())),
                             preferred_element_type=F32)
        st[...] = gl_s[pl.ds(pl.multiple_of(c * ch, ch), 1), :] * s_prev + jnp.where(own, ut, 0.0)
        return carry

    lax.fori_loop(0, tb // ch, chunk, 0)

    o = o_s[...]
    gate = gate_ref[0].astype(F32)
    outs = []
    for hh in range(2):
        oh = o[:, hh * C_VP:(hh + 1) * C_VP]
        ms = jnp.sum(oh * oh, axis=-1, keepdims=True) * (1.0 / C_V)
        gh = gate[:, hh * C_VP:(hh + 1) * C_VP]
        outs.append(oh * lax.rsqrt(ms + LN_EPS) * gn_ref[...] * (gh * jax.nn.sigmoid(gh)))
    o_ref[0] = jnp.concatenate(outs, axis=1).astype(o_ref.dtype)


def _gla(h3, wa, ba, gn, tb=128):
    bsz, seq, _ = h3.shape
    return pl.pallas_call(
        functools.partial(_gla_kernel, tb=tb),
        out_shape=jax.ShapeDtypeStruct((bsz, seq, C_HEADS * C_VP), BF16),
        grid=(bsz, C_HEADS // 2, seq // tb),
        in_specs=[pl.BlockSpec((1, tb, LANES), lambda b, p, t: (b, t, QC // LANES + p)),
                  pl.BlockSpec((1, tb, LANES), lambda b, p, t: (b, t, KC // LANES + p)),
                  pl.BlockSpec((1, tb, 2 * C_VP), lambda b, p, t: (b, t, VC // (2 * C_VP) + p)),
                  pl.BlockSpec((1, tb, 2 * C_VP), lambda b, p, t: (b, t, GC // (2 * C_VP) + p)),
                  pl.BlockSpec((1, tb, LANES), lambda b, p, t: (b, t, RC // LANES)),
                  pl.BlockSpec((1, LANES, LANES), lambda b, p, t: (p, 0, 0)),
                  pl.BlockSpec((1, 1, LANES), lambda b, p, t: (p, 0, 0)),
                  pl.BlockSpec((1, C_VP), lambda b, p, t: (0, 0))],
        out_specs=pl.BlockSpec((1, tb, 2 * C_VP), lambda b, p, t: (b, t, p)),
        scratch_shapes=[pltpu.VMEM((2 * C_VP, LANES), F32)]
                       + [pltpu.VMEM((tb, LANES), F32)] * 3
                       + [pltpu.VMEM((tb, 2 * C_VP), F32)]
                       + [pltpu.VMEM((tb, LANES), F32)] * 3
                       + [pltpu.VMEM((tb, 2 * C_VP), F32)],
        compiler_params=_cparams(("parallel", "parallel", "arbitrary")),
        name="gla",
    )(h3, h3, h3, h3, h3, wa, ba, gn)


def _outproj_kernel(oa_ref, ob_ref, oc_ref, wa_ref, wb_ref, wc_ref, x_ref, g_ref, b_ref, y_ref, yb_ref):
    acc = jnp.dot(oa_ref[...], wa_ref[...], preferred_element_type=F32)
    acc += jnp.dot(ob_ref[...], wb_ref[...], preferred_element_type=F32)
    acc += jnp.dot(oc_ref[...], wc_ref[...], preferred_element_type=F32)
    y = _layer_norm(ALPHA * x_ref[...] + acc, g_ref[...], b_ref[...])
    y_ref[...] = y
    yb_ref[...] = y.astype(BF16)


def _outproj_ln(oa, ob, oc, wa, wb, wc, x, g, b, tm=512):
    t = x.shape[0]
    row = lambda n: pl.BlockSpec((tm, n), lambda i: (i, 0))
    full = lambda a: pl.BlockSpec(a.shape, lambda i: (0, 0))
    return pl.pallas_call(
        _outproj_kernel,
        out_shape=(jax.ShapeDtypeStruct((t, D_MODEL), F32), jax.ShapeDtypeStruct((t, D_MODEL), BF16)),
        grid=(t // tm,),
        in_specs=[row(oa.shape[1]), row(ob.shape[1]), row(oc.shape[1]), full(wa), full(wb), full(wc),
                  row(D_MODEL), full(g), full(b)],
        out_specs=(row(D_MODEL), row(D_MODEL)),
        compiler_params=_cparams(("parallel",)),
        name="outproj_ln",
    )(oa, ob, oc, wa, wb, wc, x, g, b)


def _matmul_kernel(a_ref, w_ref, o_ref):
    o_ref[...] = jnp.dot(a_ref[...], w_ref[...], preferred_element_type=F32).astype(o_ref.dtype)


def _matmul(a, w, out_dtype, tm=256, tn=512):
    m, k = a.shape
    n = w.shape[1]
    return pl.pallas_call(
        _matmul_kernel,
        out_shape=jax.ShapeDtypeStruct((m, n), out_dtype),
        grid=(m // tm, n // tn),
        in_specs=[pl.BlockSpec((tm, k), lambda i, j: (i, 0)), pl.BlockSpec((k, tn), lambda i, j: (0, j))],
        out_specs=pl.BlockSpec((tm, tn), lambda i, j: (i, j)),
        compiler_params=_cparams(("parallel", "arbitrary")),
        name="matmul",
    )(a, w)


def _xattn_kernel(xb_ref, x_ref, wq_ref, k_ref, v_ref, wo_ref, g_ref, b_ref, y_ref, yb_ref):
    q = jnp.dot(xb_ref[0], wq_ref[...], preferred_element_type=F32).astype(BF16)
    outs = []
    for h in range(XA_HEADS):
        cols = slice(h * XA_DIM, (h + 1) * XA_DIM)
        s = _dot_nt(q[:, cols], k_ref[0, :, cols]) * (XA_DIM ** -0.5)
        m = jnp.max(s, axis=-1, keepdims=True)
        p = jnp.exp(s - m)
        l = jnp.sum(p, axis=-1, keepdims=True)
        o = jnp.dot(p.astype(BF16), v_ref[0, :, cols], preferred_element_type=F32)
        outs.append((o / l).astype(BF16))
    o = jnp.concatenate(outs, axis=1)
    xa = jnp.dot(o, wo_ref[...], preferred_element_type=F32)
    y = _layer_norm(ALPHA * x_ref[0] + xa, g_ref[...], b_ref[...])
    y_ref[0] = y
    yb_ref[0] = y.astype(BF16)


def _xattn_ln(xb3, x3, wq, kv3, wo, g, b, tm=512):
    bsz, seq, _ = x3.shape
    mem_len = kv3.shape[1]
    row = pl.BlockSpec((1, tm, D_MODEL), lambda bi, i: (bi, i, 0))
    full = lambda a: pl.BlockSpec(a.shape, lambda bi, i: (0, 0))
    return pl.pallas_call(
        _xattn_kernel,
        out_shape=(jax.ShapeDtypeStruct((bsz, seq, D_MODEL), F32), jax.ShapeDtypeStruct((bsz, seq, D_MODEL), BF16)),
        grid=(bsz, seq // tm),
        in_specs=[row, row, full(wq),
                  pl.BlockSpec((1, mem_len, D_MODEL), lambda bi, i: (bi, 0, 0)),
                  pl.BlockSpec((1, mem_len, D_MODEL), lambda bi, i: (bi, 0, 1)),
                  full(wo), full(g), full(b)],
        out_specs=(row, row),
        compiler_params=_cparams(("parallel", "arbitrary")),
        name="xattn_ln",
    )(xb3, x3, wq, kv3, kv3, wo, g, b)


def _ffn_kernel(xb_ref, x_ref, gates_ref, w1_ref, w3_ref, w2_ref, g_ref, b_ref, y_ref, yb_ref, acc_ref,
                *, n_exp):
    e = pl.program_id(1)
    j = pl.program_id(2)

    @pl.when((e == 0) & (j == 0))
    def _():
        acc_ref[...] = jnp.zeros_like(acc_ref)

    xb = xb_ref[...]
    hg = jnp.dot(xb, w1_ref[0], preferred_element_type=F32)
    hu = jnp.dot(xb, w3_ref[0], preferred_element_type=F32)
    a = hg * jax.nn.sigmoid(hg) * hu
    if n_exp > 1:
        lane = lax.broadcasted_iota(jnp.int32, gates_ref.shape, 1)
        ge = jnp.sum(jnp.where(lane == e, gates_ref[...], 0.0), axis=-1, keepdims=True)
        a = a * ge
    acc_ref[...] += jnp.dot(a.astype(BF16), w2_ref[0], preferred_element_type=F32)

    @pl.when((e == n_exp - 1) & (j == pl.num_programs(2) - 1))
    def _():
        y = _layer_norm(ALPHA * x_ref[...] + acc_ref[...], g_ref[...], b_ref[...])
        y_ref[...] = y
        yb_ref[...] = y.astype(BF16)


def _ffn_ln(xb, x, gates, w13, w2, g, b, tm, tf):
    t = x.shape[0]
    n_exp, f = w2.shape[0], w2.shape[1]
    nf = f // tf
    row = pl.BlockSpec((tm, D_MODEL), lambda i, e, j: (i, 0))
    vec = lambda a: pl.BlockSpec(a.shape, lambda i, e, j: (0, 0))
    return pl.pallas_call(
        functools.partial(_ffn_kernel, n_exp=n_exp),
        out_shape=(jax.ShapeDtypeStruct((t, D_MODEL), F32), jax.ShapeDtypeStruct((t, D_MODEL), BF16)),
        grid=(t // tm, n_exp, nf),
        in_specs=[row, row,
                  pl.BlockSpec((tm, LANES), lambda i, e, j: (i, 0)),
                  pl.BlockSpec((1, D_MODEL, tf), lambda i, e, j: (e, 0, j)),
                  pl.BlockSpec((1, D_MODEL, tf), lambda i, e, j: (e, 0, j + nf)),
                  pl.BlockSpec((1, tf, D_MODEL), lambda i, e, j: (e, j, 0)),
                  vec(g), vec(b)],
        out_specs=(row, row),
        scratch_shapes=[pltpu.VMEM((tm, D_MODEL), F32)],
        compiler_params=_cparams(("parallel", "arbitrary", "arbitrary")),
        name="ffn_ln",
    )(xb, x, gates, w13, w13, w2, g, b)


def _router_kernel(x_ref, w_ref, gates_ref):
    logits = jnp.dot(x_ref[...], w_ref[...], preferred_element_type=F32, precision=lax.Precision.HIGHEST)
    lane = lax.broadcasted_iota(jnp.int32, logits.shape, 1)
    logits = jnp.where(lane < N_EXPERTS, logits, -jnp.inf)
    v1 = jnp.max(logits, axis=-1, keepdims=True)
    i1 = jnp.min(jnp.where(logits == v1, lane, LANES), axis=-1, keepdims=True)
    rest = jnp.where(lane == i1, -jnp.inf, logits)
    v2 = jnp.max(rest, axis=-1, keepdims=True)
    i2 = jnp.min(jnp.where(rest == v2, lane, LANES), axis=-1, keepdims=True)
    e2 = jnp.exp(v2 - v1)
    w1 = 1.0 / (1.0 + e2)
    w2 = e2 / (1.0 + e2)
    gates_ref[...] = jnp.where(lane == i1, w1, 0.0) + jnp.where(lane == i2, w2, 0.0)


def _router(x, w, tm=1024):
    t = x.shape[0]
    return pl.pallas_call(
        _router_kernel,
        out_shape=jax.ShapeDtypeStruct((t, LANES), F32),
        grid=(t // tm,),
        in_specs=[pl.BlockSpec((tm, D_MODEL), lambda i: (i, 0)), pl.BlockSpec(w.shape, lambda i: (0, 0))],
        out_specs=pl.BlockSpec((tm, LANES), lambda i: (i, 0)),
        compiler_params=_cparams(("parallel",)),
        name="router",
    )(x, w)


def _pad_heads(w, heads, width, padded, axis):
    shape = list(w.shape)
    shape[axis:axis + 1] = [heads, width]
    w = w.reshape(shape)
    pad = [(0, 0)] * w.ndim
    pad[axis + 1] = (0, padded - width)
    w = jnp.pad(w, pad)
    shape[axis:axis + 2] = [heads * padded]
    return w.reshape(shape)


def _layout_w_in(w):
    bounds = np.cumsum([0, 256, 256, 256, 384, 384, 384, 192, 192, 384, 384, 16])
    qa, ka, va, qb, kb, vb, qc, kc, vc, gc, rc = [w[:, bounds[i]:bounds[i + 1]] for i in range(11)]
    rc = jnp.pad(rc, ((0, 0), (0, LANES - C_RANK)))
    qc = _pad_heads(qc, C_HEADS, C_K, C_KP, 1)
    kc = _pad_heads(kc, C_HEADS, C_K, C_KP, 1)
    vc = _pad_heads(vc, C_HEADS, C_V, C_VP, 1)
    gc = _pad_heads(gc, C_HEADS, C_V, C_VP, 1)
    return jnp.concatenate([qa, ka, qb, kb, va, vb, rc, qc, kc, vc, gc], axis=1).astype(BF16)


def _rope_tables(positions, dim):
    rot = dim // ROPE_FRAC
    inv_freq = ROPE_THETA ** (-jnp.arange(0, rot, 2, dtype=F32) / rot)
    ang = positions.astype(F32).reshape(-1, 1) * inv_freq
    cos, sin = jnp.cos(ang), jnp.sin(ang)
    t = ang.shape[0]
    c = jnp.concatenate([cos, cos, jnp.ones((t, dim - rot), F32)], axis=1)
    s = jnp.concatenate([-sin, sin, jnp.zeros((t, dim - rot), F32)], axis=1)
    reps = LANES // dim
    return jnp.tile(c, (1, reps)), jnp.tile(s, (1, reps))


def kernel(x, mem, positions, w_in, lam_q1, lam_k1, lam_q2, lam_k2, diff_norm_g, gla_wa2, gla_ba, gla_norm_g,
           w_out, ln_mix_g, ln_mix_b, xa_wq, xa_wkv, xa_wo, ln_xa_g, ln_xa_b, ffd_w13, ffd_w2, moe_router,
           moe_w13, moe_w2, ln_ffn_g, ln_ffn_b):
    bsz, seq, _ = x.shape
    t = bsz * seq
    ca, sa = _rope_tables(positions, A_QK)
    cb, sb = _rope_tables(positions, B_DIM)
    tabs = (ca, sa, cb, sb)
    memb = mem.reshape(-1, D_MODEL).astype(BF16)
    mem_len = mem.shape[1]
    row = lambda v: v.reshape(1, -1).astype(F32)
    unit_gates = jnp.ones((t, LANES), F32)

    xf = x.reshape(t, D_MODEL)
    xb = xf.astype(BF16)
    for l in range(DEPTH):
        lam_init = 0.8 - 0.6 * math.exp(-0.3 * l)
        lam = (jnp.exp(jnp.sum(lam_q1[l] * lam_k1[l])) - jnp.exp(jnp.sum(lam_q2[l] * lam_k2[l]))
               + lam_init).reshape(1, 1).astype(F32)
        h = _inproj(xb, _layout_w_in(w_in[l]), tabs)
        h3 = h.reshape(bsz, seq, H_COLS)
        oa = _diff_attention(h3, lam, jnp.tile(row(diff_norm_g[l]), (1, 2)), lam_init)
        ob = _dilated_attention(h3)
        wa = _pad_heads(gla_wa2[l], C_HEADS, C_K, C_KP, 1)
        wa = jnp.pad(wa, ((0, LANES - C_RANK), (0, 0))).reshape(LANES, 2, LANES).transpose(1, 0, 2).astype(BF16)
        ba = _pad_heads(gla_ba[l], C_HEADS, C_K, C_KP, 0).reshape(2, 1, LANES).astype(F32)
        gn = jnp.pad(row(gla_norm_g[l]), ((0, 0), (0, C_VP - C_V)))
        oc = _gla(h3, wa, ba, gn)
        wo = w_out[l]
        wo_a = wo[:A_HEADS * A_V].astype(BF16)
        wo_b = wo[A_HEADS * A_V:A_HEADS * A_V + B_HEADS * B_DIM].astype(BF16)
        wo_c = _pad_heads(wo[A_HEADS * A_V + B_HEADS * B_DIM:], C_HEADS, C_V, C_VP, 0).astype(BF16)
        xf, xb = _outproj_ln(oa.reshape(t, -1), ob.reshape(t, -1), oc.reshape(t, -1), wo_a, wo_b, wo_c,
                             xf, row(ln_mix_g[l]), row(ln_mix_b[l]))

        kv = _matmul(memb, xa_wkv[l].astype(BF16), BF16)
        x3, xb3 = _xattn_ln(xb.reshape(bsz, seq, D_MODEL), xf.reshape(bsz, seq, D_MODEL), xa_wq[l].astype(BF16),
                            kv.reshape(bsz, mem_len, 2 * D_MODEL), xa_wo[l].astype(BF16),
                            row(ln_xa_g[l]), row(ln_xa_b[l]))
        xf, xb = x3.reshape(t, D_MODEL), xb3.reshape(t, D_MODEL)

        if l % 2 == 0:
            xf, xb = _ffn_ln(xb, xf, unit_gates, ffd_w13[l // 2][None].astype(BF16), ffd_w2[l // 2][None].astype(BF16),
                             row(ln_ffn_g[l]), row(ln_ffn_b[l]), tm=512, tf=256)
        else:
            rw = jnp.pad(moe_router[l // 2], ((0, 0), (0, LANES - N_EXPERTS)))
            gates = _router(xf, rw)
            xf, xb = _ffn_ln(xb, xf, gates, moe_w13[l // 2].astype(BF16), moe_w2[l // 2].astype(BF16),
                             row(ln_ffn_g[l]), row(ln_ffn_b[l]), tm=512, tf=512)
    return xf.reshape(bsz, seq, D_MODEL)
```

```python
import functools
import math

import numpy as np
import jax
import jax.numpy as jnp
from jax import lax
from jax.experimental import pallas as pl
from jax.experimental.pallas import tpu as pltpu

F32 = jnp.float32
BF16 = jnp.bfloat16

LANES = 128
VMEM_LIMIT_BYTES = 56 * 1024 * 1024

D_MODEL = 1024
DEPTH = 2
ROPE_THETA = 500000.0
ROPE_FRAC = 4
LN_EPS = 1e-5
NEG_INF = -1e30
ALPHA = (2 * DEPTH) ** 0.25

A_HEADS, A_QK, A_V = 4, 32, 64
B_HEADS, B_DIM = 6, 64
B_SPAN = 128
B_DILATIONS = (1, 4, 16)
C_HEADS, C_K, C_V = 4, 48, 96
C_KP, C_VP = 64, 128
C_RANK, C_TAU, C_CHUNK = 16, 16.0, 16
XA_HEADS, XA_DIM = 4, 256
N_EXPERTS, TOP_K = 8, 2

QA, KA, QB, KB, VA, VB, RC, QC, KC, VC, GC, H_COLS = (
    0, 256, 512, 896, 1280, 1536, 1920, 2048, 2304, 2560, 3072, 3584)
IN_TN = 256
ROPE_A_TILES = (QB - QA) // IN_TN
ROPE_B_TILES = (VA - QA) // IN_TN


def _cparams(sem):
    return pltpu.CompilerParams(dimension_semantics=sem, vmem_limit_bytes=VMEM_LIMIT_BYTES)


def _layer_norm(y, g, b):
    mu = jnp.mean(y, axis=-1, keepdims=True)
    yc = y - mu
    var = jnp.mean(yc * yc, axis=-1, keepdims=True)
    return yc * lax.rsqrt(var + LN_EPS) * g + b


def _dot_nt(a, b):
    return lax.dot_general(a, b, (((1,), (1,)), ((), ())), preferred_element_type=F32)


def _inproj_kernel(x_ref, w_ref, ca_ref, sa_ref, cb_ref, sb_ref, o_ref):
    x = x_ref[...]
    lane = lax.broadcasted_iota(jnp.int32, (x.shape[0], LANES), 1)

    def rope_cols(lo, hi, c_ref, s_ref, half, period):
        acc = jnp.dot(x, w_ref[:, lo:hi], preferred_element_type=F32)
        c, s = c_ref[...], s_ref[...]
        first = lane % period < half
        for k in range((hi - lo) // LANES):
            a = acc[:, k * LANES:(k + 1) * LANES]
            up = pltpu.roll(a, LANES - half, 1)
            dn = pltpu.roll(a, half, 1)
            o_ref[:, lo + k * LANES:lo + (k + 1) * LANES] = (a * c + jnp.where(first, up, dn) * s).astype(o_ref.dtype)

    rope_cols(QA, QB, ca_ref, sa_ref, A_QK // ROPE_FRAC // 2, A_QK)
    rope_cols(QB, VA, cb_ref, sb_ref, B_DIM // ROPE_FRAC // 2, B_DIM)
    o_ref[:, VA:] = jnp.dot(x, w_ref[:, VA:], preferred_element_type=F32).astype(o_ref.dtype)


def _inproj(xb, w, tabs, tm=512):
    t = xb.shape[0]
    tab_spec = pl.BlockSpec((tm, LANES), lambda i: (i, 0))
    return pl.pallas_call(
        _inproj_kernel,
        out_shape=jax.ShapeDtypeStruct((t, H_COLS), BF16),
        grid=(t // tm,),
        in_specs=[pl.BlockSpec((tm, D_MODEL), lambda i: (i, 0)),
                  pl.BlockSpec((D_MODEL, H_COLS), lambda i: (0, 0)),
                  tab_spec, tab_spec, tab_spec, tab_spec],
        out_specs=pl.BlockSpec((tm, H_COLS), lambda i: (i, 0)),
        compiler_params=_cparams(("parallel",)),
        name="inproj_rope",
    )(xb, w, *tabs)


def _diff_kernel(lam_ref, q_ref, k_ref, v_ref, g_ref, o_ref, *, seq, tq, out_scale):
    lane = lax.broadcasted_iota(jnp.int32, (tq, LANES), 1)
    row = lax.broadcasted_iota(jnp.int32, (tq, tq), 0)
    col = lax.broadcasted_iota(jnp.int32, (tq, tq), 1)
    causal = col <= row
    lo = lane < A_V
    scale = A_QK ** -0.5
    lam = lam_ref[0, 0]

    for i in range(seq // tq):
        past = i * tq
        q = q_ref[0, past:past + tq, :]
        k_d = k_ref[0, past:past + tq, :]
        v_d = v_ref[0, past:past + tq, :]

        def attend(c):
            qm = jnp.where(lane // A_QK == c, q, jnp.zeros_like(q))
            s_d = jnp.where(causal, _dot_nt(qm, k_d) * scale, NEG_INF)
            m = jnp.max(s_d, axis=-1, keepdims=True)
            if past:
                s_p = _dot_nt(qm, k_ref[0, :past, :]) * scale
                m = jnp.maximum(m, jnp.max(s_p, axis=-1, keepdims=True))
            p_d = jnp.exp(s_d - m)
            l = jnp.sum(p_d, axis=-1, keepdims=True)
            acc = jnp.dot(p_d.astype(BF16), v_d, preferred_element_type=F32)
            if past:
                p_p = jnp.exp(s_p - m)
                l = l + jnp.sum(p_p, axis=-1, keepdims=True)
                acc = acc + jnp.dot(p_p.astype(BF16), v_ref[0, :past, :], preferred_element_type=F32)
            return acc / l

        o_h0 = attend(0) - lam * attend(1)
        o_h1 = attend(2) - lam * attend(3)
        o = jnp.where(lo, o_h0, o_h1)
        sq = o * o
        s_lo = jnp.sum(jnp.where(lo, sq, 0.0), axis=-1, keepdims=True)
        s_all = jnp.sum(sq, axis=-1, keepdims=True)
        ms = jnp.where(lo, s_lo, s_all - s_lo) * (1.0 / A_V)
        o_ref[0, past:past + tq, :] = (o * lax.rsqrt(ms + LN_EPS) * g_ref[...] * out_scale).astype(o_ref.dtype)


def _diff_attention(h3, lam, g, lam_init, tq=512):
    bsz, seq, _ = h3.shape
    kern = functools.partial(_diff_kernel, seq=seq, tq=tq, out_scale=1.0 - lam_init)
    blk = lambda off: pl.BlockSpec((1, seq, LANES), lambda b, p: (b, 0, off // LANES + p))
    return pl.pallas_call(
        kern,
        out_shape=jax.ShapeDtypeStruct((bsz, seq, A_HEADS * A_V), BF16),
        grid=(bsz, A_HEADS // 2),
        in_specs=[pl.BlockSpec(memory_space=pltpu.SMEM), blk(QA), blk(KA), blk(VA),
                  pl.BlockSpec((1, LANES), lambda b, p: (0, 0))],
        out_specs=pl.BlockSpec((1, seq, LANES), lambda b, p: (b, 0, p)),
        compiler_params=_cparams(("parallel", "parallel")),
        name="diff_attention",
    )(lam, h3, h3, h3, g)


def _dil_kernel(q_ref, k_ref, v_ref, o_ref, qf, kf, vf, m_s, l_s, a_s, *, seq):
    qf[...] = q_ref[0].astype(F32) * (B_DIM ** -0.5)
    kf[...] = k_ref[0].astype(F32)
    vf[...] = v_ref[0].astype(F32)
    span = B_SPAN
    lane = lax.broadcasted_iota(jnp.int32, (span, LANES), 1)
    lo = lane < B_DIM
    qi = lax.broadcasted_iota(jnp.int32, (span, 2 * span), 0)
    ki = lax.broadcasted_iota(jnp.int32, (span, 2 * span), 1)
    band = (ki >= qi) & (ki <= qi + span)

    for g, r in enumerate(B_DILATIONS):
        nblk = seq // (span * r)

        def body(t, carry, g=g, r=r, nblk=nblk):
            rho = t // nblk
            n = t % nblk
            cur = rho + r * span * n
            prev = rho + r * span * jnp.maximum(n - 1, 0)
            rows_c = pl.ds(cur, span, stride=r) if r > 1 else pl.ds(cur, span)
            rows_p = pl.ds(prev, span, stride=r) if r > 1 else pl.ds(prev, span)
            qb = qf[rows_c, :]
            kcat = jnp.concatenate([kf[rows_p, :], kf[rows_c, :]], axis=0).astype(BF16)
            vcat = jnp.concatenate([vf[rows_p, :], vf[rows_c, :]], axis=0).astype(BF16)
            valid = band & ((ki >= span) | (n > 0))
            stats = []
            for head_lo in (True, False):
                qm = jnp.where(lo == head_lo, qb, 0.0).astype(BF16)
                s = jnp.where(valid, _dot_nt(qm, kcat), NEG_INF)
                mh = jnp.max(s, axis=-1, keepdims=True)
                p = jnp.exp(s - mh)
                lh = jnp.sum(p, axis=-1, keepdims=True)
                ah = jnp.dot(p.astype(BF16), vcat, preferred_element_type=F32)
                stats.append((mh, lh, ah))
            m_g = jnp.where(lo, stats[0][0], stats[1][0])
            l_g = jnp.where(lo, stats[0][1], stats[1][1])
            a_g = jnp.where(lo, stats[0][2], stats[1][2])
            if g == 0:
                m_s[rows_c, :] = m_g
                l_s[rows_c, :] = l_g
                a_s[rows_c, :] = a_g
            else:
                m_o = m_s[rows_c, :]
                m_n = jnp.maximum(m_o, m_g)
                w_o = jnp.exp(m_o - m_n)
                w_g = jnp.exp(m_g - m_n)
                m_s[rows_c, :] = m_n
                l_s[rows_c, :] = w_o * l_s[rows_c, :] + w_g * l_g
                a_s[rows_c, :] = w_o * a_s[rows_c, :] + w_g * a_g
            return carry

        lax.fori_loop(0, seq // span, body, 0)

    o_ref[0] = (a_s[...] / l_s[...]).astype(o_ref.dtype)


def _dilated_attention(h3):
    bsz, seq, _ = h3.shape
    blk = lambda off: pl.BlockSpec((1, seq, LANES), lambda b, p: (b, 0, off // LANES + p))
    return pl.pallas_call(
        functools.partial(_dil_kernel, seq=seq),
        out_shape=jax.ShapeDtypeStruct((bsz, seq, B_HEADS * B_DIM), BF16),
        grid=(bsz, B_HEADS // 2),
        in_specs=[blk(QB), blk(KB), blk(VB)],
        out_specs=pl.BlockSpec((1, seq, LANES), lambda b, p: (b, 0, p)),
        scratch_shapes=[pltpu.VMEM((seq, LANES), F32)] * 6,
        compiler_params=_cparams(("parallel", "parallel")),
        name="dilated_attention",
    )(h3, h3, h3)


def _gla_kernel(q_ref, k_ref, v_ref, gate_ref, rc_ref, wa_ref, ba_ref, gn_ref, o_ref,
                st, b_s, qs_s, k_s, v_s, qt_s, kt_s, gl_s, o_s, *, tb):
    t = pl.program_id(2)

    @pl.when(t == 0)
    def _():
        st[...] = jnp.zeros_like(st)

    ch = C_CHUNK
    z = jnp.dot(rc_ref[0], wa_ref[0], preferred_element_type=F32) + ba_ref[0]
    la = (jnp.minimum(z, 0.0) - jnp.log1p(jnp.exp(-jnp.abs(z)))) * (1.0 / C_TAU)
    ri = lax.broadcasted_iota(jnp.int32, (tb, tb), 0)
    ci = lax.broadcasted_iota(jnp.int32, (tb, tb), 1)
    same = (ri // ch) == (ci // ch)
    tri = jnp.where(same & (ci <= ri), 1.0, 0.0).astype(F32)
    blk = jnp.where(same, 1.0, 0.0).astype(F32)
    b = jnp.dot(tri, la, preferred_element_type=F32, precision=lax.Precision.HIGHEST)
    bl = jnp.dot(blk, la, preferred_element_type=F32, precision=lax.Precision.HIGHEST)
    qs = q_ref[0].astype(F32) * (C_K ** -0.5)
    kf = k_ref[0].astype(F32)
    b_s[...] = b
    qs_s[...] = qs
    k_s[...] = kf
    v_s[...] = v_ref[0].astype(F32)
    qt_s[...] = qs * jnp.exp(b)
    kt_s[...] = kf * jnp.exp(bl - b)
    gl_s[...] = jnp.exp(bl)

    dl = lax.broadcasted_iota(jnp.int32, (LANES, 2 * C_VP), 0)
    vl = lax.broadcasted_iota(jnp.int32, (LANES, 2 * C_VP), 1)
    red = jnp.where(dl // C_KP == vl // C_VP, 1.0, 0.0).astype(BF16)
    sv = lax.broadcasted_iota(jnp.int32, (2 * C_VP, LANES), 0)
    sd = lax.broadcasted_iota(jnp.int32, (2 * C_VP, LANES), 1)
    own = sv // C_VP == sd // C_KP
    rowi = lax.broadcasted_iota(jnp.int32, (ch, LANES), 0)

    def chunk(c, carry):
        rows = pl.ds(pl.multiple_of(c * ch, ch), ch)
        s_prev = st[...]
        o_inter = _dot_nt(qt_s[rows, :].astype(BF16), s_prev.astype(BF16))
        bq = b_s[rows, :]
        qq = qs_s[rows, :]
        kk = k_s[rows, :]
        vv = v_s[rows, :]
        terms = []
        for j in range(ch):
            e = jnp.exp(jnp.where(rowi >= j, bq - bq[j:j + 1, :], NEG_INF))
            terms.append((qq * e * kk[j:j + 1, :]).astype(BF16))
        sc = jnp.dot(jnp.concatenate(terms, axis=0), red, preferred_element_type=F32)
        o_intra = sc[0:ch, :] * vv[0:1, :]
        for j in range(1, ch):
            o_intra = o_intra + sc[j * ch:(j + 1) * ch, :] * vv[j:j + 1, :]
        o_s[rows, :] = o_inter + o_intra
        ut = lax.dot_general(vv.astype(BF16), kt_s[rows, :].astype(BF16), (((0,), (0,)), ((), ())),
                             preferred_element_type=F32)
        st[...] = gl_s[pl.ds(pl.multiple_of(c * ch, ch), 1), :] * s_prev + jnp.where(own, ut, 0.0)
        return carry

    lax.fori_loop(0, tb // ch, chunk, 0)

    o = o_s[...]
    gate = gate_ref[0].astype(F32)
    outs = []
    for hh in range(2):
        oh = o[:, hh * C_VP:(hh + 1) * C_VP]
        ms = jnp.sum(oh * oh, axis=-1, keepdims=True) * (1.0 / C_V)
        gh = gate[:, hh * C_VP:(hh + 1) * C_VP]
        outs.append(oh * lax.rsqrt(ms + LN_EPS) * gn_ref[...] * (gh * jax.nn.sigmoid(gh)))
    o_ref[0] = jnp.concatenate(outs, axis=1).astype(o_ref.dtype)


def _gla(h3, wa, ba, gn, tb=128):
    bsz, seq, _ = h3.shape
    return pl.pallas_call(
        functools.partial(_gla_kernel, tb=tb),
        out_shape=jax.ShapeDtypeStruct((bsz, seq, C_HEADS * C_VP), BF16),
        grid=(bsz, C_HEADS // 2, seq // tb),
        in_specs=[pl.BlockSpec((1, tb, LANES), lambda b, p, t: (b, t, QC // LANES + p)),
                  pl.BlockSpec((1, tb, LANES), lambda b, p, t: (b, t, KC // LANES + p)),
                  pl.BlockSpec((1, tb, 2 * C_VP), lambda b, p, t: (b, t, VC // (2 * C_VP) + p)),
                  pl.BlockSpec((1, tb, 2 * C_VP), lambda b, p, t: (b, t, GC // (2 * C_VP) + p)),
                  pl.BlockSpec((1, tb, LANES), lambda b, p, t: (b, t, RC // LANES)),
                  pl.BlockSpec((1, LANES, LANES), lambda b, p, t: (p, 0, 0)),
                  pl.BlockSpec((1, 1, LANES), lambda b, p, t: (p, 0, 0)),
                  pl.BlockSpec((1, C_VP), lambda b, p, t: (0, 0))],
        out_specs=pl.BlockSpec((1, tb, 2 * C_VP), lambda b, p, t: (b, t, p)),
        scratch_shapes=[pltpu.VMEM((2 * C_VP, LANES), F32)]
                       + [pltpu.VMEM((tb, LANES), F32)] * 3
                       + [pltpu.VMEM((tb, 2 * C_VP), F32)]
                       + [pltpu.VMEM((tb, LANES), F32)] * 3
                       + [pltpu.VMEM((tb, 2 * C_VP), F32)],
        compiler_params=_cparams(("parallel", "parallel", "arbitrary")),
        name="gla",
    )(h3, h3, h3, h3, h3, wa, ba, gn)


def _outproj_kernel(oa_ref, ob_ref, oc_ref, wa_ref, wb_ref, wc_ref, x_ref, g_ref, b_ref, y_ref, yb_ref):
    acc = jnp.dot(oa_ref[...], wa_ref[...], preferred_element_type=F32)
    acc += jnp.dot(ob_ref[...], wb_ref[...], preferred_element_type=F32)
    acc += jnp.dot(oc_ref[...], wc_ref[...], preferred_element_type=F32)
    y = _layer_norm(ALPHA * x_ref[...] + acc, g_ref[...], b_ref[...])
    y_ref[...] = y
    yb_ref[...] = y.astype(BF16)


def _outproj_ln(oa, ob, oc, wa, wb, wc, x, g, b, tm=512):
    t = x.shape[0]
    row = lambda n: pl.BlockSpec((tm, n), lambda i: (i, 0))
    full = lambda a: pl.BlockSpec(a.shape, lambda i: (0, 0))
    return pl.pallas_call(
        _outproj_kernel,
        out_shape=(jax.ShapeDtypeStruct((t, D_MODEL), F32), jax.ShapeDtypeStruct((t, D_MODEL), BF16)),
        grid=(t // tm,),
        in_specs=[row(oa.shape[1]), row(ob.shape[1]), row(oc.shape[1]), full(wa), full(wb), full(wc),
                  row(D_MODEL), full(g), full(b)],
        out_specs=(row(D_MODEL), row(D_MODEL)),
        compiler_params=_cparams(("parallel",)),
        name="outproj_ln",
    )(oa, ob, oc, wa, wb, wc, x, g, b)


def _matmul_kernel(a_ref, w_ref, o_ref):
    o_ref[...] = jnp.dot(a_ref[...], w_ref[...], preferred_element_type=F32).astype(o_ref.dtype)


def _matmul(a, w, out_dtype, tm=256, tn=512):
    m, k = a.shape
    n = w.shape[1]
    return pl.pallas_call(
        _matmul_kernel,
        out_shape=jax.ShapeDtypeStruct((m, n), out_dtype),
        grid=(m // tm, n // tn),
        in_specs=[pl.BlockSpec((tm, k), lambda i, j: (i, 0)), pl.BlockSpec((k, tn), lambda i, j: (0, j))],
        out_specs=pl.BlockSpec((tm, tn), lambda i, j: (i, j)),
        compiler_params=_cparams(("parallel", "arbitrary")),
        name="matmul",
    )(a, w)


def _xattn_kernel(xb_ref, x_ref, wq_ref, k_ref, v_ref, wo_ref, g_ref, b_ref, y_ref, yb_ref):
    q = jnp.dot(xb_ref[0], wq_ref[...], preferred_element_type=F32).astype(BF16)
    outs = []
    for h in range(XA_HEADS):
        cols = slice(h * XA_DIM, (h + 1) * XA_DIM)
        s = _dot_nt(q[:, cols], k_ref[0, :, cols]) * (XA_DIM ** -0.5)
        m = jnp.max(s, axis=-1, keepdims=True)
        p = jnp.exp(s - m)
        l = jnp.sum(p, axis=-1, keepdims=True)
        o = jnp.dot(p.astype(BF16), v_ref[0, :, cols], preferred_element_type=F32)
        outs.append((o / l).astype(BF16))
    o = jnp.concatenate(outs, axis=1)
    xa = jnp.dot(o, wo_ref[...], preferred_element_type=F32)
    y = _layer_norm(ALPHA * x_ref[0] + xa, g_ref[...], b_ref[...])
    y_ref[0] = y
    yb_ref[0] = y.astype(BF16)


def _xattn_ln(xb3, x3, wq, kv3, wo, g, b, tm=512):
    bsz, seq, _ = x3.shape
    mem_len = kv3.shape[1]
    row = pl.BlockSpec((1, tm, D_MODEL), lambda bi, i: (bi, i, 0))
    full = lambda a: pl.BlockSpec(a.shape, lambda bi, i: (0, 0))
    return pl.pallas_call(
        _xattn_kernel,
        out_shape=(jax.ShapeDtypeStruct((bsz, seq, D_MODEL), F32), jax.ShapeDtypeStruct((bsz, seq, D_MODEL), BF16)),
        grid=(bsz, seq // tm),
        in_specs=[row, row, full(wq),
                  pl.BlockSpec((1, mem_len, D_MODEL), lambda bi, i: (bi, 0, 0)),
                  pl.BlockSpec((1, mem_len, D_MODEL), lambda bi, i: (bi, 0, 1)),
                  full(wo), full(g), full(b)],
        out_specs=(row, row),
        compiler_params=_cparams(("parallel", "arbitrary")),
        name="xattn_ln",
    )(xb3, x3, wq, kv3, kv3, wo, g, b)


def _ffn_kernel(xb_ref, x_ref, gates_ref, w1_ref, w3_ref, w2_ref, g_ref, b_ref, y_ref, yb_ref, acc_ref,
                *, n_exp):
    e = pl.program_id(1)
    j = pl.program_id(2)

    @pl.when((e == 0) & (j == 0))
    def _():
        acc_ref[...] = jnp.zeros_like(acc_ref)

    xb = xb_ref[...]
    hg = jnp.dot(xb, w1_ref[0], preferred_element_type=F32)
    hu = jnp.dot(xb, w3_ref[0], preferred_element_type=F32)
    a = hg * jax.nn.sigmoid(hg) * hu
    if n_exp > 1:
        lane = lax.broadcasted_iota(jnp.int32, gates_ref.shape, 1)
        ge = jnp.sum(jnp.where(lane == e, gates_ref[...], 0.0), axis=-1, keepdims=True)
        a = a * ge
    acc_ref[...] += jnp.dot(a.astype(BF16), w2_ref[0], preferred_element_type=F32)

    @pl.when((e == n_exp - 1) & (j == pl.num_programs(2) - 1))
    def _():
        y = _layer_norm(ALPHA * x_ref[...] + acc_ref[...], g_ref[...], b_ref[...])
        y_ref[...] = y
        yb_ref[...] = y.astype(BF16)


def _ffn_ln(xb, x, gates, w13, w2, g, b, tm, tf):
    t = x.shape[0]
    n_exp, f = w2.shape[0], w2.shape[1]
    nf = f // tf
    row = pl.BlockSpec((tm, D_MODEL), lambda i, e, j: (i, 0))
    vec = lambda a: pl.BlockSpec(a.shape, lambda i, e, j: (0, 0))
    return pl.pallas_call(
        functools.partial(_ffn_kernel, n_exp=n_exp),
        out_shape=(jax.ShapeDtypeStruct((t, D_MODEL), F32), jax.ShapeDtypeStruct((t, D_MODEL), BF16)),
        grid=(t // tm, n_exp, nf),
        in_specs=[row, row,
                  pl.BlockSpec((tm, LANES), lambda i, e, j: (i, 0)),
                  pl.BlockSpec((1, D_MODEL, tf), lambda i, e, j: (e, 0, j)),
                  pl.BlockSpec((1, D_MODEL, tf), lambda i, e, j: (e, 0, j + nf)),
                  pl.BlockSpec((1, tf, D_MODEL), lambda i, e, j: (e, j, 0)),
                  vec(g), vec(b)],
        out_specs=(row, row),
        scratch_shapes=[pltpu.VMEM((tm, D_MODEL), F32)],
        compiler_params=_cparams(("parallel", "arbitrary", "arbitrary")),
        name="ffn_ln",
    )(xb, x, gates, w13, w13, w2, g, b)


def _router_kernel(x_ref, w_ref, gates_ref):
    logits = jnp.dot(x_ref[...], w_ref[...], preferred_element_type=F32, precision=lax.Precision.HIGHEST)
    lane = lax.broadcasted_iota(jnp.int32, logits.shape, 1)
    logits = jnp.where(lane < N_EXPERTS, logits, -jnp.inf)
    v1 = jnp.max(logits, axis=-1, keepdims=True)
    i1 = jnp.min(jnp.where(logits == v1, lane, LANES), axis=-1, keepdims=True)
    rest = jnp.where(lane == i1, -jnp.inf, logits)
    v2 = jnp.max(rest, axis=-1, keepdims=True)
    i2 = jnp.min(jnp.where(rest == v2, lane, LANES), axis=-1, keepdims=True)
    e2 = jnp.exp(v2 - v1)
    w1 = 1.0 / (1.0 + e2)
    w2 = e2 / (1.0 + e2)
    gates_ref[...] = jnp.where(lane == i1, w1, 0.0) + jnp.where(lane == i2, w2, 0.0)


def _router(x, w, tm=1024):
    t = x.shape[0]
    return pl.pallas_call(
        _router_kernel,
        out_shape=jax.ShapeDtypeStruct((t, LANES), F32),
        grid=(t // tm,),
        in_specs=[pl.BlockSpec((tm, D_MODEL), lambda i: (i, 0)), pl.BlockSpec(w.shape, lambda i: (0, 0))],
        out_specs=pl.BlockSpec((tm, LANES), lambda i: (i, 0)),
        compiler_params=_cparams(("parallel",)),
        name="router",
    )(x, w)


def _pad_heads(w, heads, width, padded, axis):
    shape = list(w.shape)
    shape[axis:axis + 1] = [heads, width]
    w = w.reshape(shape)
    pad = [(0, 0)] * w.ndim
    pad[axis + 1] = (0, padded - width)
    w = jnp.pad(w, pad)
    shape[axis:axis + 2] = [heads * padded]
    return w.reshape(shape)


def _layout_w_in(w):
    bounds = np.cumsum([0, 256, 256, 256, 384, 384, 384, 192, 192, 384, 384, 16])
    qa, ka, va, qb, kb, vb, qc, kc, vc, gc, rc = [w[:, bounds[i]:bounds[i + 1]] for i in range(11)]
    rc = jnp.pad(rc, ((0, 0), (0, LANES - C_RANK)))
    qc = _pad_heads(qc, C_HEADS, C_K, C_KP, 1)
    kc = _pad_heads(kc, C_HEADS, C_K, C_KP, 1)
    vc = _pad_heads(vc, C_HEADS, C_V, C_VP, 1)
    gc = _pad_heads(gc, C_HEADS, C_V, C_VP, 1)
    return jnp.concatenate([qa, ka, qb, kb, va, vb, rc, qc, kc, vc, gc], axis=1).astype(BF16)


def _rope_tables(positions, dim):
    rot = dim // ROPE_FRAC
    inv_freq = ROPE_THETA ** (-jnp.arange(0, rot, 2, dtype=F32) / rot)
    ang = positions.astype(F32).reshape(-1, 1) * inv_freq
    cos, sin = jnp.cos(ang), jnp.sin(ang)
    t = ang.shape[0]
    c = jnp.concatenate([cos, cos, jnp.ones((t, dim - rot), F32)], axis=1)
    s = jnp.concatenate([-sin, sin, jnp.zeros((t, dim - rot), F32)], axis=1)
    reps = LANES // dim
    return jnp.tile(c, (1, reps)), jnp.tile(s, (1, reps))


def kernel(x, mem, positions, w_in, lam_q1, lam_k1, lam_q2, lam_k2, diff_norm_g, gla_wa2, gla_ba, gla_norm_g,
           w_out, ln_mix_g, ln_mix_b, xa_wq, xa_wkv, xa_wo, ln_xa_g, ln_xa_b, ffd_w13, ffd_w2, moe_router,
           moe_w13, moe_w2, ln_ffn_g, ln_ffn_b):
    bsz, seq, _ = x.shape
    t = bsz * seq
    ca, sa = _rope_tables(positions, A_QK)
    cb, sb = _rope_tables(positions, B_DIM)
    tabs = (ca, sa, cb, sb)
    memb = mem.reshape(-1, D_MODEL).astype(BF16)
    mem_len = mem.shape[1]
    row = lambda v: v.reshape(1, -1).astype(F32)
    unit_gates = jnp.ones((t, LANES), F32)

    xf = x.reshape(t, D_MODEL)
    xb = xf.astype(BF16)
    for l in range(DEPTH):
        lam_init = 0.8 - 0.6 * math.exp(-0.3 * l)
        lam = (jnp.exp(jnp.sum(lam_q1[l] * lam_k1[l])) - jnp.exp(jnp.sum(lam_q2[l] * lam_k2[l]))
               + lam_init).reshape(1, 1).astype(F32)
        h = _inproj(xb, _layout_w_in(w_in[l]), tabs)
        h3 = h.reshape(bsz, seq, H_COLS)
        oa = _diff_attention(h3, lam, jnp.tile(row(diff_norm_g[l]), (1, 2)), lam_init)
        ob = _dilated_attention(h3)
        wa = _pad_heads(gla_wa2[l], C_HEADS, C_K, C_KP, 1)
        wa = jnp.pad(wa, ((0, LANES - C_RANK), (0, 0))).reshape(LANES, 2, LANES).transpose(1, 0, 2).astype(BF16)
        ba = _pad_heads(gla_ba[l], C_HEADS, C_K, C_KP, 0).reshape(2, 1, LANES).astype(F32)
        gn = jnp.pad(row(gla_norm_g[l]), ((0, 0), (0, C_VP - C_V)))
        oc = _gla(h3, wa, ba, gn)
        wo = w_out[l]
        wo_a = wo[:A_HEADS * A_V].astype(BF16)
        wo_b = wo[A_HEADS * A_V:A_HEADS * A_V + B_HEADS * B_DIM].astype(BF16)
        wo_c = _pad_heads(wo[A_HEADS * A_V + B_HEADS * B_DIM:], C_HEADS, C_V, C_VP, 0).astype(BF16)
        xf, xb = _outproj_ln(oa.reshape(t, -1), ob.reshape(t, -1), oc.reshape(t, -1), wo_a, wo_b, wo_c,
                             xf, row(ln_mix_g[l]), row(ln_mix_b[l]))

        kv = _matmul(memb, xa_wkv[l].astype(BF16), BF16)
        x3, xb3 = _xattn_ln(xb.reshape(bsz, seq, D_MODEL), xf.reshape(bsz, seq, D_MODEL), xa_wq[l].astype(BF16),
                            kv.reshape(bsz, mem_len, 2 * D_MODEL), xa_wo[l].astype(BF16),
                            row(ln_xa_g[l]), row(ln_xa_b[l]))
        xf, xb = x3.reshape(t, D_MODEL), xb3.reshape(t, D_MODEL)

        if l % 2 == 0:
            xf, xb = _ffn_ln(xb, xf, unit_gates, ffd_w13[l // 2][None].astype(BF16), ffd_w2[l // 2][None].astype(BF16),
                             row(ln_ffn_g[l]), row(ln_ffn_b[l]), tm=1024, tf=256)
        else:
            rw = jnp.pad(moe_router[l // 2], ((0, 0), (0, LANES - N_EXPERTS)))
            gates = _router(xf, rw)
            xf, xb = _ffn_ln(xb, xf, gates, moe_w13[l // 2].astype(BF16), moe_w2[l // 2].astype(BF16),
                             row(ln_ffn_g[l]), row(ln_ffn_b[l]), tm=512, tf=512)
    return xf.reshape(bsz, seq, D_MODEL)
```

```python
import functools
import math

import numpy as np
import jax
import jax.numpy as jnp
from jax import lax
from jax.experimental import pallas as pl
from jax.experimental.pallas import tpu as pltpu

F32 = jnp.float32
BF16 = jnp.bfloat16

LANES = 128
VMEM_LIMIT_BYTES = 56 * 1024 * 1024

D_MODEL = 1024
DEPTH = 2
ROPE_THETA = 500000.0
ROPE_FRAC = 4
LN_EPS = 1e-5
NEG_INF = -1e30
ALPHA = (2 * DEPTH) ** 0.25

A_HEADS, A_QK, A_V = 4, 32, 64
B_HEADS, B_DIM = 6, 64
B_SPAN = 128
B_DILATIONS = (1, 4, 16)
C_HEADS, C_K, C_V = 4, 48, 96
C_KP, C_VP = 64, 128
C_RANK, C_TAU, C_CHUNK = 16, 16.0, 16
XA_HEADS, XA_DIM = 4, 256
N_EXPERTS, TOP_K = 8, 2

QA, KA, QB, KB, VA, VB, RC, QC, KC, VC, GC, H_COLS = (
    0, 256, 512, 896, 1280, 1536, 1920, 2048, 2304, 2560, 3072, 3584)
IN_TN = 256
ROPE_A_TILES = (QB - QA) // IN_TN
ROPE_B_TILES = (VA - QA) // IN_TN


def _cparams(sem):
    return pltpu.CompilerParams(dimension_semantics=sem, vmem_limit_bytes=VMEM_LIMIT_BYTES)


def _layer_norm(y, g, b):
    mu = jnp.mean(y, axis=-1, keepdims=True)
    yc = y - mu
    var = jnp.mean(yc * yc, axis=-1, keepdims=True)
    return yc * lax.rsqrt(var + LN_EPS) * g + b


def _dot_nt(a, b):
    return lax.dot_general(a, b, (((1,), (1,)), ((), ())), preferred_element_type=F32)


def _inproj_kernel(x_ref, w_ref, ca_ref, sa_ref, cb_ref, sb_ref, o_ref):
    x = x_ref[...]
    lane = lax.broadcasted_iota(jnp.int32, (x.shape[0], LANES), 1)

    def rope_cols(lo, hi, c_ref, s_ref, half, period):
        acc = jnp.dot(x, w_ref[:, lo:hi], preferred_element_type=F32)
        c, s = c_ref[...], s_ref[...]
        first = lane % period < half
        for k in range((hi - lo) // LANES):
            a = acc[:, k * LANES:(k + 1) * LANES]
            up = pltpu.roll(a, LANES - half, 1)
            dn = pltpu.roll(a, half, 1)
            o_ref[:, lo + k * LANES:lo + (k + 1) * LANES] = (a * c + jnp.where(first, up, dn) * s).astype(o_ref.dtype)

    rope_cols(QA, QB, ca_ref, sa_ref, A_QK // ROPE_FRAC // 2, A_QK)
    rope_cols(QB, VA, cb_ref, sb_ref, B_DIM // ROPE_FRAC // 2, B_DIM)
    o_ref[:, VA:] = jnp.dot(x, w_ref[:, VA:], preferred_element_type=F32).astype(o_ref.dtype)


def _inproj(xb, w, tabs, tm=512):
    t = xb.shape[0]
    tab_spec = pl.BlockSpec((tm, LANES), lambda i: (i, 0))
    return pl.pallas_call(
        _inproj_kernel,
        out_shape=jax.ShapeDtypeStruct((t, H_COLS), BF16),
        grid=(t // tm,),
        in_specs=[pl.BlockSpec((tm, D_MODEL), lambda i: (i, 0)),
                  pl.BlockSpec((D_MODEL, H_COLS), lambda i: (0, 0)),
                  tab_spec, tab_spec, tab_spec, tab_spec],
        out_specs=pl.BlockSpec((tm, H_COLS), lambda i: (i, 0)),
        compiler_params=_cparams(("parallel",)),
        name="inproj_rope",
    )(xb, w, *tabs)


def _diff_kernel(lam_ref, q_ref, k_ref, v_ref, g_ref, o_ref, *, seq, tq, out_scale):
    lane = lax.broadcasted_iota(jnp.int32, (tq, LANES), 1)
    row = lax.broadcasted_iota(jnp.int32, (tq, tq), 0)
    col = lax.broadcasted_iota(jnp.int32, (tq, tq), 1)
    causal = col <= row
    lo = lane < A_V
    scale = A_QK ** -0.5
    lam = lam_ref[0, 0]

    for i in range(seq // tq):
        past = i * tq
        q = q_ref[0, past:past + tq, :]
        k_d = k_ref[0, past:past + tq, :]
        v_d = v_ref[0, past:past + tq, :]

        def attend(c):
            qm = jnp.where(lane // A_QK == c, q, jnp.zeros_like(q))
            s_d = jnp.where(causal, _dot_nt(qm, k_d) * scale, NEG_INF)
            m = jnp.max(s_d, axis=-1, keepdims=True)
            if past:
                s_p = _dot_nt(qm, k_ref[0, :past, :]) * scale
                m = jnp.maximum(m, jnp.max(s_p, axis=-1, keepdims=True))
            p_d = jnp.exp(s_d - m)
            l = jnp.sum(p_d, axis=-1, keepdims=True)
            acc = jnp.dot(p_d.astype(BF16), v_d, preferred_element_type=F32)
            if past:
                p_p = jnp.exp(s_p - m)
                l = l + jnp.sum(p_p, axis=-1, keepdims=True)
                acc = acc + jnp.dot(p_p.astype(BF16), v_ref[0, :past, :], preferred_element_type=F32)
            return acc / l

        o_h0 = attend(0) - lam * attend(1)
        o_h1 = attend(2) - lam * attend(3)
        o = jnp.where(lo, o_h0, o_h1)
        sq = o * o
        s_lo = jnp.sum(jnp.where(lo, sq, 0.0), axis=-1, keepdims=True)
        s_all = jnp.sum(sq, axis=-1, keepdims=True)
        ms = jnp.where(lo, s_lo, s_all - s_lo) * (1.0 / A_V)
        o_ref[0, past:past + tq, :] = (o * lax.rsqrt(ms + LN_EPS) * g_ref[...] * out_scale).astype(o_ref.dtype)


def _diff_attention(h3, lam, g, lam_init, tq=512):
    bsz, seq, _ = h3.shape
    kern = functools.partial(_diff_kernel, seq=seq, tq=tq, out_scale=1.0 - lam_init)
    blk = lambda off: pl.BlockSpec((1, seq, LANES), lambda b, p: (b, 0, off // LANES + p))
    return pl.pallas_call(
        kern,
        out_shape=jax.ShapeDtypeStruct((bsz, seq, A_HEADS * A_V), BF16),
        grid=(bsz, A_HEADS // 2),
        in_specs=[pl.BlockSpec(memory_space=pltpu.SMEM), blk(QA), blk(KA), blk(VA),
                  pl.BlockSpec((1, LANES), lambda b, p: (0, 0))],
        out_specs=pl.BlockSpec((1, seq, LANES), lambda b, p: (b, 0, p)),
        compiler_params=_cparams(("parallel", "parallel")),
        name="diff_attention",
    )(lam, h3, h3, h3, g)


def _dil_kernel(q_ref, k_ref, v_ref, o_ref, qf, kf, vf, qs, ks, vs, m_s, l_s, a_s, *, seq):
    qf[...] = q_ref[0].astype(F32) * (B_DIM ** -0.5)
    kf[...] = k_ref[0].astype(F32)
    vf[...] = v_ref[0].astype(F32)
    span = B_SPAN
    lo = lax.broadcasted_iota(jnp.int32, (span, LANES), 1) < B_DIM
    own = ((lax.broadcasted_iota(jnp.int32, (2 * span, LANES), 1) < B_DIM)
           == (lax.broadcasted_iota(jnp.int32, (2 * span, LANES), 0) < span))
    qi = lax.broadcasted_iota(jnp.int32, (2 * span, 2 * span), 0) % span
    ki = lax.broadcasted_iota(jnp.int32, (2 * span, 2 * span), 1)
    band = (ki >= qi) & (ki <= qi + span)
    first = (lax.broadcasted_iota(jnp.int32, (2 * span, span), 1)
             <= lax.broadcasted_iota(jnp.int32, (2 * span, span), 0) % span)

    for g, r in enumerate(B_DILATIONS):
        sub = seq // r
        for rho in range(r):
            rows = pl.ds(rho, sub, stride=r) if r > 1 else pl.ds(0, sub)
            qs[rho * sub:(rho + 1) * sub, :] = qf[rows, :].astype(BF16)
            ks[rho * sub:(rho + 1) * sub, :] = kf[rows, :].astype(BF16)
            vs[rho * sub:(rho + 1) * sub, :] = vf[rows, :].astype(BF16)
        for rho in range(r):
            for n in range(sub // span):
                base = rho * sub + n * span
                qb = qs[base:base + span, :]
                lhs = jnp.where(own, jnp.concatenate([qb, qb], axis=0), jnp.zeros((), BF16))
                start = base - span if n else base
                s = jnp.where(band if n else first, _dot_nt(lhs, ks[start:base + span, :]), NEG_INF)
                m = jnp.max(s, axis=-1, keepdims=True)
                p = jnp.exp(s - m)
                l = jnp.sum(p, axis=-1, keepdims=True)
                a = jnp.dot(p.astype(BF16), vs[start:base + span, :], preferred_element_type=F32)
                m = jnp.broadcast_to(m, (2 * span, LANES))
                l = jnp.broadcast_to(l, (2 * span, LANES))
                m_g = jnp.where(lo, m[:span], m[span:])
                l_g = jnp.where(lo, l[:span], l[span:])
                a_g = jnp.where(lo, a[:span], a[span:])
                nat = pl.ds(rho + r * span * n, span, stride=r) if r > 1 else pl.ds(n * span, span)
                if g == 0:
                    m_s[nat, :] = m_g
                    l_s[nat, :] = l_g
                    a_s[nat, :] = a_g
                else:
                    m_o = m_s[nat, :]
                    m_n = jnp.maximum(m_o, m_g)
                    w_o = jnp.exp(m_o - m_n)
                    w_g = jnp.exp(m_g - m_n)
                    m_s[nat, :] = m_n
                    l_s[nat, :] = w_o * l_s[nat, :] + w_g * l_g
                    a_s[nat, :] = w_o * a_s[nat, :] + w_g * a_g

    o_ref[0] = (a_s[...] / l_s[...]).astype(o_ref.dtype)


def _dilated_attention(h3):
    bsz, seq, _ = h3.shape
    blk = lambda off: pl.BlockSpec((1, seq, LANES), lambda b, p: (b, 0, off // LANES + p))
    return pl.pallas_call(
        functools.partial(_dil_kernel, seq=seq),
        out_shape=jax.ShapeDtypeStruct((bsz, seq, B_HEADS * B_DIM), BF16),
        grid=(bsz, B_HEADS // 2),
        in_specs=[blk(QB), blk(KB), blk(VB)],
        out_specs=pl.BlockSpec((1, seq, LANES), lambda b, p: (b, 0, p)),
        scratch_shapes=([pltpu.VMEM((seq, LANES), F32)] * 3 + [pltpu.VMEM((seq, LANES), BF16)] * 3
                        + [pltpu.VMEM((seq, LANES), F32)] * 3),
        compiler_params=_cparams(("parallel", "parallel")),
        name="dilated_attention",
    )(h3, h3, h3)


def _gla_kernel(q_ref, k_ref, v_ref, gate_ref, rc_ref, wa_ref, ba_ref, gn_ref, o_ref, st, *, tb):
    t = pl.program_id(2)

    @pl.when(t == 0)
    def _():
        st[...] = jnp.zeros_like(st)

    ch = C_CHUNK
    z = jnp.dot(rc_ref[0], wa_ref[0], preferred_element_type=F32) + ba_ref[0]
    la = (jnp.minimum(z, 0.0) - jnp.log1p(jnp.exp(-jnp.abs(z)))) * (1.0 / C_TAU)
    ri = lax.broadcasted_iota(jnp.int32, (tb, tb), 0)
    ci = lax.broadcasted_iota(jnp.int32, (tb, tb), 1)
    same = (ri // ch) == (ci // ch)
    tri = jnp.where(same & (ci <= ri), 1.0, 0.0).astype(F32)
    blk = jnp.where(same, 1.0, 0.0).astype(F32)
    b = jnp.dot(tri, la, preferred_element_type=F32, precision=lax.Precision.HIGHEST)
    bl = jnp.dot(blk, la, preferred_element_type=F32, precision=lax.Precision.HIGHEST)
    qs = q_ref[0].astype(F32) * (C_K ** -0.5)
    kf = k_ref[0].astype(F32)
    vf = v_ref[0].astype(F32)
    qt = (qs * jnp.exp(b)).astype(BF16)
    kt = (kf * jnp.exp(bl - b)).astype(BF16)
    gl = jnp.exp(bl)

    dl = lax.broadcasted_iota(jnp.int32, (LANES, 2 * C_VP), 0)
    vl = lax.broadcasted_iota(jnp.int32, (LANES, 2 * C_VP), 1)
    red = jnp.where(dl // C_KP == vl // C_VP, 1.0, 0.0).astype(BF16)
    sv = lax.broadcasted_iota(jnp.int32, (2 * C_VP, LANES), 0)
    sd = lax.broadcasted_iota(jnp.int32, (2 * C_VP, LANES), 1)
    own = sv // C_VP == sd // C_KP
    rowi = lax.broadcasted_iota(jnp.int32, (ch, LANES), 0)

    s_prev = st[...]
    o_chunks = []
    for c in range(tb // ch):
        rows = slice(c * ch, (c + 1) * ch)
        o_inter = _dot_nt(qt[rows], s_prev.astype(BF16))
        bq, qq, kk, vv = b[rows], qs[rows], kf[rows], vf[rows]
        terms = []
        for j in range(ch):
            e = jnp.exp(jnp.where(rowi >= j, bq - bq[j:j + 1, :], NEG_INF))
            terms.append((qq * e * kk[j:j + 1, :]).astype(BF16))
        sc = jnp.dot(jnp.concatenate(terms, axis=0), red, preferred_element_type=F32)
        o_intra = sc[0:ch, :] * vv[0:1, :]
        for j in range(1, ch):
            o_intra = o_intra + sc[j * ch:(j + 1) * ch, :] * vv[j:j + 1, :]
        o_chunks.append(o_inter + o_intra)
        ut = lax.dot_general(vv.astype(BF16), kt[rows], (((0,), (0,)), ((), ())),
                             preferred_element_type=F32)
        s_prev = gl[c * ch:c * ch + 1, :] * s_prev + jnp.where(own, ut, 0.0)
    st[...] = s_prev

    o = jnp.concatenate(o_chunks, axis=0)
    gate = gate_ref[0].astype(F32)
    outs = []
    for hh in range(2):
        oh = o[:, hh * C_VP:(hh + 1) * C_VP]
        ms = jnp.sum(oh * oh, axis=-1, keepdims=True) * (1.0 / C_V)
        gh = gate[:, hh * C_VP:(hh + 1) * C_VP]
        outs.append(oh * lax.rsqrt(ms + LN_EPS) * gn_ref[...] * (gh * jax.nn.sigmoid(gh)))
    o_ref[0] = jnp.concatenate(outs, axis=1).astype(o_ref.dtype)


def _gla(h3, wa, ba, gn, tb=128):
    bsz, seq, _ = h3.shape
    return pl.pallas_call(
        functools.partial(_gla_kernel, tb=tb),
        out_shape=jax.ShapeDtypeStruct((bsz, seq, C_HEADS * C_VP), BF16),
        grid=(bsz, C_HEADS // 2, seq // tb),
        in_specs=[pl.BlockSpec((1, tb, LANES), lambda b, p, t: (b, t, QC // LANES + p)),
                  pl.BlockSpec((1, tb, LANES), lambda b, p, t: (b, t, KC // LANES + p)),
                  pl.BlockSpec((1, tb, 2 * C_VP), lambda b, p, t: (b, t, VC // (2 * C_VP) + p)),
                  pl.BlockSpec((1, tb, 2 * C_VP), lambda b, p, t: (b, t, GC // (2 * C_VP) + p)),
                  pl.BlockSpec((1, tb, LANES), lambda b, p, t: (b, t, RC // LANES)),
                  pl.BlockSpec((1, LANES, LANES), lambda b, p, t: (p, 0, 0)),
                  pl.BlockSpec((1, 1, LANES), lambda b, p, t: (p, 0, 0)),
                  pl.BlockSpec((1, C_VP), lambda b, p, t: (0, 0))],
        out_specs=pl.BlockSpec((1, tb, 2 * C_VP), lambda b, p, t: (b, t, p)),
        scratch_shapes=[pltpu.VMEM((2 * C_VP, LANES), F32)],
        compiler_params=_cparams(("parallel", "parallel", "arbitrary")),
        name="gla",
    )(h3, h3, h3, h3, h3, wa, ba, gn)


def _outproj_kernel(oa_ref, ob_ref, oc_ref, wa_ref, wb_ref, wc_ref, x_ref, g_ref, b_ref, y_ref, yb_ref):
    acc = jnp.dot(oa_ref[...], wa_ref[...], preferred_element_type=F32)
    acc += jnp.dot(ob_ref[...], wb_ref[...], preferred_element_type=F32)
    acc += jnp.dot(oc_ref[...], wc_ref[...], preferred_element_type=F32)
    y = _layer_norm(ALPHA * x_ref[...] + acc, g_ref[...], b_ref[...])
    y_ref[...] = y
    yb_ref[...] = y.astype(BF16)


def _outproj_ln(oa, ob, oc, wa, wb, wc, x, g, b, tm=512):
    t = x.shape[0]
    row = lambda n: pl.BlockSpec((tm, n), lambda i: (i, 0))
    full = lambda a: pl.BlockSpec(a.shape, lambda i: (0, 0))
    return pl.pallas_call(
        _outproj_kernel,
        out_shape=(jax.ShapeDtypeStruct((t, D_MODEL), F32), jax.ShapeDtypeStruct((t, D_MODEL), BF16)),
        grid=(t // tm,),
        in_specs=[row(oa.shape[1]), row(ob.shape[1]), row(oc.shape[1]), full(wa), full(wb), full(wc),
                  row(D_MODEL), full(g), full(b)],
        out_specs=(row(D_MODEL), row(D_MODEL)),
        compiler_params=_cparams(("parallel",)),
        name="outproj_ln",
    )(oa, ob, oc, wa, wb, wc, x, g, b)


def _matmul_kernel(a_ref, w_ref, o_ref):
    o_ref[...] = jnp.dot(a_ref[...], w_ref[...], preferred_element_type=F32).astype(o_ref.dtype)


def _matmul(a, w, out_dtype, tm=256, tn=512):
    m, k = a.shape
    n = w.shape[1]
    return pl.pallas_call(
        _matmul_kernel,
        out_shape=jax.ShapeDtypeStruct((m, n), out_dtype),
        grid=(m // tm, n // tn),
        in_specs=[pl.BlockSpec((tm, k), lambda i, j: (i, 0)), pl.BlockSpec((k, tn), lambda i, j: (0, j))],
        out_specs=pl.BlockSpec((tm, tn), lambda i, j: (i, j)),
        compiler_params=_cparams(("parallel", "arbitrary")),
        name="matmul",
    )(a, w)


def _xattn_kernel(xb_ref, x_ref, wq_ref, k_ref, v_ref, wo_ref, g_ref, b_ref, y_ref, yb_ref):
    q = jnp.dot(xb_ref[0], wq_ref[...], preferred_element_type=F32).astype(BF16)
    outs = []
    for h in range(XA_HEADS):
        cols = slice(h * XA_DIM, (h + 1) * XA_DIM)
        s = _dot_nt(q[:, cols], k_ref[0, :, cols]) * (XA_DIM ** -0.5)
        m = jnp.max(s, axis=-1, keepdims=True)
        p = jnp.exp(s - m)
        l = jnp.sum(p, axis=-1, keepdims=True)
        o = jnp.dot(p.astype(BF16), v_ref[0, :, cols], preferred_element_type=F32)
        outs.append((o / l).astype(BF16))
    o = jnp.concatenate(outs, axis=1)
    xa = jnp.dot(o, wo_ref[...], preferred_element_type=F32)
    y = _layer_norm(ALPHA * x_ref[0] + xa, g_ref[...], b_ref[...])
    y_ref[0] = y
    yb_ref[0] = y.astype(BF16)


def _xattn_ln(xb3, x3, wq, kv3, wo, g, b, tm=512):
    bsz, seq, _ = x3.shape
    mem_len = kv3.shape[1]
    row = pl.BlockSpec((1, tm, D_MODEL), lambda bi, i: (bi, i, 0))
    full = lambda a: pl.BlockSpec(a.shape, lambda bi, i: (0, 0))
    return pl.pallas_call(
        _xattn_kernel,
        out_shape=(jax.ShapeDtypeStruct((bsz, seq, D_MODEL), F32), jax.ShapeDtypeStruct((bsz, seq, D_MODEL), BF16)),
        grid=(bsz, seq // tm),
        in_specs=[row, row, full(wq),
                  pl.BlockSpec((1, mem_len, D_MODEL), lambda bi, i: (bi, 0, 0)),
                  pl.BlockSpec((1, mem_len, D_MODEL), lambda bi, i: (bi, 0, 1)),
                  full(wo), full(g), full(b)],
        out_specs=(row, row),
        compiler_params=_cparams(("parallel", "arbitrary")),
        name="xattn_ln",
    )(xb3, x3, wq, kv3, kv3, wo, g, b)


def _swiglu_chunk(xb, w1, w3, w2):
    hg = jnp.dot(xb, w1, preferred_element_type=F32)
    hu = jnp.dot(xb, w3, preferred_element_type=F32)
    a = hg * jax.nn.sigmoid(hg) * hu
    return jnp.dot(a.astype(BF16), w2, preferred_element_type=F32)


def _ffn_kernel(xb_ref, x_ref, w1_ref, w3_ref, w2_ref, g_ref, b_ref, y_ref, yb_ref, acc_ref):
    j = pl.program_id(1)

    @pl.when(j == 0)
    def _():
        acc_ref[...] = jnp.zeros_like(acc_ref)

    acc_ref[...] += _swiglu_chunk(xb_ref[...], w1_ref[...], w3_ref[...], w2_ref[...])

    @pl.when(j == pl.num_programs(1) - 1)
    def _():
        y = _layer_norm(ALPHA * x_ref[...] + acc_ref[...], g_ref[...], b_ref[...])
        y_ref[...] = y
        yb_ref[...] = y.astype(BF16)


def _ffn_ln(xb, x, w13, w2, g, b, tm=1024, tf=256):
    t = x.shape[0]
    nf = w2.shape[0] // tf
    row = pl.BlockSpec((tm, D_MODEL), lambda i, j: (i, 0))
    vec = lambda a: pl.BlockSpec(a.shape, lambda i, j: (0, 0))
    return pl.pallas_call(
        _ffn_kernel,
        out_shape=(jax.ShapeDtypeStruct((t, D_MODEL), F32), jax.ShapeDtypeStruct((t, D_MODEL), BF16)),
        grid=(t // tm, nf),
        in_specs=[row, row,
                  pl.BlockSpec((D_MODEL, tf), lambda i, j: (0, j)),
                  pl.BlockSpec((D_MODEL, tf), lambda i, j: (0, j + nf)),
                  pl.BlockSpec((tf, D_MODEL), lambda i, j: (j, 0)),
                  vec(g), vec(b)],
        out_specs=(row, row),
        scratch_shapes=[pltpu.VMEM((tm, D_MODEL), F32)],
        compiler_params=_cparams(("parallel", "arbitrary")),
        name="ffn_ln",
    )(xb, x, w13, w13, w2, g, b)


def _router_kernel(x_ref, w_ref, route_ref):
    logits = jnp.dot(x_ref[...], w_ref[...], preferred_element_type=F32, precision=lax.Precision.HIGHEST)
    lane = lax.broadcasted_iota(jnp.int32, logits.shape, 1)
    logits = jnp.where(lane < N_EXPERTS, logits, -jnp.inf)
    v1 = jnp.max(logits, axis=-1, keepdims=True)
    i1 = jnp.min(jnp.where(logits == v1, lane, LANES), axis=-1, keepdims=True)
    rest = jnp.where(lane == i1, -jnp.inf, logits)
    v2 = jnp.max(rest, axis=-1, keepdims=True)
    i2 = jnp.min(jnp.where(rest == v2, lane, LANES), axis=-1, keepdims=True)
    e2 = jnp.exp(v2 - v1)
    w1 = 1.0 / (1.0 + e2)
    w2 = e2 / (1.0 + e2)
    route_ref[...] = (jnp.where(lane == 0, w1, 0.0) + jnp.where(lane == 1, w2, 0.0)
                      + jnp.where(lane == 2, i1.astype(F32), 0.0) + jnp.where(lane == 3, i2.astype(F32), 0.0))


def _router(x, w, tm=1024):
    t = x.shape[0]
    return pl.pallas_call(
        _router_kernel,
        out_shape=jax.ShapeDtypeStruct((t, LANES), F32),
        grid=(t // tm,),
        in_specs=[pl.BlockSpec((tm, D_MODEL), lambda i: (i, 0)), pl.BlockSpec(w.shape, lambda i: (0, 0))],
        out_specs=pl.BlockSpec((tm, LANES), lambda i: (i, 0)),
        compiler_params=_cparams(("parallel",)),
        name="router",
    )(x, w)


def _dispatch_plan(route, tm):
    t = route.shape[0]
    expert = route[:, 2:4].astype(jnp.int32).reshape(-1)
    onehot = (expert[:, None] == jnp.arange(N_EXPERTS, dtype=jnp.int32)[None, :]).astype(jnp.int32)
    csum = jnp.cumsum(onehot, axis=0)
    rank = jnp.sum((csum - 1) * onehot, axis=1)
    counts = csum[-1]
    padded = (counts + tm - 1) // tm * tm
    ends = jnp.cumsum(padded)
    dest = (ends - padded)[expert] + rank
    n_tiles = TOP_K * t // tm + N_EXPERTS
    src = jnp.zeros((n_tiles * tm,), jnp.int32).at[dest].set(jnp.arange(TOP_K * t, dtype=jnp.int32) // TOP_K)
    tile_start = jnp.arange(n_tiles, dtype=jnp.int32) * tm
    tile_expert = jnp.sum((tile_start[:, None] >= ends[None, :]).astype(jnp.int32), axis=1)
    tile_expert = jnp.minimum(tile_expert, N_EXPERTS - 1)
    tile_used = (tile_start < ends[-1]).astype(jnp.int32)
    return src, dest.reshape(t, TOP_K).astype(jnp.int32), tile_expert, tile_used


def _gather_kernel(src_ref, x_hbm, o_ref, buf, sem, *, tm):
    base = pl.program_id(0) * tm

    def issue(r, carry):
        pltpu.make_async_copy(x_hbm.at[pl.ds(src_ref[base + r], 1), :], buf.at[pl.ds(r, 1), :], sem).start()
        return carry

    lax.fori_loop(0, tm, issue, 0, unroll=8)
    pltpu.make_async_copy(x_hbm.at[pl.ds(0, tm), :], buf, sem).wait()
    o_ref[...] = buf[...].astype(o_ref.dtype)


def _gather_rows(src, x, tm):
    n = src.shape[0]
    return pl.pallas_call(
        functools.partial(_gather_kernel, tm=tm),
        out_shape=jax.ShapeDtypeStruct((n, D_MODEL), BF16),
        grid_spec=pltpu.PrefetchScalarGridSpec(
            num_scalar_prefetch=1, grid=(n // tm,),
            in_specs=[pl.BlockSpec(memory_space=pl.ANY)],
            out_specs=pl.BlockSpec((tm, D_MODEL), lambda i, src_ref: (i, 0)),
            scratch_shapes=[pltpu.VMEM((tm, D_MODEL), F32), pltpu.SemaphoreType.DMA]),
        compiler_params=_cparams(("arbitrary",)),
        name="moe_gather",
    )(src, x)


def _moe_ffn_kernel(te_ref, tu_ref, xs_ref, w1_ref, w3_ref, w2_ref, y_ref, acc_ref):
    i = pl.program_id(0)
    j = pl.program_id(1)
    last = pl.num_programs(1) - 1

    @pl.when(j == 0)
    def _():
        acc_ref[...] = jnp.zeros_like(acc_ref)

    @pl.when(tu_ref[i] > 0)
    def _():
        acc_ref[...] += _swiglu_chunk(xs_ref[...], w1_ref[0], w3_ref[0], w2_ref[0])

    @pl.when(j == last)
    def _():
        y_ref[...] = acc_ref[...]


def _moe_ffn(tile_expert, tile_used, xs, w13, w2, tm, tf):
    n = xs.shape[0]
    nf = w2.shape[1] // tf
    chunk = lambda i, j, tu: jnp.where(tu[i] > 0, j, nf - 1)
    return pl.pallas_call(
        _moe_ffn_kernel,
        out_shape=jax.ShapeDtypeStruct((n, D_MODEL), F32),
        grid_spec=pltpu.PrefetchScalarGridSpec(
            num_scalar_prefetch=2, grid=(n // tm, nf),
            in_specs=[pl.BlockSpec((tm, D_MODEL), lambda i, j, te, tu: (i, 0)),
                      pl.BlockSpec((1, D_MODEL, tf), lambda i, j, te, tu: (te[i], 0, chunk(i, j, tu))),
                      pl.BlockSpec((1, D_MODEL, tf), lambda i, j, te, tu: (te[i], 0, chunk(i, j, tu) + nf)),
                      pl.BlockSpec((1, tf, D_MODEL), lambda i, j, te, tu: (te[i], chunk(i, j, tu), 0))],
            out_specs=pl.BlockSpec((tm, D_MODEL), lambda i, j, te, tu: (i, 0)),
            scratch_shapes=[pltpu.VMEM((tm, D_MODEL), F32)]),
        compiler_params=_cparams(("arbitrary", "arbitrary")),
        name="moe_ffn",
    )(tile_expert, tile_used, xs, w13, w13, w2)


def _combine_kernel(d0_ref, d1_ref, y_hbm, route_ref, x_ref, g_ref, b_ref, o_ref, ob_ref, buf0, buf1, sem, *, tm):
    base = pl.program_id(0) * tm

    def issue(r, carry):
        pltpu.make_async_copy(y_hbm.at[pl.ds(d0_ref[base + r], 1), :], buf0.at[pl.ds(r, 1), :], sem).start()
        pltpu.make_async_copy(y_hbm.at[pl.ds(d1_ref[base + r], 1), :], buf1.at[pl.ds(r, 1), :], sem).start()
        return carry

    lax.fori_loop(0, tm, issue, 0, unroll=8)
    pltpu.make_async_copy(y_hbm.at[pl.ds(0, tm), :], buf0, sem).wait()
    pltpu.make_async_copy(y_hbm.at[pl.ds(0, tm), :], buf1, sem).wait()
    route = route_ref[...]
    f = route[:, 0:1] * buf0[...] + route[:, 1:2] * buf1[...]
    y = _layer_norm(ALPHA * x_ref[...] + f, g_ref[...], b_ref[...])
    o_ref[...] = y
    ob_ref[...] = y.astype(BF16)


def _combine_ln(dest, ys, route, x, g, b, tm=512):
    t = x.shape[0]
    row = pl.BlockSpec((tm, D_MODEL), lambda i, d0, d1: (i, 0))
    vec = lambda a: pl.BlockSpec(a.shape, lambda i, d0, d1: (0, 0))
    return pl.pallas_call(
        functools.partial(_combine_kernel, tm=tm),
        out_shape=(jax.ShapeDtypeStruct((t, D_MODEL), F32), jax.ShapeDtypeStruct((t, D_MODEL), BF16)),
        grid_spec=pltpu.PrefetchScalarGridSpec(
            num_scalar_prefetch=2, grid=(t // tm,),
            in_specs=[pl.BlockSpec(memory_space=pl.ANY),
                      pl.BlockSpec((tm, LANES), lambda i, d0, d1: (i, 0)),
                      row, vec(g), vec(b)],
            out_specs=(row, row),
            scratch_shapes=[pltpu.VMEM((tm, D_MODEL), F32), pltpu.VMEM((tm, D_MODEL), F32),
                            pltpu.SemaphoreType.DMA]),
        compiler_params=_cparams(("arbitrary",)),
        name="moe_combine_ln",
    )(dest[:, 0], dest[:, 1], ys, route, x, g, b)


def _moe_ln(x, router_w, w13, w2, g, b, tm=1024, tf=512):
    route = _router(x, router_w)
    src, dest, tile_expert, tile_used = _dispatch_plan(route, tm)
    xs = _gather_rows(src, x, tm)
    ys = _moe_ffn(tile_expert, tile_used, xs, w13, w2, tm, tf)
    return _combine_ln(dest, ys, route, x, g, b)


def _pad_heads(w, heads, width, padded, axis):
    shape = list(w.shape)
    shape[axis:axis + 1] = [heads, width]
    w = w.reshape(shape)
    pad = [(0, 0)] * w.ndim
    pad[axis + 1] = (0, padded - width)
    w = jnp.pad(w, pad)
    shape[axis:axis + 2] = [heads * padded]
    return w.reshape(shape)


def _layout_w_in(w):
    bounds = np.cumsum([0, 256, 256, 256, 384, 384, 384, 192, 192, 384, 384, 16])
    qa, ka, va, qb, kb, vb, qc, kc, vc, gc, rc = [w[:, bounds[i]:bounds[i + 1]] for i in range(11)]
    rc = jnp.pad(rc, ((0, 0), (0, LANES - C_RANK)))
    qc = _pad_heads(qc, C_HEADS, C_K, C_KP, 1)
    kc = _pad_heads(kc, C_HEADS, C_K, C_KP, 1)
    vc = _pad_heads(vc, C_HEADS, C_V, C_VP, 1)
    gc = _pad_heads(gc, C_HEADS, C_V, C_VP, 1)
    return jnp.concatenate([qa, ka, qb, kb, va, vb, rc, qc, kc, vc, gc], axis=1).astype(BF16)


def _rope_tables(positions, dim):
    rot = dim // ROPE_FRAC
    inv_freq = ROPE_THETA ** (-jnp.arange(0, rot, 2, dtype=F32) / rot)
    ang = positions.astype(F32).reshape(-1, 1) * inv_freq
    cos, sin = jnp.cos(ang), jnp.sin(ang)
    t = ang.shape[0]
    c = jnp.concatenate([cos, cos, jnp.ones((t, dim - rot), F32)], axis=1)
    s = jnp.concatenate([-sin, sin, jnp.zeros((t, dim - rot), F32)], axis=1)
    reps = LANES // dim
    return jnp.tile(c, (1, reps)), jnp.tile(s, (1, reps))


def kernel(x, mem, positions, w_in, lam_q1, lam_k1, lam_q2, lam_k2, diff_norm_g, gla_wa2, gla_ba, gla_norm_g,
           w_out, ln_mix_g, ln_mix_b, xa_wq, xa_wkv, xa_wo, ln_xa_g, ln_xa_b, ffd_w13, ffd_w2, moe_router,
           moe_w13, moe_w2, ln_ffn_g, ln_ffn_b):
    bsz, seq, _ = x.shape
    t = bsz * seq
    ca, sa = _rope_tables(positions, A_QK)
    cb, sb = _rope_tables(positions, B_DIM)
    tabs = (ca, sa, cb, sb)
    memb = mem.reshape(-1, D_MODEL).astype(BF16)
    mem_len = mem.shape[1]
    row = lambda v: v.reshape(1, -1).astype(F32)

    xf = x.reshape(t, D_MODEL)
    xb = xf.astype(BF16)
    for l in range(DEPTH):
        lam_init = 0.8 - 0.6 * math.exp(-0.3 * l)
        lam = (jnp.exp(jnp.sum(lam_q1[l] * lam_k1[l])) - jnp.exp(jnp.sum(lam_q2[l] * lam_k2[l]))
               + lam_init).reshape(1, 1).astype(F32)
        h = _inproj(xb, _layout_w_in(w_in[l]), tabs)
        h3 = h.reshape(bsz, seq, H_COLS)
        oa = _diff_attention(h3, lam, jnp.tile(row(diff_norm_g[l]), (1, 2)), lam_init)
        ob = _dilated_attention(h3)
        wa = _pad_heads(gla_wa2[l], C_HEADS, C_K, C_KP, 1)
        wa = jnp.pad(wa, ((0, LANES - C_RANK), (0, 0))).reshape(LANES, 2, LANES).transpose(1, 0, 2).astype(BF16)
        ba = _pad_heads(gla_ba[l], C_HEADS, C_K, C_KP, 0).reshape(2, 1, LANES).astype(F32)
        gn = jnp.pad(row(gla_norm_g[l]), ((0, 0), (0, C_VP - C_V)))
        oc = _gla(h3, wa, ba, gn)
        wo = w_out[l]
        wo_a = wo[:A_HEADS * A_V].astype(BF16)
        wo_b = wo[A_HEADS * A_V:A_HEADS * A_V + B_HEADS * B_DIM].astype(BF16)
        wo_c = _pad_heads(wo[A_HEADS * A_V + B_HEADS * B_DIM:], C_HEADS, C_V, C_VP, 0).astype(BF16)
        xf, xb = _outproj_ln(oa.reshape(t, -1), ob.reshape(t, -1), oc.reshape(t, -1), wo_a, wo_b, wo_c,
                             xf, row(ln_mix_g[l]), row(ln_mix_b[l]))

        kv = _matmul(memb, xa_wkv[l].astype(BF16), BF16)
        x3, xb3 = _xattn_ln(xb.reshape(bsz, seq, D_MODEL), xf.reshape(bsz, seq, D_MODEL), xa_wq[l].astype(BF16),
                            kv.reshape(bsz, mem_len, 2 * D_MODEL), xa_wo[l].astype(BF16),
                            row(ln_xa_g[l]), row(ln_xa_b[l]))
        xf, xb = x3.reshape(t, D_MODEL), xb3.reshape(t, D_MODEL)

        if l % 2 == 0:
            xf, xb = _ffn_ln(xb, xf, ffd_w13[l // 2].astype(BF16), ffd_w2[l // 2].astype(BF16),
                             row(ln_ffn_g[l]), row(ln_ffn_b[l]))
        else:
            rw = jnp.pad(moe_router[l // 2], ((0, 0), (0, LANES - N_EXPERTS)))
            xf, xb = _moe_ln(xf, rw, moe_w13[l // 2].astype(BF16), moe_w2[l // 2].astype(BF16),
                             row(ln_ffn_g[l]), row(ln_ffn_b[l]))
    return xf.reshape(bsz, seq, D_MODEL)
```

```python
import functools
import math

import numpy as np
import jax
import jax.numpy as jnp
from jax import lax
from jax.experimental import pallas as pl
from jax.experimental.pallas import tpu as pltpu

F32 = jnp.float32
BF16 = jnp.bfloat16

LANES = 128
VMEM_LIMIT_BYTES = 56 * 1024 * 1024

D_MODEL = 1024
DEPTH = 2
ROPE_THETA = 500000.0
ROPE_FRAC = 4
LN_EPS = 1e-5
NEG_INF = -1e30
ALPHA = (2 * DEPTH) ** 0.25
LOG2_E = math.log2(math.e)

A_HEADS, A_QK, A_V = 4, 32, 64
B_HEADS, B_DIM = 6, 64
B_SPAN = 128
B_DILATIONS = (1, 4, 16)
C_HEADS, C_K, C_V = 4, 48, 96
C_KP, C_VP = 64, 128
C_RANK, C_TAU, C_CHUNK = 16, 16.0, 16
XA_HEADS, XA_DIM = 4, 256
N_EXPERTS, TOP_K = 8, 2

QA, KA, QB, KB, VA, VB, RC, QC, KC, VC, GC, H_COLS = (
    0, 256, 512, 896, 1280, 1536, 1920, 2048, 2304, 2560, 3072, 3584)
IN_TN = 256
ROPE_A_TILES = (QB - QA) // IN_TN
ROPE_B_TILES = (VA - QA) // IN_TN


def _cparams(sem):
    return pltpu.CompilerParams(dimension_semantics=sem, vmem_limit_bytes=VMEM_LIMIT_BYTES)


def _layer_norm(y, g, b):
    mu = jnp.mean(y, axis=-1, keepdims=True)
    yc = y - mu
    var = jnp.mean(yc * yc, axis=-1, keepdims=True)
    return yc * lax.rsqrt(var + LN_EPS) * g + b


def _dot_nt(a, b):
    return lax.dot_general(a, b, (((1,), (1,)), ((), ())), preferred_element_type=F32)


def _inproj_kernel(x_ref, w_ref, ca_ref, sa_ref, cb_ref, sb_ref, o_ref):
    x = x_ref[...]
    lane = lax.broadcasted_iota(jnp.int32, (x.shape[0], LANES), 1)

    def rope_cols(lo, k_lo, hi, c_ref, s_ref, half, period):
        acc = jnp.dot(x, w_ref[:, lo:hi], preferred_element_type=F32)
        c, s = c_ref[...], s_ref[...]
        first = lane % period < half
        for k in range((hi - lo) // LANES):
            a = acc[:, k * LANES:(k + 1) * LANES]
            up = pltpu.roll(a, LANES - half, 1)
            dn = pltpu.roll(a, half, 1)
            r = a * c + jnp.where(first, up, dn) * s
            if lo + k * LANES < k_lo:
                r = r * (period ** -0.5 * LOG2_E)
            o_ref[:, lo + k * LANES:lo + (k + 1) * LANES] = r.astype(o_ref.dtype)

    rope_cols(QA, KA, QB, ca_ref, sa_ref, A_QK // ROPE_FRAC // 2, A_QK)
    rope_cols(QB, KB, VA, cb_ref, sb_ref, B_DIM // ROPE_FRAC // 2, B_DIM)
    o_ref[:, VA:] = jnp.dot(x, w_ref[:, VA:], preferred_element_type=F32).astype(o_ref.dtype)


def _inproj(xb, w, tabs, tm=512):
    t = xb.shape[0]
    tab_spec = pl.BlockSpec((tm, LANES), lambda i: (i, 0))
    return pl.pallas_call(
        _inproj_kernel,
        out_shape=jax.ShapeDtypeStruct((t, H_COLS), BF16),
        grid=(t // tm,),
        in_specs=[pl.BlockSpec((tm, D_MODEL), lambda i: (i, 0)),
                  pl.BlockSpec((D_MODEL, H_COLS), lambda i: (0, 0)),
                  tab_spec, tab_spec, tab_spec, tab_spec],
        out_specs=pl.BlockSpec((tm, H_COLS), lambda i: (i, 0)),
        compiler_params=_cparams(("parallel",)),
        name="inproj_rope",
    )(xb, w, *tabs)


def _diff_kernel(lam_ref, q_ref, k_ref, v_ref, g_ref, o_ref, *, seq, tq, out_scale):
    lane = lax.broadcasted_iota(jnp.int32, (tq, LANES), 1)
    row = lax.broadcasted_iota(jnp.int32, (tq, tq), 0)
    col = lax.broadcasted_iota(jnp.int32, (tq, tq), 1)
    causal = col <= row
    lo = lane < A_V
    lam = lam_ref[0, 0]

    for i in range(seq // tq):
        past = i * tq
        q = q_ref[0, past:past + tq, :]
        k_d = k_ref[0, past:past + tq, :]
        v_d = v_ref[0, past:past + tq, :]

        def attend(c):
            qm = jnp.where(lane // A_QK == c, q, jnp.zeros_like(q))
            s_d = jnp.where(causal, _dot_nt(qm, k_d), NEG_INF)
            m = jnp.max(s_d, axis=-1, keepdims=True)
            if past:
                s_p = _dot_nt(qm, k_ref[0, :past, :])
                m = jnp.maximum(m, jnp.max(s_p, axis=-1, keepdims=True))
            p_d = jnp.exp2(s_d - m)
            l = jnp.sum(p_d, axis=-1, keepdims=True)
            acc = jnp.dot(p_d.astype(BF16), v_d, preferred_element_type=F32)
            if past:
                p_p = jnp.exp2(s_p - m)
                l = l + jnp.sum(p_p, axis=-1, keepdims=True)
                acc = acc + jnp.dot(p_p.astype(BF16), v_ref[0, :past, :], preferred_element_type=F32)
            return acc / l

        o_h0 = attend(0) - lam * attend(1)
        o_h1 = attend(2) - lam * attend(3)
        o = jnp.where(lo, o_h0, o_h1)
        sq = o * o
        s_lo = jnp.sum(jnp.where(lo, sq, 0.0), axis=-1, keepdims=True)
        s_all = jnp.sum(sq, axis=-1, keepdims=True)
        ms = jnp.where(lo, s_lo, s_all - s_lo) * (1.0 / A_V)
        o_ref[0, past:past + tq, :] = (o * lax.rsqrt(ms + LN_EPS) * g_ref[...] * out_scale).astype(o_ref.dtype)


def _diff_attention(h3, lam, g, lam_init, tq=512):
    bsz, seq, _ = h3.shape
    kern = functools.partial(_diff_kernel, seq=seq, tq=tq, out_scale=1.0 - lam_init)
    blk = lambda off: pl.BlockSpec((1, seq, LANES), lambda b, p: (b, 0, off // LANES + p))
    return pl.pallas_call(
        kern,
        out_shape=jax.ShapeDtypeStruct((bsz, seq, A_HEADS * A_V), BF16),
        grid=(bsz, A_HEADS // 2),
        in_specs=[pl.BlockSpec(memory_space=pltpu.SMEM), blk(QA), blk(KA), blk(VA),
                  pl.BlockSpec((1, LANES), lambda b, p: (0, 0))],
        out_specs=pl.BlockSpec((1, seq, LANES), lambda b, p: (b, 0, p)),
        compiler_params=_cparams(("parallel", "parallel")),
        name="diff_attention",
    )(lam, h3, h3, h3, g)


def _dil_kernel(q_ref, k_ref, v_ref, o_ref, qf, kf, vf, qs, ks, vs, m_s, l_s, a_s, *, seq):
    qf[...] = q_ref[0].astype(F32)
    kf[...] = k_ref[0].astype(F32)
    vf[...] = v_ref[0].astype(F32)
    span = B_SPAN
    lo = lax.broadcasted_iota(jnp.int32, (span, LANES), 1) < B_DIM
    own = ((lax.broadcasted_iota(jnp.int32, (2 * span, LANES), 1) < B_DIM)
           == (lax.broadcasted_iota(jnp.int32, (2 * span, LANES), 0) < span))
    qi = lax.broadcasted_iota(jnp.int32, (2 * span, 2 * span), 0) % span
    ki = lax.broadcasted_iota(jnp.int32, (2 * span, 2 * span), 1)
    band = (ki >= qi) & (ki <= qi + span)
    first = (lax.broadcasted_iota(jnp.int32, (2 * span, span), 1)
             <= lax.broadcasted_iota(jnp.int32, (2 * span, span), 0) % span)

    for g, r in enumerate(B_DILATIONS):
        sub = seq // r
        for rho in range(r):
            rows = pl.ds(rho, sub, stride=r) if r > 1 else pl.ds(0, sub)
            qs[rho * sub:(rho + 1) * sub, :] = qf[rows, :].astype(BF16)
            ks[rho * sub:(rho + 1) * sub, :] = kf[rows, :].astype(BF16)
            vs[rho * sub:(rho + 1) * sub, :] = vf[rows, :].astype(BF16)
        for rho in range(r):
            for n in range(sub // span):
                base = rho * sub + n * span
                qb = qs[base:base + span, :]
                lhs = jnp.where(own, jnp.concatenate([qb, qb], axis=0), jnp.zeros((), BF16))
                start = base - span if n else base
                s = jnp.where(band if n else first, _dot_nt(lhs, ks[start:base + span, :]), NEG_INF)
                m = jnp.max(s, axis=-1, keepdims=True)
                p = jnp.exp2(s - m)
                l = jnp.sum(p, axis=-1, keepdims=True)
                a = jnp.dot(p.astype(BF16), vs[start:base + span, :], preferred_element_type=F32)
                m = jnp.broadcast_to(m, (2 * span, LANES))
                l = jnp.broadcast_to(l, (2 * span, LANES))
                m_g = jnp.where(lo, m[:span], m[span:])
                l_g = jnp.where(lo, l[:span], l[span:])
                a_g = jnp.where(lo, a[:span], a[span:])
                nat = pl.ds(rho + r * span * n, span, stride=r) if r > 1 else pl.ds(n * span, span)
                if g == 0:
                    m_s[nat, :] = m_g
                    l_s[nat, :] = l_g
                    a_s[nat, :] = a_g
                else:
                    m_o = m_s[nat, :]
                    m_n = jnp.maximum(m_o, m_g)
                    w_o = jnp.exp2(m_o - m_n)
                    w_g = jnp.exp2(m_g - m_n)
                    m_s[nat, :] = m_n
                    l_s[nat, :] = w_o * l_s[nat, :] + w_g * l_g
                    a_s[nat, :] = w_o * a_s[nat, :] + w_g * a_g

    o_ref[0] = (a_s[...] / l_s[...]).astype(o_ref.dtype)


def _dilated_attention(h3):
    bsz, seq, _ = h3.shape
    blk = lambda off: pl.BlockSpec((1, seq, LANES), lambda b, p: (b, 0, off // LANES + p))
    return pl.pallas_call(
        functools.partial(_dil_kernel, seq=seq),
        out_shape=jax.ShapeDtypeStruct((bsz, seq, B_HEADS * B_DIM), BF16),
        grid=(bsz, B_HEADS // 2),
        in_specs=[blk(QB), blk(KB), blk(VB)],
        out_specs=pl.BlockSpec((1, seq, LANES), lambda b, p: (b, 0, p)),
        scratch_shapes=([pltpu.VMEM((seq, LANES), F32)] * 3 + [pltpu.VMEM((seq, LANES), BF16)] * 3
                        + [pltpu.VMEM((seq, LANES), F32)] * 3),
        compiler_params=_cparams(("parallel", "parallel")),
        name="dilated_attention",
    )(h3, h3, h3)


def _gla_kernel(q_ref, k_ref, v_ref, gate_ref, rc_ref, wa_ref, ba_ref, gn_ref, o_ref, st, *, tb):
    t = pl.program_id(2)

    @pl.when(t == 0)
    def _():
        st[...] = jnp.zeros_like(st)

    ch = C_CHUNK
    z = jnp.dot(rc_ref[0], wa_ref[0], preferred_element_type=F32) + ba_ref[0]
    la = (jnp.minimum(z, 0.0) - jnp.log1p(jnp.exp(-jnp.abs(z)))) * (1.0 / C_TAU)
    ri = lax.broadcasted_iota(jnp.int32, (LANES, LANES), 0)
    ci = lax.broadcasted_iota(jnp.int32, (LANES, LANES), 1)
    same = (ri // ch) == (ci // ch)
    tri = jnp.where(same & (ci <= ri), 1.0, 0.0).astype(F32)
    blk = jnp.where(same, 1.0, 0.0).astype(F32)
    sums = lambda m: jnp.concatenate(
        [jnp.dot(m, la[r:r + LANES], preferred_element_type=F32, precision=lax.Precision.HIGHEST)
         for r in range(0, tb, LANES)], axis=0)
    b = sums(tri)
    bl = sums(blk)
    qs = q_ref[0].astype(F32) * (C_K ** -0.5)
    kf = k_ref[0].astype(F32)
    vf = v_ref[0].astype(F32)
    qt = (qs * jnp.exp(b)).astype(BF16)
    kt = (kf * jnp.exp(bl - b)).astype(BF16)
    gl = jnp.exp(bl)

    dl = lax.broadcasted_iota(jnp.int32, (LANES, 2 * C_VP), 0)
    vl = lax.broadcasted_iota(jnp.int32, (LANES, 2 * C_VP), 1)
    red = jnp.where(dl // C_KP == vl // C_VP, 1.0, 0.0).astype(BF16)
    sv = lax.broadcasted_iota(jnp.int32, (2 * C_VP, LANES), 0)
    sd = lax.broadcasted_iota(jnp.int32, (2 * C_VP, LANES), 1)
    own = sv // C_VP == sd // C_KP
    nc = tb // ch
    rowi = lax.broadcasted_iota(jnp.int32, (tb, LANES), 0) % ch

    def chunk_row(x, j):
        w = x.shape[1]
        xj = x.reshape(nc, ch, w)[:, j:j + 1, :]
        return jnp.broadcast_to(xj, (nc, ch, w)).reshape(tb, w)

    terms = []
    for j in range(ch):
        e = jnp.exp(jnp.where(rowi >= j, b - chunk_row(b, j), NEG_INF))
        terms.append((qs * e * chunk_row(kf, j)).astype(BF16))
    sc = jnp.dot(jnp.concatenate(terms, axis=0), red, preferred_element_type=F32)
    parts = [sc[j * tb:(j + 1) * tb, :] * chunk_row(vf, j) for j in range(ch)]
    while len(parts) > 1:
        parts = [parts[i] + parts[i + 1] for i in range(0, len(parts), 2)]

    s_prev = st[...]
    o_inter = []
    for c in range(nc):
        rows = slice(c * ch, (c + 1) * ch)
        o_inter.append(_dot_nt(qt[rows], s_prev.astype(BF16)))
        ut = lax.dot_general(vf[rows].astype(BF16), kt[rows], (((0,), (0,)), ((), ())),
                             preferred_element_type=F32)
        s_prev = gl[c * ch:c * ch + 1, :] * s_prev + jnp.where(own, ut, 0.0)
    st[...] = s_prev

    o = parts[0] + jnp.concatenate(o_inter, axis=0)
    gate = gate_ref[0].astype(F32)
    outs = []
    for hh in range(2):
        oh = o[:, hh * C_VP:(hh + 1) * C_VP]
        ms = jnp.sum(oh * oh, axis=-1, keepdims=True) * (1.0 / C_V)
        gh = gate[:, hh * C_VP:(hh + 1) * C_VP]
        outs.append(oh * lax.rsqrt(ms + LN_EPS) * gn_ref[...] * (gh * jax.nn.sigmoid(gh)))
    o_ref[0] = jnp.concatenate(outs, axis=1).astype(o_ref.dtype)


def _gla(h3, wa, ba, gn, tb=512):
    bsz, seq, _ = h3.shape
    return pl.pallas_call(
        functools.partial(_gla_kernel, tb=tb),
        out_shape=jax.ShapeDtypeStruct((bsz, seq, C_HEADS * C_VP), BF16),
        grid=(bsz, C_HEADS // 2, seq // tb),
        in_specs=[pl.BlockSpec((1, tb, LANES), lambda b, p, t: (b, t, QC // LANES + p)),
                  pl.BlockSpec((1, tb, LANES), lambda b, p, t: (b, t, KC // LANES + p)),
                  pl.BlockSpec((1, tb, 2 * C_VP), lambda b, p, t: (b, t, VC // (2 * C_VP) + p)),
                  pl.BlockSpec((1, tb, 2 * C_VP), lambda b, p, t: (b, t, GC // (2 * C_VP) + p)),
                  pl.BlockSpec((1, tb, LANES), lambda b, p, t: (b, t, RC // LANES)),
                  pl.BlockSpec((1, LANES, LANES), lambda b, p, t: (p, 0, 0)),
                  pl.BlockSpec((1, 1, LANES), lambda b, p, t: (p, 0, 0)),
                  pl.BlockSpec((1, C_VP), lambda b, p, t: (0, 0))],
        out_specs=pl.BlockSpec((1, tb, 2 * C_VP), lambda b, p, t: (b, t, p)),
        scratch_shapes=[pltpu.VMEM((2 * C_VP, LANES), F32)],
        compiler_params=_cparams(("parallel", "parallel", "arbitrary")),
        name="gla",
    )(h3, h3, h3, h3, h3, wa, ba, gn)


def _outproj_kernel(oa_ref, ob_ref, oc_ref, wa_ref, wb_ref, wc_ref, x_ref, g_ref, b_ref, y_ref, yb_ref):
    acc = jnp.dot(oa_ref[...], wa_ref[...], preferred_element_type=F32)
    acc += jnp.dot(ob_ref[...], wb_ref[...], preferred_element_type=F32)
    acc += jnp.dot(oc_ref[...], wc_ref[...], preferred_element_type=F32)
    y = _layer_norm(ALPHA * x_ref[...] + acc, g_ref[...], b_ref[...])
    y_ref[...] = y
    yb_ref[...] = y.astype(BF16)


def _outproj_ln(oa, ob, oc, wa, wb, wc, x, g, b, tm=512):
    t = x.shape[0]
    row = lambda n: pl.BlockSpec((tm, n), lambda i: (i, 0))
    full = lambda a: pl.BlockSpec(a.shape, lambda i: (0, 0))
    return pl.pallas_call(
        _outproj_kernel,
        out_shape=(jax.ShapeDtypeStruct((t, D_MODEL), F32), jax.ShapeDtypeStruct((t, D_MODEL), BF16)),
        grid=(t // tm,),
        in_specs=[row(oa.shape[1]), row(ob.shape[1]), row(oc.shape[1]), full(wa), full(wb), full(wc),
                  row(D_MODEL), full(g), full(b)],
        out_specs=(row(D_MODEL), row(D_MODEL)),
        compiler_params=_cparams(("parallel",)),
        name="outproj_ln",
    )(oa, ob, oc, wa, wb, wc, x, g, b)


def _matmul_kernel(a_ref, w_ref, o_ref):
    o_ref[...] = jnp.dot(a_ref[...], w_ref[...], preferred_element_type=F32).astype(o_ref.dtype)


def _matmul(a, w, out_dtype, tm=256, tn=512):
    m, k = a.shape
    n = w.shape[1]
    return pl.pallas_call(
        _matmul_kernel,
        out_shape=jax.ShapeDtypeStruct((m, n), out_dtype),
        grid=(m // tm, n // tn),
        in_specs=[pl.BlockSpec((tm, k), lambda i, j: (i, 0)), pl.BlockSpec((k, tn), lambda i, j: (0, j))],
        out_specs=pl.BlockSpec((tm, tn), lambda i, j: (i, j)),
        compiler_params=_cparams(("parallel", "arbitrary")),
        name="matmul",
    )(a, w)


def _xattn_kernel(xb_ref, x_ref, wq_ref, k_ref, v_ref, wo_ref, g_ref, b_ref, y_ref, yb_ref):
    q = jnp.dot(xb_ref[0], wq_ref[...], preferred_element_type=F32).astype(BF16)
    outs = []
    for h in range(XA_HEADS):
        cols = slice(h * XA_DIM, (h + 1) * XA_DIM)
        s = _dot_nt(q[:, cols], k_ref[0, :, cols]) * (XA_DIM ** -0.5)
        m = jnp.max(s, axis=-1, keepdims=True)
        p = jnp.exp(s - m)
        l = jnp.sum(p, axis=-1, keepdims=True)
        o = jnp.dot(p.astype(BF16), v_ref[0, :, cols], preferred_element_type=F32)
        outs.append((o / l).astype(BF16))
    o = jnp.concatenate(outs, axis=1)
    xa = jnp.dot(o, wo_ref[...], preferred_element_type=F32)
    y = _layer_norm(ALPHA * x_ref[0] + xa, g_ref[...], b_ref[...])
    y_ref[0] = y
    yb_ref[0] = y.astype(BF16)


def _xattn_ln(xb3, x3, wq, kv3, wo, g, b, tm=512):
    bsz, seq, _ = x3.shape
    mem_len = kv3.shape[1]
    row = pl.BlockSpec((1, tm, D_MODEL), lambda bi, i: (bi, i, 0))
    full = lambda a: pl.BlockSpec(a.shape, lambda bi, i: (0, 0))
    return pl.pallas_call(
        _xattn_kernel,
        out_shape=(jax.ShapeDtypeStruct((bsz, seq, D_MODEL), F32), jax.ShapeDtypeStruct((bsz, seq, D_MODEL), BF16)),
        grid=(bsz, seq // tm),
        in_specs=[row, row, full(wq),
                  pl.BlockSpec((1, mem_len, D_MODEL), lambda bi, i: (bi, 0, 0)),
                  pl.BlockSpec((1, mem_len, D_MODEL), lambda bi, i: (bi, 0, 1)),
                  full(wo), full(g), full(b)],
        out_specs=(row, row),
        compiler_params=_cparams(("parallel", "arbitrary")),
        name="xattn_ln",
    )(xb3, x3, wq, kv3, kv3, wo, g, b)


def _swiglu_chunk(xb, w1, w3, w2):
    hg = jnp.dot(xb, w1, preferred_element_type=F32)
    hu = jnp.dot(xb, w3, preferred_element_type=F32)
    a = hg * jax.nn.sigmoid(hg) * hu
    return jnp.dot(a.astype(BF16), w2, preferred_element_type=F32)


def _ffn_kernel(xb_ref, x_ref, w1_ref, w3_ref, w2_ref, g_ref, b_ref, y_ref, yb_ref, acc_ref):
    j = pl.program_id(1)

    @pl.when(j == 0)
    def _():
        acc_ref[...] = jnp.zeros_like(acc_ref)

    acc_ref[...] += _swiglu_chunk(xb_ref[...], w1_ref[...], w3_ref[...], w2_ref[...])

    @pl.when(j == pl.num_programs(1) - 1)
    def _():
        y = _layer_norm(ALPHA * x_ref[...] + acc_ref[...], g_ref[...], b_ref[...])
        y_ref[...] = y
        yb_ref[...] = y.astype(BF16)


def _ffn_ln(xb, x, w13, w2, g, b, tm=1024, tf=256):
    t = x.shape[0]
    nf = w2.shape[0] // tf
    row = pl.BlockSpec((tm, D_MODEL), lambda i, j: (i, 0))
    vec = lambda a: pl.BlockSpec(a.shape, lambda i, j: (0, 0))
    return pl.pallas_call(
        _ffn_kernel,
        out_shape=(jax.ShapeDtypeStruct((t, D_MODEL), F32), jax.ShapeDtypeStruct((t, D_MODEL), BF16)),
        grid=(t // tm, nf),
        in_specs=[row, row,
                  pl.BlockSpec((D_MODEL, tf), lambda i, j: (0, j)),
                  pl.BlockSpec((D_MODEL, tf), lambda i, j: (0, j + nf)),
                  pl.BlockSpec((tf, D_MODEL), lambda i, j: (j, 0)),
                  vec(g), vec(b)],
        out_specs=(row, row),
        scratch_shapes=[pltpu.VMEM((tm, D_MODEL), F32)],
        compiler_params=_cparams(("parallel", "arbitrary")),
        name="ffn_ln",
    )(xb, x, w13, w13, w2, g, b)


def _router_kernel(x_ref, w_ref, route_ref):
    logits = jnp.dot(x_ref[...], w_ref[...], preferred_element_type=F32, precision=lax.Precision.HIGHEST)
    lane = lax.broadcasted_iota(jnp.int32, logits.shape, 1)
    logits = jnp.where(lane < N_EXPERTS, logits, -jnp.inf)
    v1 = jnp.max(logits, axis=-1, keepdims=True)
    i1 = jnp.min(jnp.where(logits == v1, lane, LANES), axis=-1, keepdims=True)
    rest = jnp.where(lane == i1, -jnp.inf, logits)
    v2 = jnp.max(rest, axis=-1, keepdims=True)
    i2 = jnp.min(jnp.where(rest == v2, lane, LANES), axis=-1, keepdims=True)
    e2 = jnp.exp(v2 - v1)
    w1 = 1.0 / (1.0 + e2)
    w2 = e2 / (1.0 + e2)
    route_ref[...] = (jnp.where(lane == 0, w1, 0.0) + jnp.where(lane == 1, w2, 0.0)
                      + jnp.where(lane == 2, i1.astype(F32), 0.0) + jnp.where(lane == 3, i2.astype(F32), 0.0))


def _router(x, w, tm=1024):
    t = x.shape[0]
    return pl.pallas_call(
        _router_kernel,
        out_shape=jax.ShapeDtypeStruct((t, LANES), F32),
        grid=(t // tm,),
        in_specs=[pl.BlockSpec((tm, D_MODEL), lambda i: (i, 0)), pl.BlockSpec(w.shape, lambda i: (0, 0))],
        out_specs=pl.BlockSpec((tm, LANES), lambda i: (i, 0)),
        compiler_params=_cparams(("parallel",)),
        name="router",
    )(x, w)


def _dispatch_plan(route, tm):
    t = route.shape[0]
    expert = route[:, 2:4].astype(jnp.int32).reshape(-1)
    onehot = (expert[:, None] == jnp.arange(N_EXPERTS, dtype=jnp.int32)[None, :]).astype(jnp.int32)
    csum = jnp.cumsum(onehot, axis=0)
    rank = jnp.sum((csum - 1) * onehot, axis=1)
    counts = csum[-1]
    padded = (counts + tm - 1) // tm * tm
    ends = jnp.cumsum(padded)
    dest = (ends - padded)[expert] + rank
    n_tiles = TOP_K * t // tm + N_EXPERTS
    src = jnp.zeros((n_tiles * tm,), jnp.int32).at[dest].set(jnp.arange(TOP_K * t, dtype=jnp.int32) // TOP_K)
    tile_start = jnp.arange(n_tiles, dtype=jnp.int32) * tm
    tile_expert = jnp.sum((tile_start[:, None] >= ends[None, :]).astype(jnp.int32), axis=1)
    tile_expert = jnp.minimum(tile_expert, N_EXPERTS - 1)
    tile_used = (tile_start < ends[-1]).astype(jnp.int32)
    return src, dest.reshape(t, TOP_K).astype(jnp.int32), tile_expert, tile_used


ROW_SLABS = D_MODEL // LANES


def _slab(ref, s, rows):
    return ref.at[pl.ds(s, rows, stride=ROW_SLABS), :]


def _row_tile(ref, r):
    return ref.at[pl.ds(r * ROW_SLABS, ROW_SLABS), :]


def _gather_kernel(src_ref, x_hbm, o_ref, buf, sem, *, tm):
    base = pl.program_id(0) * tm

    def issue(r, carry):
        pltpu.make_async_copy(_row_tile(x_hbm, src_ref[base + r]), _row_tile(buf, r), sem).start()
        return carry

    lax.fori_loop(0, tm, issue, 0, unroll=8)
    pltpu.make_async_copy(x_hbm.at[pl.ds(0, tm * ROW_SLABS), :], buf, sem).wait()
    for s in range(ROW_SLABS):
        o_ref[:, s * LANES:(s + 1) * LANES] = _slab(buf, s, tm)[...].astype(o_ref.dtype)


def _gather_rows(src, x3, tm):
    n = src.shape[0]
    return pl.pallas_call(
        functools.partial(_gather_kernel, tm=tm),
        out_shape=jax.ShapeDtypeStruct((n, D_MODEL), BF16),
        grid_spec=pltpu.PrefetchScalarGridSpec(
            num_scalar_prefetch=1, grid=(n // tm,),
            in_specs=[pl.BlockSpec(memory_space=pl.ANY)],
            out_specs=pl.BlockSpec((tm, D_MODEL), lambda i, src_ref: (i, 0)),
            scratch_shapes=[pltpu.VMEM((tm * ROW_SLABS, LANES), F32), pltpu.SemaphoreType.DMA]),
        compiler_params=_cparams(("arbitrary",)),
        name="moe_gather",
    )(src, x3)


def _moe_ffn_kernel(te_ref, tu_ref, xs_ref, w1_ref, w3_ref, w2_ref, y_ref, acc_ref):
    i = pl.program_id(0)
    j = pl.program_id(1)
    last = pl.num_programs(1) - 1

    @pl.when(j == 0)
    def _():
        acc_ref[...] = jnp.zeros_like(acc_ref)

    @pl.when(tu_ref[i] > 0)
    def _():
        acc_ref[...] += _swiglu_chunk(xs_ref[...], w1_ref[0], w3_ref[0], w2_ref[0])

    @pl.when(j == last)
    def _():
        for s in range(ROW_SLABS):
            _slab(y_ref, s, acc_ref.shape[0])[...] = acc_ref[:, s * LANES:(s + 1) * LANES]


def _moe_ffn(tile_expert, tile_used, xs, w13, w2, tm, tf):
    n = xs.shape[0]
    nf = w2.shape[1] // tf
    chunk = lambda i, j, tu: jnp.where(tu[i] > 0, j, nf - 1)
    return pl.pallas_call(
        _moe_ffn_kernel,
        out_shape=jax.ShapeDtypeStruct((n * ROW_SLABS, LANES), F32),
        grid_spec=pltpu.PrefetchScalarGridSpec(
            num_scalar_prefetch=2, grid=(n // tm, nf),
            in_specs=[pl.BlockSpec((tm, D_MODEL), lambda i, j, te, tu: (i, 0)),
                      pl.BlockSpec((1, D_MODEL, tf), lambda i, j, te, tu: (te[i], 0, chunk(i, j, tu))),
                      pl.BlockSpec((1, D_MODEL, tf), lambda i, j, te, tu: (te[i], 0, chunk(i, j, tu) + nf)),
                      pl.BlockSpec((1, tf, D_MODEL), lambda i, j, te, tu: (te[i], chunk(i, j, tu), 0))],
            out_specs=pl.BlockSpec((tm * ROW_SLABS, LANES), lambda i, j, te, tu: (i, 0)),
            scratch_shapes=[pltpu.VMEM((tm, D_MODEL), F32)]),
        compiler_params=_cparams(("arbitrary", "arbitrary")),
        name="moe_ffn",
    )(tile_expert, tile_used, xs, w13, w13, w2)


def _combine_kernel(d0_ref, d1_ref, y_hbm, route_ref, x_ref, g_ref, b_ref, o_ref, ob_ref, buf0, buf1, sem, *, tm):
    base = pl.program_id(0) * tm

    def issue(r, carry):
        pltpu.make_async_copy(_row_tile(y_hbm, d0_ref[base + r]), _row_tile(buf0, r), sem).start()
        pltpu.make_async_copy(_row_tile(y_hbm, d1_ref[base + r]), _row_tile(buf1, r), sem).start()
        return carry

    lax.fori_loop(0, tm, issue, 0, unroll=8)
    pltpu.make_async_copy(y_hbm.at[pl.ds(0, tm * ROW_SLABS), :], buf0, sem).wait()
    pltpu.make_async_copy(y_hbm.at[pl.ds(0, tm * ROW_SLABS), :], buf1, sem).wait()
    route = route_ref[...]
    w0, w1 = route[:, 0:1], route[:, 1:2]
    f = jnp.concatenate([w0 * _slab(buf0, s, tm)[...] + w1 * _slab(buf1, s, tm)[...] for s in range(ROW_SLABS)],
                        axis=1)
    y = _layer_norm(ALPHA * x_ref[...] + f, g_ref[...], b_ref[...])
    o_ref[...] = y
    ob_ref[...] = y.astype(BF16)


def _combine_ln(dest, ys, route, x, g, b, tm=512):
    t = x.shape[0]
    row = pl.BlockSpec((tm, D_MODEL), lambda i, d0, d1: (i, 0))
    vec = lambda a: pl.BlockSpec(a.shape, lambda i, d0, d1: (0, 0))
    return pl.pallas_call(
        functools.partial(_combine_kernel, tm=tm),
        out_shape=(jax.ShapeDtypeStruct((t, D_MODEL), F32), jax.ShapeDtypeStruct((t, D_MODEL), BF16)),
        grid_spec=pltpu.PrefetchScalarGridSpec(
            num_scalar_prefetch=2, grid=(t // tm,),
            in_specs=[pl.BlockSpec(memory_space=pl.ANY),
                      pl.BlockSpec((tm, LANES), lambda i, d0, d1: (i, 0)),
                      row, vec(g), vec(b)],
            out_specs=(row, row),
            scratch_shapes=[pltpu.VMEM((tm * ROW_SLABS, LANES), F32), pltpu.VMEM((tm * ROW_SLABS, LANES), F32),
                            pltpu.SemaphoreType.DMA]),
        compiler_params=_cparams(("arbitrary",)),
        name="moe_combine_ln",
    )(dest[:, 0], dest[:, 1], ys, route, x, g, b)


def _moe_ln(x, router_w, w13, w2, g, b, tm=1024, tf=512):
    route = _router(x, router_w)
    src, dest, tile_expert, tile_used = _dispatch_plan(route, tm)
    xs = _gather_rows(src, x.reshape(-1, LANES), tm)
    ys = _moe_ffn(tile_expert, tile_used, xs, w13, w2, tm, tf)
    return _combine_ln(dest, ys, route, x, g, b)


def _pad_heads(w, heads, width, padded, axis):
    shape = list(w.shape)
    shape[axis:axis + 1] = [heads, width]
    w = w.reshape(shape)
    pad = [(0, 0)] * w.ndim
    pad[axis + 1] = (0, padded - width)
    w = jnp.pad(w, pad)
    shape[axis:axis + 2] = [heads * padded]
    return w.reshape(shape)


def _layout_w_in(w):
    bounds = np.cumsum([0, 256, 256, 256, 384, 384, 384, 192, 192, 384, 384, 16])
    qa, ka, va, qb, kb, vb, qc, kc, vc, gc, rc = [w[:, bounds[i]:bounds[i + 1]] for i in range(11)]
    rc = jnp.pad(rc, ((0, 0), (0, LANES - C_RANK)))
    qc = _pad_heads(qc, C_HEADS, C_K, C_KP, 1)
    kc = _pad_heads(kc, C_HEADS, C_K, C_KP, 1)
    vc = _pad_heads(vc, C_HEADS, C_V, C_VP, 1)
    gc = _pad_heads(gc, C_HEADS, C_V, C_VP, 1)
    return jnp.concatenate([qa, ka, qb, kb, va, vb, rc, qc, kc, vc, gc], axis=1).astype(BF16)


def _rope_tables(positions, dim):
    rot = dim // ROPE_FRAC
    inv_freq = ROPE_THETA ** (-jnp.arange(0, rot, 2, dtype=F32) / rot)
    ang = positions.astype(F32).reshape(-1, 1) * inv_freq
    cos, sin = jnp.cos(ang), jnp.sin(ang)
    t = ang.shape[0]
    c = jnp.concatenate([cos, cos, jnp.ones((t, dim - rot), F32)], axis=1)
    s = jnp.concatenate([-sin, sin, jnp.zeros((t, dim - rot), F32)], axis=1)
    reps = LANES // dim
    return jnp.tile(c, (1, reps)), jnp.tile(s, (1, reps))


def kernel(x, mem, positions, w_in, lam_q1, lam_k1, lam_q2, lam_k2, diff_norm_g, gla_wa2, gla_ba, gla_norm_g,
           w_out, ln_mix_g, ln_mix_b, xa_wq, xa_wkv, xa_wo, ln_xa_g, ln_xa_b, ffd_w13, ffd_w2, moe_router,
           moe_w13, moe_w2, ln_ffn_g, ln_ffn_b):
    bsz, seq, _ = x.shape
    t = bsz * seq
    ca, sa = _rope_tables(positions, A_QK)
    cb, sb = _rope_tables(positions, B_DIM)
    tabs = (ca, sa, cb, sb)
    memb = mem.reshape(-1, D_MODEL).astype(BF16)
    mem_len = mem.shape[1]
    row = lambda v: v.reshape(1, -1).astype(F32)

    xf = x.reshape(t, D_MODEL)
    xb = xf.astype(BF16)
    for l in range(DEPTH):
        lam_init = 0.8 - 0.6 * math.exp(-0.3 * l)
        lam = (jnp.exp(jnp.sum(lam_q1[l] * lam_k1[l])) - jnp.exp(jnp.sum(lam_q2[l] * lam_k2[l]))
               + lam_init).reshape(1, 1).astype(F32)
        h = _inproj(xb, _layout_w_in(w_in[l]), tabs)
        h3 = h.reshape(bsz, seq, H_COLS)
        oa = _diff_attention(h3, lam, jnp.tile(row(diff_norm_g[l]), (1, 2)), lam_init)
        ob = _dilated_attention(h3)
        wa = _pad_heads(gla_wa2[l], C_HEADS, C_K, C_KP, 1)
        wa = jnp.pad(wa, ((0, LANES - C_RANK), (0, 0))).reshape(LANES, 2, LANES).transpose(1, 0, 2).astype(BF16)
        ba = _pad_heads(gla_ba[l], C_HEADS, C_K, C_KP, 0).reshape(2, 1, LANES).astype(F32)
        gn = jnp.pad(row(gla_norm_g[l]), ((0, 0), (0, C_VP - C_V)))
        oc = _gla(h3, wa, ba, gn)
        wo = w_out[l]
        wo_a = wo[:A_HEADS * A_V].astype(BF16)
        wo_b = wo[A_HEADS * A_V:A_HEADS * A_V + B_HEADS * B_DIM].astype(BF16)
        wo_c = _pad_heads(wo[A_HEADS * A_V + B_HEADS * B_DIM:], C_HEADS, C_V, C_VP, 0).astype(BF16)
        xf, xb = _outproj_ln(oa.reshape(t, -1), ob.reshape(t, -1), oc.reshape(t, -1), wo_a, wo_b, wo_c,
                             xf, row(ln_mix_g[l]), row(ln_mix_b[l]))

        kv = _matmul(memb, xa_wkv[l].astype(BF16), BF16)
        x3, xb3 = _xattn_ln(xb.reshape(bsz, seq, D_MODEL), xf.reshape(bsz, seq, D_MODEL), xa_wq[l].astype(BF16),
                            kv.reshape(bsz, mem_len, 2 * D_MODEL), xa_wo[l].astype(BF16),
                            row(ln_xa_g[l]), row(ln_xa_b[l]))
        xf, xb = x3.reshape(t, D_MODEL), xb3.reshape(t, D_MODEL)

        if l % 2 == 0:
            xf, xb = _ffn_ln(xb, xf, ffd_w13[l // 2].astype(BF16), ffd_w2[l // 2].astype(BF16),
                             row(ln_ffn_g[l]), row(ln_ffn_b[l]))
        else:
            rw = jnp.pad(moe_router[l // 2], ((0, 0), (0, LANES - N_EXPERTS)))
            xf, xb = _moe_ln(xf, rw, moe_w13[l // 2].astype(BF16), moe_w2[l // 2].astype(BF16),
                             row(ln_ffn_g[l]), row(ln_ffn_b[l]))
    return xf.reshape(bsz, seq, D_MODEL)
```

```python
import functools
import math

import numpy as np
import jax
import jax.numpy as jnp
from jax import lax
from jax.experimental import pallas as pl
from jax.experimental.pallas import tpu as pltpu

F32 = jnp.float32
BF16 = jnp.bfloat16

LANES = 128
VMEM_LIMIT_BYTES = 56 * 1024 * 1024

D_MODEL = 1024
DEPTH = 2
ROPE_THETA = 500000.0
ROPE_FRAC = 4
LN_EPS = 1e-5
NEG_INF = -1e30
ALPHA = (2 * DEPTH) ** 0.25
LOG2_E = math.log2(math.e)

A_HEADS, A_QK, A_V = 4, 32, 64
B_HEADS, B_DIM = 6, 64
B_SPAN = 128
B_DILATIONS = (1, 4, 16)
C_HEADS, C_K, C_V = 4, 48, 96
C_KP, C_VP = 64, 128
C_RANK, C_TAU, C_CHUNK = 16, 16.0, 16
XA_HEADS, XA_DIM = 4, 256
N_EXPERTS, TOP_K = 8, 2

QA, KA, QB, KB, VA, VB, RC, QC, KC, VC, GC, H_COLS = (
    0, 256, 512, 896, 1280, 1536, 1920, 2048, 2304, 2560, 3072, 3584)
IN_TN = 256
ROPE_A_TILES = (QB - QA) // IN_TN
ROPE_B_TILES = (VA - QA) // IN_TN


def _cparams(sem):
    return pltpu.CompilerParams(dimension_semantics=sem, vmem_limit_bytes=VMEM_LIMIT_BYTES)


def _layer_norm(y, g, b):
    mu = jnp.mean(y, axis=-1, keepdims=True)
    yc = y - mu
    var = jnp.mean(yc * yc, axis=-1, keepdims=True)
    return yc * lax.rsqrt(var + LN_EPS) * g + b


def _dot_nt(a, b):
    return lax.dot_general(a, b, (((1,), (1,)), ((), ())), preferred_element_type=F32)


def _inproj_kernel(x_ref, w_ref, ca_ref, sa_ref, cb_ref, sb_ref, o_ref):
    x = x_ref[...]
    lane = lax.broadcasted_iota(jnp.int32, (x.shape[0], LANES), 1)

    def rope_cols(lo, k_lo, hi, c_ref, s_ref, half, period):
        acc = jnp.dot(x, w_ref[:, lo:hi], preferred_element_type=F32)
        c, s = c_ref[...], s_ref[...]
        first = lane % period < half
        for k in range((hi - lo) // LANES):
            a = acc[:, k * LANES:(k + 1) * LANES]
            up = pltpu.roll(a, LANES - half, 1)
            dn = pltpu.roll(a, half, 1)
            r = a * c + jnp.where(first, up, dn) * s
            if lo + k * LANES < k_lo:
                r = r * (period ** -0.5 * LOG2_E)
            o_ref[:, lo + k * LANES:lo + (k + 1) * LANES] = r.astype(o_ref.dtype)

    rope_cols(QA, KA, QB, ca_ref, sa_ref, A_QK // ROPE_FRAC // 2, A_QK)
    rope_cols(QB, KB, VA, cb_ref, sb_ref, B_DIM // ROPE_FRAC // 2, B_DIM)
    o_ref[:, VA:] = jnp.dot(x, w_ref[:, VA:], preferred_element_type=F32).astype(o_ref.dtype)


def _inproj(xb, w, tabs, tm=512):
    t = xb.shape[0]
    tab_spec = pl.BlockSpec((tm, LANES), lambda i: (i, 0))
    return pl.pallas_call(
        _inproj_kernel,
        out_shape=jax.ShapeDtypeStruct((t, H_COLS), BF16),
        grid=(t // tm,),
        in_specs=[pl.BlockSpec((tm, D_MODEL), lambda i: (i, 0)),
                  pl.BlockSpec((D_MODEL, H_COLS), lambda i: (0, 0)),
                  tab_spec, tab_spec, tab_spec, tab_spec],
        out_specs=pl.BlockSpec((tm, H_COLS), lambda i: (i, 0)),
        compiler_params=_cparams(("parallel",)),
        name="inproj_rope",
    )(xb, w, *tabs)


def _diff_kernel(lam_ref, q_ref, k_ref, v_ref, g_ref, o_ref, *, seq, tq, out_scale):
    lane = lax.broadcasted_iota(jnp.int32, (tq, LANES), 1)
    row = lax.broadcasted_iota(jnp.int32, (tq, tq), 0)
    col = lax.broadcasted_iota(jnp.int32, (tq, tq), 1)
    causal = col <= row
    lo = lane < A_V
    lam = lam_ref[0, 0]

    for i in range(seq // tq):
        past = i * tq
        q = q_ref[0, past:past + tq, :]
        k_d = k_ref[0, past:past + tq, :]
        v_d = v_ref[0, past:past + tq, :]

        def attend(c):
            qm = jnp.where(lane // A_QK == c, q, jnp.zeros_like(q))
            s_d = jnp.where(causal, _dot_nt(qm, k_d), NEG_INF)
            m = jnp.max(s_d, axis=-1, keepdims=True)
            if past:
                s_p = _dot_nt(qm, k_ref[0, :past, :])
                m = jnp.maximum(m, jnp.max(s_p, axis=-1, keepdims=True))
            p_d = jnp.exp2(s_d - m)
            l = jnp.sum(p_d, axis=-1, keepdims=True)
            acc = jnp.dot(p_d.astype(BF16), v_d, preferred_element_type=F32)
            if past:
                p_p = jnp.exp2(s_p - m)
                l = l + jnp.sum(p_p, axis=-1, keepdims=True)
                acc = acc + jnp.dot(p_p.astype(BF16), v_ref[0, :past, :], preferred_element_type=F32)
            return acc / l

        o_h0 = attend(0) - lam * attend(1)
        o_h1 = attend(2) - lam * attend(3)
        o = jnp.where(lo, o_h0, o_h1)
        sq = o * o
        s_lo = jnp.sum(jnp.where(lo, sq, 0.0), axis=-1, keepdims=True)
        s_all = jnp.sum(sq, axis=-1, keepdims=True)
        ms = jnp.where(lo, s_lo, s_all - s_lo) * (1.0 / A_V)
        o_ref[0, past:past + tq, :] = (o * lax.rsqrt(ms + LN_EPS) * g_ref[...] * out_scale).astype(o_ref.dtype)


def _diff_attention(h3, lam, g, lam_init, tq=512):
    bsz, seq, _ = h3.shape
    kern = functools.partial(_diff_kernel, seq=seq, tq=tq, out_scale=1.0 - lam_init)
    blk = lambda off: pl.BlockSpec((1, seq, LANES), lambda b, p: (b, 0, off // LANES + p))
    return pl.pallas_call(
        kern,
        out_shape=jax.ShapeDtypeStruct((bsz, seq, A_HEADS * A_V), BF16),
        grid=(bsz, A_HEADS // 2),
        in_specs=[pl.BlockSpec(memory_space=pltpu.SMEM), blk(QA), blk(KA), blk(VA),
                  pl.BlockSpec((1, LANES), lambda b, p: (0, 0))],
        out_specs=pl.BlockSpec((1, seq, LANES), lambda b, p: (b, 0, p)),
        compiler_params=_cparams(("parallel", "parallel")),
        name="diff_attention",
    )(lam, h3, h3, h3, g)


def _dil_kernel(q_ref, k_ref, v_ref, o_ref, qf, kf, vf, qs, ks, vs, m_s, l_s, a_s, *, seq):
    qf[...] = q_ref[0].astype(F32)
    kf[...] = k_ref[0].astype(F32)
    vf[...] = v_ref[0].astype(F32)
    span = B_SPAN
    lo = lax.broadcasted_iota(jnp.int32, (span, LANES), 1) < B_DIM
    own = ((lax.broadcasted_iota(jnp.int32, (2 * span, LANES), 1) < B_DIM)
           == (lax.broadcasted_iota(jnp.int32, (2 * span, LANES), 0) < span))
    qi = lax.broadcasted_iota(jnp.int32, (2 * span, 2 * span), 0) % span
    ki = lax.broadcasted_iota(jnp.int32, (2 * span, 2 * span), 1)
    band = (ki >= qi) & (ki <= qi + span)
    first = (lax.broadcasted_iota(jnp.int32, (2 * span, span), 1)
             <= lax.broadcasted_iota(jnp.int32, (2 * span, span), 0) % span)

    for g, r in enumerate(B_DILATIONS):
        sub = seq // r
        for rho in range(r):
            rows = pl.ds(rho, sub, stride=r) if r > 1 else pl.ds(0, sub)
            qs[rho * sub:(rho + 1) * sub, :] = qf[rows, :].astype(BF16)
            ks[rho * sub:(rho + 1) * sub, :] = kf[rows, :].astype(BF16)
            vs[rho * sub:(rho + 1) * sub, :] = vf[rows, :].astype(BF16)
        for rho in range(r):
            for n in range(sub // span):
                base = rho * sub + n * span
                qb = qs[base:base + span, :]
                lhs = jnp.where(own, jnp.concatenate([qb, qb], axis=0), jnp.zeros((), BF16))
                start = base - span if n else base
                s = jnp.where(band if n else first, _dot_nt(lhs, ks[start:base + span, :]), NEG_INF)
                m = jnp.max(s, axis=-1, keepdims=True)
                p = jnp.exp2(s - m)
                l = jnp.sum(p, axis=-1, keepdims=True)
                a = jnp.dot(p.astype(BF16), vs[start:base + span, :], preferred_element_type=F32)
                m = jnp.broadcast_to(m, (2 * span, LANES))
                l = jnp.broadcast_to(l, (2 * span, LANES))
                m_g = jnp.where(lo, m[:span], m[span:])
                l_g = jnp.where(lo, l[:span], l[span:])
                a_g = jnp.where(lo, a[:span], a[span:])
                nat = pl.ds(rho + r * span * n, span, stride=r) if r > 1 else pl.ds(n * span, span)
                if g == 0:
                    m_s[nat, :] = m_g
                    l_s[nat, :] = l_g
                    a_s[nat, :] = a_g
                else:
                    m_o = m_s[nat, :]
                    m_n = jnp.maximum(m_o, m_g)
                    w_o = jnp.exp2(m_o - m_n)
                    w_g = jnp.exp2(m_g - m_n)
                    m_s[nat, :] = m_n
                    l_s[nat, :] = w_o * l_s[nat, :] + w_g * l_g
                    a_s[nat, :] = w_o * a_s[nat, :] + w_g * a_g

    o_ref[0] = (a_s[...] / l_s[...]).astype(o_ref.dtype)


def _dilated_attention(h3):
    bsz, seq, _ = h3.shape
    blk = lambda off: pl.BlockSpec((1, seq, LANES), lambda b, p: (b, 0, off // LANES + p))
    return pl.pallas_call(
        functools.partial(_dil_kernel, seq=seq),
        out_shape=jax.ShapeDtypeStruct((bsz, seq, B_HEADS * B_DIM), BF16),
        grid=(bsz, B_HEADS // 2),
        in_specs=[blk(QB), blk(KB), blk(VB)],
        out_specs=pl.BlockSpec((1, seq, LANES), lambda b, p: (b, 0, p)),
        scratch_shapes=([pltpu.VMEM((seq, LANES), F32)] * 3 + [pltpu.VMEM((seq, LANES), BF16)] * 3
                        + [pltpu.VMEM((seq, LANES), F32)] * 3),
        compiler_params=_cparams(("parallel", "parallel")),
        name="dilated_attention",
    )(h3, h3, h3)


def _gla_kernel(q_ref, k_ref, v_ref, gate_ref, rc_ref, wa_ref, ba_ref, gn_ref, o_ref, st, *, tb):
    t = pl.program_id(2)

    @pl.when(t == 0)
    def _():
        st[...] = jnp.zeros_like(st)

    ch = C_CHUNK
    z = jnp.dot(rc_ref[0], wa_ref[0], preferred_element_type=F32) + ba_ref[0]
    la = (jnp.minimum(z, 0.0) - jnp.log1p(jnp.exp(-jnp.abs(z)))) * (1.0 / C_TAU)
    ri = lax.broadcasted_iota(jnp.int32, (LANES, LANES), 0)
    ci = lax.broadcasted_iota(jnp.int32, (LANES, LANES), 1)
    same = (ri // ch) == (ci // ch)
    tri = jnp.where(same & (ci <= ri), 1.0, 0.0).astype(F32)
    blk = jnp.where(same, 1.0, 0.0).astype(F32)
    sums = lambda m: jnp.concatenate(
        [jnp.dot(m, la[r:r + LANES], preferred_element_type=F32, precision=lax.Precision.HIGHEST)
         for r in range(0, tb, LANES)], axis=0)
    b = sums(tri)
    bl = sums(blk)
    qs = q_ref[0].astype(F32) * (C_K ** -0.5)
    kf = k_ref[0].astype(F32)
    vf = v_ref[0].astype(F32)
    qt = (qs * jnp.exp(b)).astype(BF16)
    kt = (kf * jnp.exp(bl - b)).astype(BF16)
    gl = jnp.exp(bl)

    dl = lax.broadcasted_iota(jnp.int32, (LANES, 2 * C_VP), 0)
    vl = lax.broadcasted_iota(jnp.int32, (LANES, 2 * C_VP), 1)
    red = jnp.where(dl // C_KP == vl // C_VP, 1.0, 0.0).astype(BF16)
    sv = lax.broadcasted_iota(jnp.int32, (2 * C_VP, LANES), 0)
    sd = lax.broadcasted_iota(jnp.int32, (2 * C_VP, LANES), 1)
    own = sv // C_VP == sd // C_KP
    nc = tb // ch
    rowi = lax.broadcasted_iota(jnp.int32, (tb, LANES), 0) % ch

    def chunk_row(x, j):
        w = x.shape[1]
        xj = x.reshape(nc, ch, w)[:, j:j + 1, :]
        return jnp.broadcast_to(xj, (nc, ch, w)).reshape(tb, w)

    terms = []
    for j in range(ch):
        e = jnp.exp(jnp.where(rowi >= j, b - chunk_row(b, j), NEG_INF))
        terms.append((qs * e * chunk_row(kf, j)).astype(BF16))
    sc = jnp.dot(jnp.concatenate(terms, axis=0), red, preferred_element_type=F32)
    parts = [sc[j * tb:(j + 1) * tb, :] * chunk_row(vf, j) for j in range(ch)]
    while len(parts) > 1:
        parts = [parts[i] + parts[i + 1] for i in range(0, len(parts), 2)]

    s_prev = st[...]
    o_inter = []
    for c in range(nc):
        rows = slice(c * ch, (c + 1) * ch)
        o_inter.append(_dot_nt(qt[rows], s_prev.astype(BF16)))
        ut = lax.dot_general(vf[rows].astype(BF16), kt[rows], (((0,), (0,)), ((), ())),
                             preferred_element_type=F32)
        s_prev = gl[c * ch:c * ch + 1, :] * s_prev + jnp.where(own, ut, 0.0)
    st[...] = s_prev

    o = parts[0] + jnp.concatenate(o_inter, axis=0)
    gate = gate_ref[0].astype(F32)
    outs = []
    for hh in range(2):
        oh = o[:, hh * C_VP:(hh + 1) * C_VP]
        ms = jnp.sum(oh * oh, axis=-1, keepdims=True) * (1.0 / C_V)
        gh = gate[:, hh * C_VP:(hh + 1) * C_VP]
        outs.append(oh * lax.rsqrt(ms + LN_EPS) * gn_ref[...] * (gh * jax.nn.sigmoid(gh)))
    o_ref[0] = jnp.concatenate(outs, axis=1).astype(o_ref.dtype)


def _gla(h3, wa, ba, gn, tb=512):
    bsz, seq, _ = h3.shape
    return pl.pallas_call(
        functools.partial(_gla_kernel, tb=tb),
        out_shape=jax.ShapeDtypeStruct((bsz, seq, C_HEADS * C_VP), BF16),
        grid=(bsz, C_HEADS // 2, seq // tb),
        in_specs=[pl.BlockSpec((1, tb, LANES), lambda b, p, t: (b, t, QC // LANES + p)),
                  pl.BlockSpec((1, tb, LANES), lambda b, p, t: (b, t, KC // LANES + p)),
                  pl.BlockSpec((1, tb, 2 * C_VP), lambda b, p, t: (b, t, VC // (2 * C_VP) + p)),
                  pl.BlockSpec((1, tb, 2 * C_VP), lambda b, p, t: (b, t, GC // (2 * C_VP) + p)),
                  pl.BlockSpec((1, tb, LANES), lambda b, p, t: (b, t, RC // LANES)),
                  pl.BlockSpec((1, LANES, LANES), lambda b, p, t: (p, 0, 0)),
                  pl.BlockSpec((1, 1, LANES), lambda b, p, t: (p, 0, 0)),
                  pl.BlockSpec((1, C_VP), lambda b, p, t: (0, 0))],
        out_specs=pl.BlockSpec((1, tb, 2 * C_VP), lambda b, p, t: (b, t, p)),
        scratch_shapes=[pltpu.VMEM((2 * C_VP, LANES), F32)],
        compiler_params=_cparams(("parallel", "parallel", "arbitrary")),
        name="gla",
    )(h3, h3, h3, h3, h3, wa, ba, gn)


def _outproj_kernel(oa_ref, ob_ref, oc_ref, wa_ref, wb_ref, wc_ref, x_ref, g_ref, b_ref, y_ref, yb_ref):
    acc = jnp.dot(oa_ref[...], wa_ref[...], preferred_element_type=F32)
    acc += jnp.dot(ob_ref[...], wb_ref[...], preferred_element_type=F32)
    acc += jnp.dot(oc_ref[...], wc_ref[...], preferred_element_type=F32)
    y = _layer_norm(ALPHA * x_ref[...] + acc, g_ref[...], b_ref[...])
    y_ref[...] = y
    yb_ref[...] = y.astype(BF16)


def _outproj_ln(oa, ob, oc, wa, wb, wc, x, g, b, tm=512):
    t = x.shape[0]
    row = lambda n: pl.BlockSpec((tm, n), lambda i: (i, 0))
    full = lambda a: pl.BlockSpec(a.shape, lambda i: (0, 0))
    return pl.pallas_call(
        _outproj_kernel,
        out_shape=(jax.ShapeDtypeStruct((t, D_MODEL), F32), jax.ShapeDtypeStruct((t, D_MODEL), BF16)),
        grid=(t // tm,),
        in_specs=[row(oa.shape[1]), row(ob.shape[1]), row(oc.shape[1]), full(wa), full(wb), full(wc),
                  row(D_MODEL), full(g), full(b)],
        out_specs=(row(D_MODEL), row(D_MODEL)),
        compiler_params=_cparams(("parallel",)),
        name="outproj_ln",
    )(oa, ob, oc, wa, wb, wc, x, g, b)


def _matmul_kernel(a_ref, w_ref, o_ref):
    o_ref[...] = jnp.dot(a_ref[...], w_ref[...], preferred_element_type=F32).astype(o_ref.dtype)


def _matmul(a, w, out_dtype, tm=256, tn=512):
    m, k = a.shape
    n = w.shape[1]
    return pl.pallas_call(
        _matmul_kernel,
        out_shape=jax.ShapeDtypeStruct((m, n), out_dtype),
        grid=(m // tm, n // tn),
        in_specs=[pl.BlockSpec((tm, k), lambda i, j: (i, 0)), pl.BlockSpec((k, tn), lambda i, j: (0, j))],
        out_specs=pl.BlockSpec((tm, tn), lambda i, j: (i, j)),
        compiler_params=_cparams(("parallel", "arbitrary")),
        name="matmul",
    )(a, w)


def _xattn_kernel(xb_ref, x_ref, wq_ref, k_ref, v_ref, wo_ref, g_ref, b_ref, y_ref, yb_ref):
    q = jnp.dot(xb_ref[0], wq_ref[...], preferred_element_type=F32).astype(BF16)
    outs = []
    for h in range(XA_HEADS):
        cols = slice(h * XA_DIM, (h + 1) * XA_DIM)
        s = _dot_nt(q[:, cols], k_ref[0, :, cols]) * (XA_DIM ** -0.5)
        m = jnp.max(s, axis=-1, keepdims=True)
        p = jnp.exp(s - m)
        l = jnp.sum(p, axis=-1, keepdims=True)
        o = jnp.dot(p.astype(BF16), v_ref[0, :, cols], preferred_element_type=F32)
        outs.append((o / l).astype(BF16))
    o = jnp.concatenate(outs, axis=1)
    xa = jnp.dot(o, wo_ref[...], preferred_element_type=F32)
    y = _layer_norm(ALPHA * x_ref[0] + xa, g_ref[...], b_ref[...])
    y_ref[0] = y
    yb_ref[0] = y.astype(BF16)


def _xattn_ln(xb3, x3, wq, kv3, wo, g, b, tm=512):
    bsz, seq, _ = x3.shape
    mem_len = kv3.shape[1]
    row = pl.BlockSpec((1, tm, D_MODEL), lambda bi, i: (bi, i, 0))
    full = lambda a: pl.BlockSpec(a.shape, lambda bi, i: (0, 0))
    return pl.pallas_call(
        _xattn_kernel,
        out_shape=(jax.ShapeDtypeStruct((bsz, seq, D_MODEL), F32), jax.ShapeDtypeStruct((bsz, seq, D_MODEL), BF16)),
        grid=(bsz, seq // tm),
        in_specs=[row, row, full(wq),
                  pl.BlockSpec((1, mem_len, D_MODEL), lambda bi, i: (bi, 0, 0)),
                  pl.BlockSpec((1, mem_len, D_MODEL), lambda bi, i: (bi, 0, 1)),
                  full(wo), full(g), full(b)],
        out_specs=(row, row),
        compiler_params=_cparams(("parallel", "arbitrary")),
        name="xattn_ln",
    )(xb3, x3, wq, kv3, kv3, wo, g, b)


def _swiglu_chunk(xb, w1, w3, w2):
    hg = jnp.dot(xb, w1, preferred_element_type=F32)
    hu = jnp.dot(xb, w3, preferred_element_type=F32)
    a = hg * jax.nn.sigmoid(hg) * hu
    return jnp.dot(a.astype(BF16), w2, preferred_element_type=F32)


def _ffn_kernel(xb_ref, x_ref, w1_ref, w3_ref, w2_ref, g_ref, b_ref, y_ref, yb_ref, acc_ref):
    j = pl.program_id(1)

    @pl.when(j == 0)
    def _():
        acc_ref[...] = jnp.zeros_like(acc_ref)

    acc_ref[...] += _swiglu_chunk(xb_ref[...], w1_ref[...], w3_ref[...], w2_ref[...])

    @pl.when(j == pl.num_programs(1) - 1)
    def _():
        y = _layer_norm(ALPHA * x_ref[...] + acc_ref[...], g_ref[...], b_ref[...])
        y_ref[...] = y
        yb_ref[...] = y.astype(BF16)


def _ffn_ln(xb, x, w13, w2, g, b, tm=1024, tf=256):
    t = x.shape[0]
    nf = w2.shape[0] // tf
    row = pl.BlockSpec((tm, D_MODEL), lambda i, j: (i, 0))
    vec = lambda a: pl.BlockSpec(a.shape, lambda i, j: (0, 0))
    return pl.pallas_call(
        _ffn_kernel,
        out_shape=(jax.ShapeDtypeStruct((t, D_MODEL), F32), jax.ShapeDtypeStruct((t, D_MODEL), BF16)),
        grid=(t // tm, nf),
        in_specs=[row, row,
                  pl.BlockSpec((D_MODEL, tf), lambda i, j: (0, j)),
                  pl.BlockSpec((D_MODEL, tf), lambda i, j: (0, j + nf)),
                  pl.BlockSpec((tf, D_MODEL), lambda i, j: (j, 0)),
                  vec(g), vec(b)],
        out_specs=(row, row),
        scratch_shapes=[pltpu.VMEM((tm, D_MODEL), F32)],
        compiler_params=_cparams(("parallel", "arbitrary")),
        name="ffn_ln",
    )(xb, x, w13, w13, w2, g, b)


def _router_kernel(x_ref, w_ref, route_ref):
    logits = jnp.dot(x_ref[...], w_ref[...], preferred_element_type=F32, precision=lax.Precision.HIGHEST)
    lane = lax.broadcasted_iota(jnp.int32, logits.shape, 1)
    logits = jnp.where(lane < N_EXPERTS, logits, -jnp.inf)
    v1 = jnp.max(logits, axis=-1, keepdims=True)
    i1 = jnp.min(jnp.where(logits == v1, lane, LANES), axis=-1, keepdims=True)
    rest = jnp.where(lane == i1, -jnp.inf, logits)
    v2 = jnp.max(rest, axis=-1, keepdims=True)
    i2 = jnp.min(jnp.where(rest == v2, lane, LANES), axis=-1, keepdims=True)
    e2 = jnp.exp(v2 - v1)
    w1 = 1.0 / (1.0 + e2)
    w2 = e2 / (1.0 + e2)
    route_ref[...] = (jnp.where(lane == 0, w1, 0.0) + jnp.where(lane == 1, w2, 0.0)
                      + jnp.where(lane == 2, i1.astype(F32), 0.0) + jnp.where(lane == 3, i2.astype(F32), 0.0))


def _router(x, w, tm=1024):
    t = x.shape[0]
    return pl.pallas_call(
        _router_kernel,
        out_shape=jax.ShapeDtypeStruct((t, LANES), F32),
        grid=(t // tm,),
        in_specs=[pl.BlockSpec((tm, D_MODEL), lambda i: (i, 0)), pl.BlockSpec(w.shape, lambda i: (0, 0))],
        out_specs=pl.BlockSpec((tm, LANES), lambda i: (i, 0)),
        compiler_params=_cparams(("parallel",)),
        name="router",
    )(x, w)


def _dispatch_plan(route, tm):
    t = route.shape[0]
    expert = route[:, 2:4].astype(jnp.int32).reshape(-1)
    onehot = (expert[:, None] == jnp.arange(N_EXPERTS, dtype=jnp.int32)[None, :]).astype(jnp.int32)
    csum = jnp.cumsum(onehot, axis=0)
    rank = jnp.sum((csum - 1) * onehot, axis=1)
    counts = csum[-1]
    padded = (counts + tm - 1) // tm * tm
    ends = jnp.cumsum(padded)
    dest = (ends - padded)[expert] + rank
    n_tiles = TOP_K * t // tm + N_EXPERTS
    src = jnp.zeros((n_tiles * tm,), jnp.int32).at[dest].set(jnp.arange(TOP_K * t, dtype=jnp.int32) // TOP_K)
    tile_start = jnp.arange(n_tiles, dtype=jnp.int32) * tm
    tile_expert = jnp.sum((tile_start[:, None] >= ends[None, :]).astype(jnp.int32), axis=1)
    tile_expert = jnp.minimum(tile_expert, N_EXPERTS - 1)
    tile_used = (tile_start < ends[-1]).astype(jnp.int32)
    return src, dest.reshape(t, TOP_K).astype(jnp.int32), tile_expert, tile_used


ROW_SLABS = D_MODEL // LANES


def _slab(ref, s, rows):
    return ref.at[pl.ds(s, rows, stride=ROW_SLABS), :]


def _row_tile(ref, r):
    return ref.at[pl.ds(r * ROW_SLABS, ROW_SLABS), :]


DMA_THREADS = 2


def _gather_kernel(used_ref, src_ref, x_hbm, o_ref, buf, sem, *, tm):
    used = used_ref[pl.program_id(0)] > 0

    @pl.when(used)
    def _():
        def issue(r, carry):
            for par in range(DMA_THREADS):
                row = DMA_THREADS * r + par
                pltpu.make_async_copy(_row_tile(x_hbm, src_ref[0, 0, row]), _row_tile(buf, row), sem).start(priority=par)
            return carry

        lax.fori_loop(0, tm // DMA_THREADS, issue, 0, unroll=4)
        pltpu.make_async_copy(x_hbm.at[pl.ds(0, tm * ROW_SLABS), :], buf, sem).wait()
        for s in range(ROW_SLABS):
            o_ref[:, s * LANES:(s + 1) * LANES] = _slab(buf, s, tm)[...].astype(o_ref.dtype)

    @pl.when(jnp.logical_not(used))
    def _():
        o_ref[...] = jnp.zeros_like(o_ref)


def _gather_rows(tile_used, src, x3, tm):
    n_tiles = src.shape[0]
    return pl.pallas_call(
        functools.partial(_gather_kernel, tm=tm),
        out_shape=jax.ShapeDtypeStruct((n_tiles * tm, D_MODEL), BF16),
        grid_spec=pltpu.PrefetchScalarGridSpec(
            num_scalar_prefetch=1, grid=(n_tiles,),
            in_specs=[pl.BlockSpec((1, 1, tm), lambda i, used: (i, 0, 0), memory_space=pltpu.SMEM),
                      pl.BlockSpec(memory_space=pl.ANY)],
            out_specs=pl.BlockSpec((tm, D_MODEL), lambda i, used: (i, 0)),
            scratch_shapes=[pltpu.VMEM((tm * ROW_SLABS, LANES), F32), pltpu.SemaphoreType.DMA]),
        compiler_params=_cparams(("arbitrary",)),
        name="moe_gather",
    )(tile_used, src, x3)


def _moe_ffn_kernel(te_ref, tu_ref, xs_ref, w1_ref, w3_ref, w2_ref, y_ref, acc_ref):
    i = pl.program_id(0)
    j = pl.program_id(1)
    last = pl.num_programs(1) - 1

    @pl.when(j == 0)
    def _():
        acc_ref[...] = jnp.zeros_like(acc_ref)

    @pl.when(tu_ref[i] > 0)
    def _():
        acc_ref[...] += _swiglu_chunk(xs_ref[...], w1_ref[0], w3_ref[0], w2_ref[0])

    @pl.when(j == last)
    def _():
        for s in range(ROW_SLABS):
            _slab(y_ref, s, acc_ref.shape[0])[...] = acc_ref[:, s * LANES:(s + 1) * LANES]


def _moe_ffn(tile_expert, tile_used, xs, w13, w2, tm, tf):
    n = xs.shape[0]
    nf = w2.shape[1] // tf
    chunk = lambda i, j, tu: jnp.where(tu[i] > 0, j, nf - 1)
    return pl.pallas_call(
        _moe_ffn_kernel,
        out_shape=jax.ShapeDtypeStruct((n * ROW_SLABS, LANES), F32),
        grid_spec=pltpu.PrefetchScalarGridSpec(
            num_scalar_prefetch=2, grid=(n // tm, nf),
            in_specs=[pl.BlockSpec((tm, D_MODEL), lambda i, j, te, tu: (i, 0)),
                      pl.BlockSpec((1, D_MODEL, tf), lambda i, j, te, tu: (te[i], 0, chunk(i, j, tu))),
                      pl.BlockSpec((1, D_MODEL, tf), lambda i, j, te, tu: (te[i], 0, chunk(i, j, tu) + nf)),
                      pl.BlockSpec((1, tf, D_MODEL), lambda i, j, te, tu: (te[i], chunk(i, j, tu), 0))],
            out_specs=pl.BlockSpec((tm * ROW_SLABS, LANES), lambda i, j, te, tu: (i, 0)),
            scratch_shapes=[pltpu.VMEM((tm, D_MODEL), F32)]),
        compiler_params=_cparams(("arbitrary", "arbitrary")),
        name="moe_ffn",
    )(tile_expert, tile_used, xs, w13, w13, w2)


def _combine_kernel(d0_ref, d1_ref, y_hbm, route_ref, x_ref, g_ref, b_ref, o_ref, ob_ref, buf0, buf1, sem, *, tm):
    def issue(r, carry):
        pltpu.make_async_copy(_row_tile(y_hbm, d0_ref[0, 0, r]), _row_tile(buf0, r), sem).start(priority=0)
        pltpu.make_async_copy(_row_tile(y_hbm, d1_ref[0, 0, r]), _row_tile(buf1, r), sem).start(priority=1)
        return carry

    lax.fori_loop(0, tm, issue, 0, unroll=8)
    pltpu.make_async_copy(y_hbm.at[pl.ds(0, tm * ROW_SLABS), :], buf0, sem).wait()
    pltpu.make_async_copy(y_hbm.at[pl.ds(0, tm * ROW_SLABS), :], buf1, sem).wait()
    route = route_ref[...]
    w0, w1 = route[:, 0:1], route[:, 1:2]
    f = jnp.concatenate([w0 * _slab(buf0, s, tm)[...] + w1 * _slab(buf1, s, tm)[...] for s in range(ROW_SLABS)],
                        axis=1)
    y = _layer_norm(ALPHA * x_ref[...] + f, g_ref[...], b_ref[...])
    o_ref[...] = y
    ob_ref[...] = y.astype(BF16)


def _combine_ln(dest, ys, route, x, g, b, tm=512):
    t = x.shape[0]
    row = pl.BlockSpec((tm, D_MODEL), lambda i: (i, 0))
    vec = lambda a: pl.BlockSpec(a.shape, lambda i: (0, 0))
    idx = pl.BlockSpec((1, 1, tm), lambda i: (i, 0, 0), memory_space=pltpu.SMEM)
    return pl.pallas_call(
        functools.partial(_combine_kernel, tm=tm),
        out_shape=(jax.ShapeDtypeStruct((t, D_MODEL), F32), jax.ShapeDtypeStruct((t, D_MODEL), BF16)),
        grid=(t // tm,),
        in_specs=[idx, idx, pl.BlockSpec(memory_space=pl.ANY),
                  pl.BlockSpec((tm, LANES), lambda i: (i, 0)),
                  row, vec(g), vec(b)],
        out_specs=(row, row),
        scratch_shapes=[pltpu.VMEM((tm * ROW_SLABS, LANES), F32), pltpu.VMEM((tm * ROW_SLABS, LANES), F32),
                        pltpu.SemaphoreType.DMA],
        compiler_params=_cparams(("arbitrary",)),
        name="moe_combine_ln",
    )(dest[:, 0].reshape(-1, 1, tm), dest[:, 1].reshape(-1, 1, tm), ys, route, x, g, b)


def _moe_ln(x, router_w, w13, w2, g, b, tm=1024, tf=512):
    route = _router(x, router_w)
    src, dest, tile_expert, tile_used = _dispatch_plan(route, tm)
    xs = _gather_rows(tile_used, src.reshape(-1, 1, tm), x.reshape(-1, LANES), tm)
    ys = _moe_ffn(tile_expert, tile_used, xs, w13, w2, tm, tf)
    return _combine_ln(dest, ys, route, x, g, b)


def _pad_heads(w, heads, width, padded, axis):
    shape = list(w.shape)
    shape[axis:axis + 1] = [heads, width]
    w = w.reshape(shape)
    pad = [(0, 0)] * w.ndim
    pad[axis + 1] = (0, padded - width)
    w = jnp.pad(w, pad)
    shape[axis:axis + 2] = [heads * padded]
    return w.reshape(shape)


def _layout_w_in(w):
    bounds = np.cumsum([0, 256, 256, 256, 384, 384, 384, 192, 192, 384, 384, 16])
    qa, ka, va, qb, kb, vb, qc, kc, vc, gc, rc = [w[:, bounds[i]:bounds[i + 1]] for i in range(11)]
    rc = jnp.pad(rc, ((0, 0), (0, LANES - C_RANK)))
    qc = _pad_heads(qc, C_HEADS, C_K, C_KP, 1)
    kc = _pad_heads(kc, C_HEADS, C_K, C_KP, 1)
    vc = _pad_heads(vc, C_HEADS, C_V, C_VP, 1)
    gc = _pad_heads(gc, C_HEADS, C_V, C_VP, 1)
    return jnp.concatenate([qa, ka, qb, kb, va, vb, rc, qc, kc, vc, gc], axis=1).astype(BF16)


def _rope_tables(positions, dim):
    rot = dim // ROPE_FRAC
    inv_freq = ROPE_THETA ** (-jnp.arange(0, rot, 2, dtype=F32) / rot)
    ang = positions.astype(F32).reshape(-1, 1) * inv_freq
    cos, sin = jnp.cos(ang), jnp.sin(ang)
    t = ang.shape[0]
    c = jnp.concatenate([cos, cos, jnp.ones((t, dim - rot), F32)], axis=1)
    s = jnp.concatenate([-sin, sin, jnp.zeros((t, dim - rot), F32)], axis=1)
    reps = LANES // dim
    return jnp.tile(c, (1, reps)), jnp.tile(s, (1, reps))


def kernel(x, mem, positions, w_in, lam_q1, lam_k1, lam_q2, lam_k2, diff_norm_g, gla_wa2, gla_ba, gla_norm_g,
           w_out, ln_mix_g, ln_mix_b, xa_wq, xa_wkv, xa_wo, ln_xa_g, ln_xa_b, ffd_w13, ffd_w2, moe_router,
           moe_w13, moe_w2, ln_ffn_g, ln_ffn_b):
    bsz, seq, _ = x.shape
    t = bsz * seq
    ca, sa = _rope_tables(positions, A_QK)
    cb, sb = _rope_tables(positions, B_DIM)
    tabs = (ca, sa, cb, sb)
    memb = mem.reshape(-1, D_MODEL).astype(BF16)
    mem_len = mem.shape[1]
    row = lambda v: v.reshape(1, -1).astype(F32)

    xf = x.reshape(t, D_MODEL)
    xb = xf.astype(BF16)
    for l in range(DEPTH):
        lam_init = 0.8 - 0.6 * math.exp(-0.3 * l)
        lam = (jnp.exp(jnp.sum(lam_q1[l] * lam_k1[l])) - jnp.exp(jnp.sum(lam_q2[l] * lam_k2[l]))
               + lam_init).reshape(1, 1).astype(F32)
        h = _inproj(xb, _layout_w_in(w_in[l]), tabs)
        h3 = h.reshape(bsz, seq, H_COLS)
        oa = _diff_attention(h3, lam, jnp.tile(row(diff_norm_g[l]), (1, 2)), lam_init)
        ob = _dilated_attention(h3)
        wa = _pad_heads(gla_wa2[l], C_HEADS, C_K, C_KP, 1)
        wa = jnp.pad(wa, ((0, LANES - C_RANK), (0, 0))).reshape(LANES, 2, LANES).transpose(1, 0, 2).astype(BF16)
        ba = _pad_heads(gla_ba[l], C_HEADS, C_K, C_KP, 0).reshape(2, 1, LANES).astype(F32)
        gn = jnp.pad(row(gla_norm_g[l]), ((0, 0), (0, C_VP - C_V)))
        oc = _gla(h3, wa, ba, gn)
        wo = w_out[l]
        wo_a = wo[:A_HEADS * A_V].astype(BF16)
        wo_b = wo[A_HEADS * A_V:A_HEADS * A_V + B_HEADS * B_DIM].astype(BF16)
        wo_c = _pad_heads(wo[A_HEADS * A_V + B_HEADS * B_DIM:], C_HEADS, C_V, C_VP, 0).astype(BF16)
        xf, xb = _outproj_ln(oa.reshape(t, -1), ob.reshape(t, -1), oc.reshape(t, -1), wo_a, wo_b, wo_c,
                             xf, row(ln_mix_g[l]), row(ln_mix_b[l]))

        kv = _matmul(memb, xa_wkv[l].astype(BF16), BF16)
        x3, xb3 = _xattn_ln(xb.reshape(bsz, seq, D_MODEL), xf.reshape(bsz, seq, D_MODEL), xa_wq[l].astype(BF16),
                            kv.reshape(bsz, mem_len, 2 * D_MODEL), xa_wo[l].astype(BF16),
                            row(ln_xa_g[l]), row(ln_xa_b[l]))
        xf, xb = x3.reshape(t, D_MODEL), xb3.reshape(t, D_MODEL)

        if l % 2 == 0:
            xf, xb = _ffn_ln(xb, xf, ffd_w13[l // 2].astype(BF16), ffd_w2[l // 2].astype(BF16),
                             row(ln_ffn_g[l]), row(ln_ffn_b[l]))
        else:
            rw = jnp.pad(moe_router[l // 2], ((0, 0), (0, LANES - N_EXPERTS)))
            xf, xb = _moe_ln(xf, rw, moe_w13[l // 2].astype(BF16), moe_w2[l // 2].astype(BF16),
                             row(ln_ffn_g[l]), row(ln_ffn_b[l]))
    return xf.reshape(bsz, seq, D_MODEL)
```

```python
import functools
import math

import numpy as np
import jax
import jax.numpy as jnp
from jax import lax
from jax.experimental import pallas as pl
from jax.experimental.pallas import tpu as pltpu

F32 = jnp.float32
BF16 = jnp.bfloat16

LANES = 128
VMEM_LIMIT_BYTES = 56 * 1024 * 1024

D_MODEL = 1024
DEPTH = 2
ROPE_THETA = 500000.0
ROPE_FRAC = 4
LN_EPS = 1e-5
NEG_INF = -1e30
ALPHA = (2 * DEPTH) ** 0.25
LOG2_E = math.log2(math.e)

A_HEADS, A_QK, A_V = 4, 32, 64
B_HEADS, B_DIM = 6, 64
B_SPAN = 128
B_DILATIONS = (1, 4, 16)
C_HEADS, C_K, C_V = 4, 48, 96
C_KP, C_VP = 64, 128
C_RANK, C_TAU, C_CHUNK = 16, 16.0, 16
XA_HEADS, XA_DIM = 4, 256
N_EXPERTS, TOP_K = 8, 2

QA, KA, QB, KB, VA, VB, RC, QC, KC, VC, GC, H_COLS = (
    0, 256, 512, 896, 1280, 1536, 1920, 2048, 2304, 2560, 3072, 3584)
IN_TN = 256
ROPE_A_TILES = (QB - QA) // IN_TN
ROPE_B_TILES = (VA - QA) // IN_TN


def _cparams(sem):
    return pltpu.CompilerParams(dimension_semantics=sem, vmem_limit_bytes=VMEM_LIMIT_BYTES)


def _layer_norm(y, g, b):
    mu = jnp.mean(y, axis=-1, keepdims=True)
    yc = y - mu
    var = jnp.mean(yc * yc, axis=-1, keepdims=True)
    return yc * lax.rsqrt(var + LN_EPS) * g + b


def _dot_nt(a, b):
    return lax.dot_general(a, b, (((1,), (1,)), ((), ())), preferred_element_type=F32)


def _inproj_kernel(x_ref, w_ref, ca_ref, sa_ref, cb_ref, sb_ref, o_ref):
    x = x_ref[...]
    lane = lax.broadcasted_iota(jnp.int32, (x.shape[0], LANES), 1)

    def rope_cols(lo, k_lo, hi, c_ref, s_ref, half, period):
        acc = jnp.dot(x, w_ref[:, lo:hi], preferred_element_type=F32)
        c, s = c_ref[...], s_ref[...]
        first = lane % period < half
        for k in range((hi - lo) // LANES):
            a = acc[:, k * LANES:(k + 1) * LANES]
            up = pltpu.roll(a, LANES - half, 1)
            dn = pltpu.roll(a, half, 1)
            r = a * c + jnp.where(first, up, dn) * s
            if lo + k * LANES < k_lo:
                r = r * (period ** -0.5 * LOG2_E)
            o_ref[:, lo + k * LANES:lo + (k + 1) * LANES] = r.astype(o_ref.dtype)

    rope_cols(QA, KA, QB, ca_ref, sa_ref, A_QK // ROPE_FRAC // 2, A_QK)
    rope_cols(QB, KB, VA, cb_ref, sb_ref, B_DIM // ROPE_FRAC // 2, B_DIM)
    o_ref[:, VA:] = jnp.dot(x, w_ref[:, VA:], preferred_element_type=F32).astype(o_ref.dtype)


def _inproj(xb, w, tabs, tm=512):
    t = xb.shape[0]
    tab_spec = pl.BlockSpec((tm, LANES), lambda i: (i, 0))
    return pl.pallas_call(
        _inproj_kernel,
        out_shape=jax.ShapeDtypeStruct((t, H_COLS), BF16),
        grid=(t // tm,),
        in_specs=[pl.BlockSpec((tm, D_MODEL), lambda i: (i, 0)),
                  pl.BlockSpec((D_MODEL, H_COLS), lambda i: (0, 0)),
                  tab_spec, tab_spec, tab_spec, tab_spec],
        out_specs=pl.BlockSpec((tm, H_COLS), lambda i: (i, 0)),
        compiler_params=_cparams(("parallel",)),
        name="inproj_rope",
    )(xb, w, *tabs)


def _diff_kernel(lam_ref, q_ref, k_ref, v_ref, g_ref, o_ref, *, seq, tq, out_scale):
    lane = lax.broadcasted_iota(jnp.int32, (tq, LANES), 1)
    row = lax.broadcasted_iota(jnp.int32, (tq, tq), 0)
    col = lax.broadcasted_iota(jnp.int32, (tq, tq), 1)
    causal = col <= row
    lo = lane < A_V
    lam = lam_ref[0, 0]

    for i in range(seq // tq):
        past = i * tq
        q = q_ref[0, past:past + tq, :]
        k_d = k_ref[0, past:past + tq, :]
        v_d = v_ref[0, past:past + tq, :]

        def attend(c):
            qm = jnp.where(lane // A_QK == c, q, jnp.zeros_like(q))
            s_d = jnp.where(causal, _dot_nt(qm, k_d), NEG_INF)
            m = jnp.max(s_d, axis=-1, keepdims=True)
            if past:
                s_p = _dot_nt(qm, k_ref[0, :past, :])
                m = jnp.maximum(m, jnp.max(s_p, axis=-1, keepdims=True))
            p_d = jnp.exp2(s_d - m)
            l = jnp.sum(p_d, axis=-1, keepdims=True)
            acc = jnp.dot(p_d.astype(BF16), v_d, preferred_element_type=F32)
            if past:
                p_p = jnp.exp2(s_p - m)
                l = l + jnp.sum(p_p, axis=-1, keepdims=True)
                acc = acc + jnp.dot(p_p.astype(BF16), v_ref[0, :past, :], preferred_element_type=F32)
            return acc / l

        o_h0 = attend(0) - lam * attend(1)
        o_h1 = attend(2) - lam * attend(3)
        o = jnp.where(lo, o_h0, o_h1)
        sq = o * o
        s_lo = jnp.sum(jnp.where(lo, sq, 0.0), axis=-1, keepdims=True)
        s_all = jnp.sum(sq, axis=-1, keepdims=True)
        ms = jnp.where(lo, s_lo, s_all - s_lo) * (1.0 / A_V)
        o_ref[0, past:past + tq, :] = (o * lax.rsqrt(ms + LN_EPS) * g_ref[...] * out_scale).astype(o_ref.dtype)


def _diff_attention(h3, lam, g, lam_init, tq=512):
    bsz, seq, _ = h3.shape
    kern = functools.partial(_diff_kernel, seq=seq, tq=tq, out_scale=1.0 - lam_init)
    blk = lambda off: pl.BlockSpec((1, seq, LANES), lambda b, p: (b, 0, off // LANES + p))
    return pl.pallas_call(
        kern,
        out_shape=jax.ShapeDtypeStruct((bsz, seq, A_HEADS * A_V), BF16),
        grid=(bsz, A_HEADS // 2),
        in_specs=[pl.BlockSpec(memory_space=pltpu.SMEM), blk(QA), blk(KA), blk(VA),
                  pl.BlockSpec((1, LANES), lambda b, p: (0, 0))],
        out_specs=pl.BlockSpec((1, seq, LANES), lambda b, p: (b, 0, p)),
        compiler_params=_cparams(("parallel", "parallel")),
        name="diff_attention",
    )(lam, h3, h3, h3, g)


def _dil_kernel(q_ref, k_ref, v_ref, o_ref, qf, kf, vf, qs, ks, vs, m_s, l_s, a_s, *, seq):
    qf[...] = q_ref[0].astype(F32)
    kf[...] = k_ref[0].astype(F32)
    vf[...] = v_ref[0].astype(F32)
    span = B_SPAN
    lo = lax.broadcasted_iota(jnp.int32, (span, LANES), 1) < B_DIM
    own = ((lax.broadcasted_iota(jnp.int32, (2 * span, LANES), 1) < B_DIM)
           == (lax.broadcasted_iota(jnp.int32, (2 * span, LANES), 0) < span))
    qi = lax.broadcasted_iota(jnp.int32, (2 * span, 2 * span), 0) % span
    ki = lax.broadcasted_iota(jnp.int32, (2 * span, 2 * span), 1)
    band = (ki >= qi) & (ki <= qi + span)
    first = (lax.broadcasted_iota(jnp.int32, (2 * span, span), 1)
             <= lax.broadcasted_iota(jnp.int32, (2 * span, span), 0) % span)

    for g, r in enumerate(B_DILATIONS):
        sub = seq // r
        for rho in range(r):
            rows = pl.ds(rho, sub, stride=r) if r > 1 else pl.ds(0, sub)
            qs[rho * sub:(rho + 1) * sub, :] = qf[rows, :].astype(BF16)
            ks[rho * sub:(rho + 1) * sub, :] = kf[rows, :].astype(BF16)
            vs[rho * sub:(rho + 1) * sub, :] = vf[rows, :].astype(BF16)
        for rho in range(r):
            for n in range(sub // span):
                base = rho * sub + n * span
                qb = qs[base:base + span, :]
                lhs = jnp.where(own, jnp.concatenate([qb, qb], axis=0), jnp.zeros((), BF16))
                start = base - span if n else base
                s = jnp.where(band if n else first, _dot_nt(lhs, ks[start:base + span, :]), NEG_INF)
                m = jnp.max(s, axis=-1, keepdims=True)
                p = jnp.exp2(s - m)
                l = jnp.sum(p, axis=-1, keepdims=True)
                a = jnp.dot(p.astype(BF16), vs[start:base + span, :], preferred_element_type=F32)
                m = jnp.broadcast_to(m, (2 * span, LANES))
                l = jnp.broadcast_to(l, (2 * span, LANES))
                m_g = jnp.where(lo, m[:span], m[span:])
                l_g = jnp.where(lo, l[:span], l[span:])
                a_g = jnp.where(lo, a[:span], a[span:])
                nat = pl.ds(rho + r * span * n, span, stride=r) if r > 1 else pl.ds(n * span, span)
                if g == 0:
                    m_s[nat, :] = m_g
                    l_s[nat, :] = l_g
                    a_s[nat, :] = a_g
                else:
                    m_o = m_s[nat, :]
                    m_n = jnp.maximum(m_o, m_g)
                    w_o = jnp.exp2(m_o - m_n)
                    w_g = jnp.exp2(m_g - m_n)
                    m_s[nat, :] = m_n
                    l_s[nat, :] = w_o * l_s[nat, :] + w_g * l_g
                    a_s[nat, :] = w_o * a_s[nat, :] + w_g * a_g

    o_ref[0] = (a_s[...] / l_s[...]).astype(o_ref.dtype)


def _dilated_attention(h3):
    bsz, seq, _ = h3.shape
    blk = lambda off: pl.BlockSpec((1, seq, LANES), lambda b, p: (b, 0, off // LANES + p))
    return pl.pallas_call(
        functools.partial(_dil_kernel, seq=seq),
        out_shape=jax.ShapeDtypeStruct((bsz, seq, B_HEADS * B_DIM), BF16),
        grid=(bsz, B_HEADS // 2),
        in_specs=[blk(QB), blk(KB), blk(VB)],
        out_specs=pl.BlockSpec((1, seq, LANES), lambda b, p: (b, 0, p)),
        scratch_shapes=([pltpu.VMEM((seq, LANES), F32)] * 3 + [pltpu.VMEM((seq, LANES), BF16)] * 3
                        + [pltpu.VMEM((seq, LANES), F32)] * 3),
        compiler_params=_cparams(("parallel", "parallel")),
        name="dilated_attention",
    )(h3, h3, h3)


def _gla_kernel(q_ref, k_ref, v_ref, gate_ref, rc_ref, wa_ref, ba_ref, gn_ref, rcat_ref, o_ref, st, *, tb):
    t = pl.program_id(2)

    @pl.when(t == 0)
    def _():
        st[...] = jnp.zeros_like(st)

    ch, half, nc = C_CHUNK, C_CHUNK // 2, tb // C_CHUNK
    z = jnp.dot(rc_ref[0], wa_ref[0], preferred_element_type=F32) + ba_ref[0]
    la = (jnp.minimum(z, 0.0) - jnp.log(1.0 + jnp.exp(-jnp.abs(z)))) * (LOG2_E / C_TAU)
    ri = lax.broadcasted_iota(jnp.int32, (LANES, LANES), 0)
    ci = lax.broadcasted_iota(jnp.int32, (LANES, LANES), 1)
    same = (ri // ch) == (ci // ch)
    tri = jnp.where(same & (ci <= ri), 1.0, 0.0).astype(F32)
    blk = jnp.where(same, 1.0, 0.0).astype(F32)
    sums = lambda m: jnp.concatenate(
        [jnp.dot(m, la[r:r + LANES], preferred_element_type=F32, precision=lax.Precision.HIGHEST)
         for r in range(0, tb, LANES)], axis=0)
    b = sums(tri)
    bl = sums(blk)
    qs = q_ref[0].astype(F32) * (C_K ** -0.5)
    kf = k_ref[0].astype(F32)
    vb = v_ref[0]
    qt = (qs * jnp.exp2(b)).astype(BF16)
    kt = kf * jnp.exp2(bl - b)
    gl = jnp.exp2(bl)
    head0 = lax.broadcasted_iota(jnp.int32, (tb, LANES), 1) < C_KP
    kt_h = (jnp.where(head0, kt, 0.0).astype(BF16), jnp.where(head0, 0.0, kt).astype(BF16))

    def chunk_row(x, j, rows):
        xj = x.reshape(nc, ch, LANES)[:, j:j + 1, :]
        return jnp.broadcast_to(xj, (nc, rows, LANES)).reshape(nc * rows, LANES)

    def upper(x):
        return x.reshape(nc, 2, half, LANES)[:, 1].reshape(nc * half, LANES)

    rowi = lax.broadcasted_iota(jnp.int32, (tb, LANES), 0) % ch
    rowu = lax.broadcasted_iota(jnp.int32, (nc * half, LANES), 0) % half + half
    b_u, qs_u = upper(b), upper(qs)
    terms = []
    for j in range(ch):
        if j < half:
            e = jnp.exp2(jnp.where(rowi >= j, b - chunk_row(b, j, ch), NEG_INF))
            term = qs * e * chunk_row(kf, j, ch)
        else:
            e = jnp.exp2(jnp.where(rowu >= j, b_u - chunk_row(b, j, half), NEG_INF))
            tu = (qs_u * e * chunk_row(kf, j, half)).reshape(nc, 1, half, LANES)
            term = jnp.concatenate([jnp.zeros_like(tu), tu], axis=1).reshape(tb, LANES)
        terms.append(term.astype(BF16))
    s_tiled = jnp.dot(jnp.concatenate(terms, axis=1), rcat_ref[...], preferred_element_type=F32)
    rr = lax.broadcasted_iota(jnp.int32, (tb, 2 * LANES), 0)
    cc = lax.broadcasted_iota(jnp.int32, (tb, 2 * LANES), 1)
    m = jnp.where((rr % LANES) // ch == (cc % LANES) // ch, s_tiled, 0.0).astype(BF16)
    o_intra = jnp.concatenate(
        [jnp.concatenate([jnp.dot(m[r:r + LANES, h * LANES:(h + 1) * LANES], vb[r:r + LANES, h * C_VP:(h + 1) * C_VP],
                                  preferred_element_type=F32) for h in range(2)], axis=1)
         for r in range(0, tb, LANES)], axis=0)

    chunks = [slice(c * ch, (c + 1) * ch) for c in range(nc)]
    uts = [jnp.concatenate(
        [lax.dot_general(vb[rows, h * C_VP:(h + 1) * C_VP], kt_h[h][rows], (((0,), (0,)), ((), ())),
                         preferred_element_type=F32) for h in range(2)], axis=0) for rows in chunks]
    s_prev = st[...]
    states = []
    for c in range(nc):
        states.append(s_prev.astype(BF16))
        s_prev = gl[c * ch:c * ch + 1, :] * s_prev + uts[c]
    st[...] = s_prev
    o_inter = [_dot_nt(qt[rows], states[c]) for c, rows in enumerate(chunks)]

    o = o_intra + jnp.concatenate(o_inter, axis=0)
    gate = gate_ref[0].astype(F32)
    outs = []
    for hh in range(2):
        oh = o[:, hh * C_VP:(hh + 1) * C_VP]
        ms = jnp.sum(oh * oh, axis=-1, keepdims=True) * (1.0 / C_V)
        gh = gate[:, hh * C_VP:(hh + 1) * C_VP]
        outs.append(oh * lax.rsqrt(ms + LN_EPS) * gn_ref[...] * (gh * jax.nn.sigmoid(gh)))
    o_ref[0] = jnp.concatenate(outs, axis=1).astype(o_ref.dtype)


def _gla_reduce_matrix():
    r = np.arange(C_CHUNK * LANES)[:, None]
    c = np.arange(2 * LANES)[None, :]
    hit = ((r % LANES) // C_KP == c // LANES) & (r // LANES == (c % LANES) % C_CHUNK)
    return jnp.asarray(hit, dtype=BF16)


def _gla(h3, wa, ba, gn, tb=512):
    bsz, seq, _ = h3.shape
    return pl.pallas_call(
        functools.partial(_gla_kernel, tb=tb),
        out_shape=jax.ShapeDtypeStruct((bsz, seq, C_HEADS * C_VP), BF16),
        grid=(bsz, C_HEADS // 2, seq // tb),
        in_specs=[pl.BlockSpec((1, tb, LANES), lambda b, p, t: (b, t, QC // LANES + p)),
                  pl.BlockSpec((1, tb, LANES), lambda b, p, t: (b, t, KC // LANES + p)),
                  pl.BlockSpec((1, tb, 2 * C_VP), lambda b, p, t: (b, t, VC // (2 * C_VP) + p)),
                  pl.BlockSpec((1, tb, 2 * C_VP), lambda b, p, t: (b, t, GC // (2 * C_VP) + p)),
                  pl.BlockSpec((1, tb, LANES), lambda b, p, t: (b, t, RC // LANES)),
                  pl.BlockSpec((1, LANES, LANES), lambda b, p, t: (p, 0, 0)),
                  pl.BlockSpec((1, 1, LANES), lambda b, p, t: (p, 0, 0)),
                  pl.BlockSpec((1, C_VP), lambda b, p, t: (0, 0)),
                  pl.BlockSpec((C_CHUNK * LANES, 2 * LANES), lambda b, p, t: (0, 0))],
        out_specs=pl.BlockSpec((1, tb, 2 * C_VP), lambda b, p, t: (b, t, p)),
        scratch_shapes=[pltpu.VMEM((2 * C_VP, LANES), F32)],
        compiler_params=_cparams(("parallel", "parallel", "arbitrary")),
        name="gla",
    )(h3, h3, h3, h3, h3, wa, ba, gn, _gla_reduce_matrix())


def _outproj_kernel(oa_ref, ob_ref, oc_ref, wa_ref, wb_ref, wc_ref, x_ref, g_ref, b_ref, y_ref, yb_ref):
    acc = jnp.dot(oa_ref[...], wa_ref[...], preferred_element_type=F32)
    acc += jnp.dot(ob_ref[...], wb_ref[...], preferred_element_type=F32)
    acc += jnp.dot(oc_ref[...], wc_ref[...], preferred_element_type=F32)
    y = _layer_norm(ALPHA * x_ref[...] + acc, g_ref[...], b_ref[...])
    y_ref[...] = y
    yb_ref[...] = y.astype(BF16)


def _outproj_ln(oa, ob, oc, wa, wb, wc, x, g, b, tm=512):
    t = x.shape[0]
    row = lambda n: pl.BlockSpec((tm, n), lambda i: (i, 0))
    full = lambda a: pl.BlockSpec(a.shape, lambda i: (0, 0))
    return pl.pallas_call(
        _outproj_kernel,
        out_shape=(jax.ShapeDtypeStruct((t, D_MODEL), F32), jax.ShapeDtypeStruct((t, D_MODEL), BF16)),
        grid=(t // tm,),
        in_specs=[row(oa.shape[1]), row(ob.shape[1]), row(oc.shape[1]), full(wa), full(wb), full(wc),
                  row(D_MODEL), full(g), full(b)],
        out_specs=(row(D_MODEL), row(D_MODEL)),
        compiler_params=_cparams(("parallel",)),
        name="outproj_ln",
    )(oa, ob, oc, wa, wb, wc, x, g, b)


def _matmul_kernel(a_ref, w_ref, o_ref):
    o_ref[...] = jnp.dot(a_ref[...], w_ref[...], preferred_element_type=F32).astype(o_ref.dtype)


def _matmul(a, w, out_dtype, tm=256, tn=512):
    m, k = a.shape
    n = w.shape[1]
    return pl.pallas_call(
        _matmul_kernel,
        out_shape=jax.ShapeDtypeStruct((m, n), out_dtype),
        grid=(m // tm, n // tn),
        in_specs=[pl.BlockSpec((tm, k), lambda i, j: (i, 0)), pl.BlockSpec((k, tn), lambda i, j: (0, j))],
        out_specs=pl.BlockSpec((tm, tn), lambda i, j: (i, j)),
        compiler_params=_cparams(("parallel", "arbitrary")),
        name="matmul",
    )(a, w)


def _xattn_kernel(xb_ref, x_ref, wq_ref, k_ref, v_ref, wo_ref, g_ref, b_ref, y_ref, yb_ref):
    q = jnp.dot(xb_ref[0], wq_ref[...], preferred_element_type=F32).astype(BF16)
    outs = []
    for h in range(XA_HEADS):
        cols = slice(h * XA_DIM, (h + 1) * XA_DIM)
        s = _dot_nt(q[:, cols], k_ref[0, :, cols]) * (XA_DIM ** -0.5)
        m = jnp.max(s, axis=-1, keepdims=True)
        p = jnp.exp(s - m)
        l = jnp.sum(p, axis=-1, keepdims=True)
        o = jnp.dot(p.astype(BF16), v_ref[0, :, cols], preferred_element_type=F32)
        outs.append((o / l).astype(BF16))
    o = jnp.concatenate(outs, axis=1)
    xa = jnp.dot(o, wo_ref[...], preferred_element_type=F32)
    y = _layer_norm(ALPHA * x_ref[0] + xa, g_ref[...], b_ref[...])
    y_ref[0] = y
    yb_ref[0] = y.astype(BF16)


def _xattn_ln(xb3, x3, wq, kv3, wo, g, b, tm=512):
    bsz, seq, _ = x3.shape
    mem_len = kv3.shape[1]
    row = pl.BlockSpec((1, tm, D_MODEL), lambda bi, i: (bi, i, 0))
    full = lambda a: pl.BlockSpec(a.shape, lambda bi, i: (0, 0))
    return pl.pallas_call(
        _xattn_kernel,
        out_shape=(jax.ShapeDtypeStruct((bsz, seq, D_MODEL), F32), jax.ShapeDtypeStruct((bsz, seq, D_MODEL), BF16)),
        grid=(bsz, seq // tm),
        in_specs=[row, row, full(wq),
                  pl.BlockSpec((1, mem_len, D_MODEL), lambda bi, i: (bi, 0, 0)),
                  pl.BlockSpec((1, mem_len, D_MODEL), lambda bi, i: (bi, 0, 1)),
                  full(wo), full(g), full(b)],
        out_specs=(row, row),
        compiler_params=_cparams(("parallel", "arbitrary")),
        name="xattn_ln",
    )(xb3, x3, wq, kv3, kv3, wo, g, b)


def _swiglu_chunk(xb, w1, w3, w2):
    hg = jnp.dot(xb, w1, preferred_element_type=F32)
    hu = jnp.dot(xb, w3, preferred_element_type=F32)
    a = hg * jax.nn.sigmoid(hg) * hu
    return jnp.dot(a.astype(BF16), w2, preferred_element_type=F32)


def _ffn_kernel(xb_ref, x_ref, w1_ref, w3_ref, w2_ref, g_ref, b_ref, y_ref, yb_ref, acc_ref):
    j = pl.program_id(1)

    @pl.when(j == 0)
    def _():
        acc_ref[...] = jnp.zeros_like(acc_ref)

    acc_ref[...] += _swiglu_chunk(xb_ref[...], w1_ref[...], w3_ref[...], w2_ref[...])

    @pl.when(j == pl.num_programs(1) - 1)
    def _():
        y = _layer_norm(ALPHA * x_ref[...] + acc_ref[...], g_ref[...], b_ref[...])
        y_ref[...] = y
        yb_ref[...] = y.astype(BF16)


def _ffn_ln(xb, x, w13, w2, g, b, tm=1024, tf=256):
    t = x.shape[0]
    nf = w2.shape[0] // tf
    row = pl.BlockSpec((tm, D_MODEL), lambda i, j: (i, 0))
    vec = lambda a: pl.BlockSpec(a.shape, lambda i, j: (0, 0))
    return pl.pallas_call(
        _ffn_kernel,
        out_shape=(jax.ShapeDtypeStruct((t, D_MODEL), F32), jax.ShapeDtypeStruct((t, D_MODEL), BF16)),
        grid=(t // tm, nf),
        in_specs=[row, row,
                  pl.BlockSpec((D_MODEL, tf), lambda i, j: (0, j)),
                  pl.BlockSpec((D_MODEL, tf), lambda i, j: (0, j + nf)),
                  pl.BlockSpec((tf, D_MODEL), lambda i, j: (j, 0)),
                  vec(g), vec(b)],
        out_specs=(row, row),
        scratch_shapes=[pltpu.VMEM((tm, D_MODEL), F32)],
        compiler_params=_cparams(("parallel", "arbitrary")),
        name="ffn_ln",
    )(xb, x, w13, w13, w2, g, b)


def _router_kernel(x_ref, w_ref, route_ref):
    logits = jnp.dot(x_ref[...], w_ref[...], preferred_element_type=F32, precision=lax.Precision.HIGHEST)
    lane = lax.broadcasted_iota(jnp.int32, logits.shape, 1)
    logits = jnp.where(lane < N_EXPERTS, logits, -jnp.inf)
    v1 = jnp.max(logits, axis=-1, keepdims=True)
    i1 = jnp.min(jnp.where(logits == v1, lane, LANES), axis=-1, keepdims=True)
    rest = jnp.where(lane == i1, -jnp.inf, logits)
    v2 = jnp.max(rest, axis=-1, keepdims=True)
    i2 = jnp.min(jnp.where(rest == v2, lane, LANES), axis=-1, keepdims=True)
    e2 = jnp.exp(v2 - v1)
    w1 = 1.0 / (1.0 + e2)
    w2 = e2 / (1.0 + e2)
    route_ref[...] = (jnp.where(lane == 0, w1, 0.0) + jnp.where(lane == 1, w2, 0.0)
                      + jnp.where(lane == 2, i1.astype(F32), 0.0) + jnp.where(lane == 3, i2.astype(F32), 0.0))


def _router(x, w, tm=1024):
    t = x.shape[0]
    return pl.pallas_call(
        _router_kernel,
        out_shape=jax.ShapeDtypeStruct((t, LANES), F32),
        grid=(t // tm,),
        in_specs=[pl.BlockSpec((tm, D_MODEL), lambda i: (i, 0)), pl.BlockSpec(w.shape, lambda i: (0, 0))],
        out_specs=pl.BlockSpec((tm, LANES), lambda i: (i, 0)),
        compiler_params=_cparams(("parallel",)),
        name="router",
    )(x, w)


def _dispatch_plan(route, tm):
    t = route.shape[0]
    expert = route[:, 2:4].astype(jnp.int32).reshape(-1)
    onehot = (expert[:, None] == jnp.arange(N_EXPERTS, dtype=jnp.int32)[None, :]).astype(jnp.int32)
    csum = jnp.cumsum(onehot, axis=0)
    rank = jnp.sum((csum - 1) * onehot, axis=1)
    counts = csum[-1]
    padded = (counts + tm - 1) // tm * tm
    ends = jnp.cumsum(padded)
    dest = (ends - padded)[expert] + rank
    n_tiles = TOP_K * t // tm + N_EXPERTS
    src = jnp.zeros((n_tiles * tm,), jnp.int32).at[dest].set(jnp.arange(TOP_K * t, dtype=jnp.int32) // TOP_K)
    tile_start = jnp.arange(n_tiles, dtype=jnp.int32) * tm
    tile_expert = jnp.sum((tile_start[:, None] >= ends[None, :]).astype(jnp.int32), axis=1)
    tile_expert = jnp.minimum(tile_expert, N_EXPERTS - 1)
    tile_used = (tile_start < ends[-1]).astype(jnp.int32)
    return src, dest.reshape(t, TOP_K).astype(jnp.int32), tile_expert, tile_used


ROW_SLABS = D_MODEL // LANES


def _slab(ref, s, rows):
    return ref.at[pl.ds(s, rows, stride=ROW_SLABS), :]


def _row_tile(ref, r):
    return ref.at[pl.ds(r * ROW_SLABS, ROW_SLABS), :]


DMA_THREADS = 2


def _gather_kernel(used_ref, src_ref, x_hbm, o_ref, buf, sem, *, tm):
    used = used_ref[pl.program_id(0)] > 0

    @pl.when(used)
    def _():
        def issue(r, carry):
            for par in range(DMA_THREADS):
                row = DMA_THREADS * r + par
                pltpu.make_async_copy(_row_tile(x_hbm, src_ref[0, 0, row]), _row_tile(buf, row), sem).start(priority=par)
            return carry

        lax.fori_loop(0, tm // DMA_THREADS, issue, 0, unroll=4)
        pltpu.make_async_copy(x_hbm.at[pl.ds(0, tm * ROW_SLABS), :], buf, sem).wait()
        for s in range(ROW_SLABS):
            o_ref[:, s * LANES:(s + 1) * LANES] = _slab(buf, s, tm)[...].astype(o_ref.dtype)

    @pl.when(jnp.logical_not(used))
    def _():
        o_ref[...] = jnp.zeros_like(o_ref)


def _gather_rows(tile_used, src, x3, tm):
    n_tiles = src.shape[0]
    return pl.pallas_call(
        functools.partial(_gather_kernel, tm=tm),
        out_shape=jax.ShapeDtypeStruct((n_tiles * tm, D_MODEL), BF16),
        grid_spec=pltpu.PrefetchScalarGridSpec(
            num_scalar_prefetch=1, grid=(n_tiles,),
            in_specs=[pl.BlockSpec((1, 1, tm), lambda i, used: (i, 0, 0), memory_space=pltpu.SMEM),
                      pl.BlockSpec(memory_space=pl.ANY)],
            out_specs=pl.BlockSpec((tm, D_MODEL), lambda i, used: (i, 0)),
            scratch_shapes=[pltpu.VMEM((tm * ROW_SLABS, LANES), F32), pltpu.SemaphoreType.DMA]),
        compiler_params=_cparams(("arbitrary",)),
        name="moe_gather",
    )(tile_used, src, x3)


def _moe_ffn_kernel(te_ref, tu_ref, xs_ref, w1_ref, w3_ref, w2_ref, y_ref, acc_ref):
    i = pl.program_id(0)
    j = pl.program_id(1)
    last = pl.num_programs(1) - 1

    @pl.when(j == 0)
    def _():
        acc_ref[...] = jnp.zeros_like(acc_ref)

    @pl.when(tu_ref[i] > 0)
    def _():
        acc_ref[...] += _swiglu_chunk(xs_ref[...], w1_ref[0], w3_ref[0], w2_ref[0])

    @pl.when(j == last)
    def _():
        for s in range(ROW_SLABS):
            _slab(y_ref, s, acc_ref.shape[0])[...] = acc_ref[:, s * LANES:(s + 1) * LANES]


def _moe_ffn(tile_expert, tile_used, xs, w13, w2, tm, tf):
    n = xs.shape[0]
    nf = w2.shape[1] // tf
    chunk = lambda i, j, tu: jnp.where(tu[i] > 0, j, nf - 1)
    return pl.pallas_call(
        _moe_ffn_kernel,
        out_shape=jax.ShapeDtypeStruct((n * ROW_SLABS, LANES), F32),
        grid_spec=pltpu.PrefetchScalarGridSpec(
            num_scalar_prefetch=2, grid=(n // tm, nf),
            in_specs=[pl.BlockSpec((tm, D_MODEL), lambda i, j, te, tu: (i, 0)),
                      pl.BlockSpec((1, D_MODEL, tf), lambda i, j, te, tu: (te[i], 0, chunk(i, j, tu))),
                      pl.BlockSpec((1, D_MODEL, tf), lambda i, j, te, tu: (te[i], 0, chunk(i, j, tu) + nf)),
                      pl.BlockSpec((1, tf, D_MODEL), lambda i, j, te, tu: (te[i], chunk(i, j, tu), 0))],
            out_specs=pl.BlockSpec((tm * ROW_SLABS, LANES), lambda i, j, te, tu: (i, 0)),
            scratch_shapes=[pltpu.VMEM((tm, D_MODEL), F32)]),
        compiler_params=_cparams(("arbitrary", "arbitrary")),
        name="moe_ffn",
    )(tile_expert, tile_used, xs, w13, w13, w2)


def _combine_kernel(d0_ref, d1_ref, y_hbm, route_ref, x_ref, g_ref, b_ref, o_ref, ob_ref, buf0, buf1, sem, *, tm):
    def issue(r, carry):
        pltpu.make_async_copy(_row_tile(y_hbm, d0_ref[0, 0, r]), _row_tile(buf0, r), sem).start(priority=0)
        pltpu.make_async_copy(_row_tile(y_hbm, d1_ref[0, 0, r]), _row_tile(buf1, r), sem).start(priority=1)
        return carry

    lax.fori_loop(0, tm, issue, 0, unroll=8)
    pltpu.make_async_copy(y_hbm.at[pl.ds(0, tm * ROW_SLABS), :], buf0, sem).wait()
    pltpu.make_async_copy(y_hbm.at[pl.ds(0, tm * ROW_SLABS), :], buf1, sem).wait()
    route = route_ref[...]
    w0, w1 = route[:, 0:1], route[:, 1:2]
    f = jnp.concatenate([w0 * _slab(buf0, s, tm)[...] + w1 * _slab(buf1, s, tm)[...] for s in range(ROW_SLABS)],
                        axis=1)
    y = _layer_norm(ALPHA * x_ref[...] + f, g_ref[...], b_ref[...])
    o_ref[...] = y
    ob_ref[...] = y.astype(BF16)


def _combine_ln(dest, ys, route, x, g, b, tm=512):
    t = x.shape[0]
    row = pl.BlockSpec((tm, D_MODEL), lambda i: (i, 0))
    vec = lambda a: pl.BlockSpec(a.shape, lambda i: (0, 0))
    idx = pl.BlockSpec((1, 1, tm), lambda i: (i, 0, 0), memory_space=pltpu.SMEM)
    return pl.pallas_call(
        functools.partial(_combine_kernel, tm=tm),
        out_shape=(jax.ShapeDtypeStruct((t, D_MODEL), F32), jax.ShapeDtypeStruct((t, D_MODEL), BF16)),
        grid=(t // tm,),
        in_specs=[idx, idx, pl.BlockSpec(memory_space=pl.ANY),
                  pl.BlockSpec((tm, LANES), lambda i: (i, 0)),
                  row, vec(g), vec(b)],
        out_specs=(row, row),
        scratch_shapes=[pltpu.VMEM((tm * ROW_SLABS, LANES), F32), pltpu.VMEM((tm * ROW_SLABS, LANES), F32),
                        pltpu.SemaphoreType.DMA],
        compiler_params=_cparams(("arbitrary",)),
        name="moe_combine_ln",
    )(dest[:, 0].reshape(-1, 1, tm), dest[:, 1].reshape(-1, 1, tm), ys, route, x, g, b)


def _moe_ln(x, router_w, w13, w2, g, b, tm=1024, tf=512):
    route = _router(x, router_w)
    src, dest, tile_expert, tile_used = _dispatch_plan(route, tm)
    xs = _gather_rows(tile_used, src.reshape(-1, 1, tm), x.reshape(-1, LANES), tm)
    ys = _moe_ffn(tile_expert, tile_used, xs, w13, w2, tm, tf)
    return _combine_ln(dest, ys, route, x, g, b)


def _pad_heads(w, heads, width, padded, axis):
    shape = list(w.shape)
    shape[axis:axis + 1] = [heads, width]
    w = w.reshape(shape)
    pad = [(0, 0)] * w.ndim
    pad[axis + 1] = (0, padded - width)
    w = jnp.pad(w, pad)
    shape[axis:axis + 2] = [heads * padded]
    return w.reshape(shape)


def _layout_w_in(w):
    bounds = np.cumsum([0, 256, 256, 256, 384, 384, 384, 192, 192, 384, 384, 16])
    qa, ka, va, qb, kb, vb, qc, kc, vc, gc, rc = [w[:, bounds[i]:bounds[i + 1]] for i in range(11)]
    rc = jnp.pad(rc, ((0, 0), (0, LANES - C_RANK)))
    qc = _pad_heads(qc, C_HEADS, C_K, C_KP, 1)
    kc = _pad_heads(kc, C_HEADS, C_K, C_KP, 1)
    vc = _pad_heads(vc, C_HEADS, C_V, C_VP, 1)
    gc = _pad_heads(gc, C_HEADS, C_V, C_VP, 1)
    return jnp.concatenate([qa, ka, qb, kb, va, vb, rc, qc, kc, vc, gc], axis=1).astype(BF16)


def _rope_tables(positions, dim):
    rot = dim // ROPE_FRAC
    inv_freq = ROPE_THETA ** (-jnp.arange(0, rot, 2, dtype=F32) / rot)
    ang = positions.astype(F32).reshape(-1, 1) * inv_freq
    cos, sin = jnp.cos(ang), jnp.sin(ang)
    t = ang.shape[0]
    c = jnp.concatenate([cos, cos, jnp.ones((t, dim - rot), F32)], axis=1)
    s = jnp.concatenate([-sin, sin, jnp.zeros((t, dim - rot), F32)], axis=1)
    reps = LANES // dim
    return jnp.tile(c, (1, reps)), jnp.tile(s, (1, reps))


def kernel(x, mem, positions, w_in, lam_q1, lam_k1, lam_q2, lam_k2, diff_norm_g, gla_wa2, gla_ba, gla_norm_g,
           w_out, ln_mix_g, ln_mix_b, xa_wq, xa_wkv, xa_wo, ln_xa_g, ln_xa_b, ffd_w13, ffd_w2, moe_router,
           moe_w13, moe_w2, ln_ffn_g, ln_ffn_b):
    bsz, seq, _ = x.shape
    t = bsz * seq
    ca, sa = _rope_tables(positions, A_QK)
    cb, sb = _rope_tables(positions, B_DIM)
    tabs = (ca, sa, cb, sb)
    memb = mem.reshape(-1, D_MODEL).astype(BF16)
    mem_len = mem.shape[1]
    row = lambda v: v.reshape(1, -1).astype(F32)

    xf = x.reshape(t, D_MODEL)
    xb = xf.astype(BF16)
    for l in range(DEPTH):
        lam_init = 0.8 - 0.6 * math.exp(-0.3 * l)
        lam = (jnp.exp(jnp.sum(lam_q1[l] * lam_k1[l])) - jnp.exp(jnp.sum(lam_q2[l] * lam_k2[l]))
               + lam_init).reshape(1, 1).astype(F32)
        h = _inproj(xb, _layout_w_in(w_in[l]), tabs)
        h3 = h.reshape(bsz, seq, H_COLS)
        oa = _diff_attention(h3, lam, jnp.tile(row(diff_norm_g[l]), (1, 2)), lam_init)
        ob = _dilated_attention(h3)
        wa = _pad_heads(gla_wa2[l], C_HEADS, C_K, C_KP, 1)
        wa = jnp.pad(wa, ((0, LANES - C_RANK), (0, 0))).reshape(LANES, 2, LANES).transpose(1, 0, 2).astype(BF16)
        ba = _pad_heads(gla_ba[l], C_HEADS, C_K, C_KP, 0).reshape(2, 1, LANES).astype(F32)
        gn = jnp.pad(row(gla_norm_g[l]), ((0, 0), (0, C_VP - C_V)))
        oc = _gla(h3, wa, ba, gn)
        wo = w_out[l]
        wo_a = wo[:A_HEADS * A_V].astype(BF16)
        wo_b = wo[A_HEADS * A_V:A_HEADS * A_V + B_HEADS * B_DIM].astype(BF16)
        wo_c = _pad_heads(wo[A_HEADS * A_V + B_HEADS * B_DIM:], C_HEADS, C_V, C_VP, 0).astype(BF16)
        xf, xb = _outproj_ln(oa.reshape(t, -1), ob.reshape(t, -1), oc.reshape(t, -1), wo_a, wo_b, wo_c,
                             xf, row(ln_mix_g[l]), row(ln_mix_b[l]))

        kv = _matmul(memb, xa_wkv[l].astype(BF16), BF16)
        x3, xb3 = _xattn_ln(xb.reshape(bsz, seq, D_MODEL), xf.reshape(bsz, seq, D_MODEL), xa_wq[l].astype(BF16),
                            kv.reshape(bsz, mem_len, 2 * D_MODEL), xa_wo[l].astype(BF16),
                            row(ln_xa_g[l]), row(ln_xa_b[l]))
        xf, xb = x3.reshape(t, D_MODEL), xb3.reshape(t, D_MODEL)

        if l % 2 == 0:
            xf, xb = _ffn_ln(xb, xf, ffd_w13[l // 2].astype(BF16), ffd_w2[l // 2].astype(BF16),
                             row(ln_ffn_g[l]), row(ln_ffn_b[l]))
        else:
            rw = jnp.pad(moe_router[l // 2], ((0, 0), (0, LANES - N_EXPERTS)))
            xf, xb = _moe_ln(xf, rw, moe_w13[l // 2].astype(BF16), moe_w2[l // 2].astype(BF16),
                             row(ln_ffn_g[l]), row(ln_ffn_b[l]))
    return xf.reshape(bsz, seq, D_MODEL)
```

```python
import functools
import math

import numpy as np
import jax
import jax.numpy as jnp
from jax import lax
from jax.experimental import pallas as pl
from jax.experimental.pallas import tpu as pltpu

F32 = jnp.float32
BF16 = jnp.bfloat16

LANES = 128
VMEM_LIMIT_BYTES = 56 * 1024 * 1024

D_MODEL = 1024
DEPTH = 2
ROPE_THETA = 500000.0
ROPE_FRAC = 4
LN_EPS = 1e-5
NEG_INF = -1e30
ALPHA = (2 * DEPTH) ** 0.25
LOG2_E = math.log2(math.e)

A_HEADS, A_QK, A_V = 4, 32, 64
B_HEADS, B_DIM = 6, 64
B_SPAN = 128
B_DILATIONS = (1, 4, 16)
C_HEADS, C_K, C_V = 4, 48, 96
C_KP, C_VP = 64, 128
C_RANK, C_TAU, C_CHUNK = 16, 16.0, 16
XA_HEADS, XA_DIM = 4, 256
N_EXPERTS, TOP_K = 8, 2

QA, KA, QB, KB, VA, VB, RC, QC, KC, VC, GC, H_COLS = (
    0, 256, 512, 896, 1280, 1536, 1920, 2048, 2304, 2560, 3072, 3584)
IN_TN = 256
ROPE_A_TILES = (QB - QA) // IN_TN
ROPE_B_TILES = (VA - QA) // IN_TN


def _cparams(sem):
    return pltpu.CompilerParams(dimension_semantics=sem, vmem_limit_bytes=VMEM_LIMIT_BYTES)


def _layer_norm(y, g, b):
    mu = jnp.mean(y, axis=-1, keepdims=True)
    yc = y - mu
    var = jnp.mean(yc * yc, axis=-1, keepdims=True)
    return yc * lax.rsqrt(var + LN_EPS) * g + b


def _dot_nt(a, b):
    return lax.dot_general(a, b, (((1,), (1,)), ((), ())), preferred_element_type=F32)


def _inproj_kernel(x_ref, w_ref, ca_ref, sa_ref, cb_ref, sb_ref, o_ref):
    x = x_ref[...]
    lane = lax.broadcasted_iota(jnp.int32, (x.shape[0], LANES), 1)

    def rope_cols(lo, k_lo, hi, c_ref, s_ref, half, period):
        acc = jnp.dot(x, w_ref[:, lo:hi], preferred_element_type=F32)
        c, s = c_ref[...], s_ref[...]
        first = lane % period < half
        for k in range((hi - lo) // LANES):
            a = acc[:, k * LANES:(k + 1) * LANES]
            up = pltpu.roll(a, LANES - half, 1)
            dn = pltpu.roll(a, half, 1)
            r = a * c + jnp.where(first, up, dn) * s
            if lo + k * LANES < k_lo:
                r = r * (period ** -0.5 * LOG2_E)
            o_ref[:, lo + k * LANES:lo + (k + 1) * LANES] = r.astype(o_ref.dtype)

    rope_cols(QA, KA, QB, ca_ref, sa_ref, A_QK // ROPE_FRAC // 2, A_QK)
    rope_cols(QB, KB, VA, cb_ref, sb_ref, B_DIM // ROPE_FRAC // 2, B_DIM)
    o_ref[:, VA:] = jnp.dot(x, w_ref[:, VA:], preferred_element_type=F32).astype(o_ref.dtype)


def _inproj(xb, w, tabs, tm=512):
    t = xb.shape[0]
    tab_spec = pl.BlockSpec((tm, LANES), lambda i: (i, 0))
    return pl.pallas_call(
        _inproj_kernel,
        out_shape=jax.ShapeDtypeStruct((t, H_COLS), BF16),
        grid=(t // tm,),
        in_specs=[pl.BlockSpec((tm, D_MODEL), lambda i: (i, 0)),
                  pl.BlockSpec((D_MODEL, H_COLS), lambda i: (0, 0)),
                  tab_spec, tab_spec, tab_spec, tab_spec],
        out_specs=pl.BlockSpec((tm, H_COLS), lambda i: (i, 0)),
        compiler_params=_cparams(("parallel",)),
        name="inproj_rope",
    )(xb, w, *tabs)


def _diff_kernel(lam_ref, q_ref, k_ref, v_ref, g_ref, o_ref, *, seq, tq, out_scale):
    lane = lax.broadcasted_iota(jnp.int32, (tq, LANES), 1)
    row = lax.broadcasted_iota(jnp.int32, (tq, tq), 0)
    col = lax.broadcasted_iota(jnp.int32, (tq, tq), 1)
    causal = col <= row
    lo = lane < A_V
    lam = lam_ref[0, 0]

    for i in range(seq // tq):
        past = i * tq
        q = q_ref[0, past:past + tq, :]
        k_d = k_ref[0, past:past + tq, :]
        v_d = v_ref[0, past:past + tq, :]

        def attend(c):
            qm = jnp.where(lane // A_QK == c, q, jnp.zeros_like(q))
            s_d = jnp.where(causal, _dot_nt(qm, k_d), NEG_INF)
            m = jnp.max(s_d, axis=-1, keepdims=True)
            if past:
                s_p = _dot_nt(qm, k_ref[0, :past, :])
                m = jnp.maximum(m, jnp.max(s_p, axis=-1, keepdims=True))
            p_d = jnp.exp2(s_d - m)
            l = jnp.sum(p_d, axis=-1, keepdims=True)
            acc = jnp.dot(p_d.astype(BF16), v_d, preferred_element_type=F32)
            if past:
                p_p = jnp.exp2(s_p - m)
                l = l + jnp.sum(p_p, axis=-1, keepdims=True)
                acc = acc + jnp.dot(p_p.astype(BF16), v_ref[0, :past, :], preferred_element_type=F32)
            return acc / l

        o_h0 = attend(0) - lam * attend(1)
        o_h1 = attend(2) - lam * attend(3)
        o = jnp.where(lo, o_h0, o_h1)
        sq = o * o
        s_lo = jnp.sum(jnp.where(lo, sq, 0.0), axis=-1, keepdims=True)
        s_all = jnp.sum(sq, axis=-1, keepdims=True)
        ms = jnp.where(lo, s_lo, s_all - s_lo) * (1.0 / A_V)
        o_ref[0, past:past + tq, :] = (o * lax.rsqrt(ms + LN_EPS) * g_ref[...] * out_scale).astype(o_ref.dtype)


def _diff_attention(h3, lam, g, lam_init, tq=512):
    bsz, seq, _ = h3.shape
    kern = functools.partial(_diff_kernel, seq=seq, tq=tq, out_scale=1.0 - lam_init)
    blk = lambda off: pl.BlockSpec((1, seq, LANES), lambda b, p: (b, 0, off // LANES + p))
    return pl.pallas_call(
        kern,
        out_shape=jax.ShapeDtypeStruct((bsz, seq, A_HEADS * A_V), BF16),
        grid=(bsz, A_HEADS // 2),
        in_specs=[pl.BlockSpec(memory_space=pltpu.SMEM), blk(QA), blk(KA), blk(VA),
                  pl.BlockSpec((1, LANES), lambda b, p: (0, 0))],
        out_specs=pl.BlockSpec((1, seq, LANES), lambda b, p: (b, 0, p)),
        compiler_params=_cparams(("parallel", "parallel")),
        name="diff_attention",
    )(lam, h3, h3, h3, g)


def _dil_kernel(q_ref, k_ref, v_ref, o_ref, qf, kf, vf, qs, ks, vs, m_s, l_s, a_s, *, seq):
    qf[...] = q_ref[0].astype(F32)
    kf[...] = k_ref[0].astype(F32)
    vf[...] = v_ref[0].astype(F32)
    span = B_SPAN
    lo = lax.broadcasted_iota(jnp.int32, (span, LANES), 1) < B_DIM
    own = ((lax.broadcasted_iota(jnp.int32, (2 * span, LANES), 1) < B_DIM)
           == (lax.broadcasted_iota(jnp.int32, (2 * span, LANES), 0) < span))
    qi = lax.broadcasted_iota(jnp.int32, (2 * span, 2 * span), 0) % span
    ki = lax.broadcasted_iota(jnp.int32, (2 * span, 2 * span), 1)
    band = (ki >= qi) & (ki <= qi + span)
    first = (lax.broadcasted_iota(jnp.int32, (2 * span, span), 1)
             <= lax.broadcasted_iota(jnp.int32, (2 * span, span), 0) % span)

    for g, r in enumerate(B_DILATIONS):
        sub = seq // r
        for rho in range(r):
            rows = pl.ds(rho, sub, stride=r) if r > 1 else pl.ds(0, sub)
            qs[rho * sub:(rho + 1) * sub, :] = qf[rows, :].astype(BF16)
            ks[rho * sub:(rho + 1) * sub, :] = kf[rows, :].astype(BF16)
            vs[rho * sub:(rho + 1) * sub, :] = vf[rows, :].astype(BF16)
        for rho in range(r):
            for n in range(sub // span):
                base = rho * sub + n * span
                qb = qs[base:base + span, :]
                lhs = jnp.where(own, jnp.concatenate([qb, qb], axis=0), jnp.zeros((), BF16))
                start = base - span if n else base
                s = jnp.where(band if n else first, _dot_nt(lhs, ks[start:base + span, :]), NEG_INF)
                m = jnp.max(s, axis=-1, keepdims=True)
                p = jnp.exp2(s - m)
                l = jnp.sum(p, axis=-1, keepdims=True)
                a = jnp.dot(p.astype(BF16), vs[start:base + span, :], preferred_element_type=F32)
                m = jnp.broadcast_to(m, (2 * span, LANES))
                l = jnp.broadcast_to(l, (2 * span, LANES))
                m_g = jnp.where(lo, m[:span], m[span:])
                l_g = jnp.where(lo, l[:span], l[span:])
                a_g = jnp.where(lo, a[:span], a[span:])
                nat = pl.ds(rho + r * span * n, span, stride=r) if r > 1 else pl.ds(n * span, span)
                if g == 0:
                    m_s[nat, :] = m_g
                    l_s[nat, :] = l_g
                    a_s[nat, :] = a_g
                else:
                    m_o = m_s[nat, :]
                    m_n = jnp.maximum(m_o, m_g)
                    w_o = jnp.exp2(m_o - m_n)
                    w_g = jnp.exp2(m_g - m_n)
                    m_s[nat, :] = m_n
                    l_s[nat, :] = w_o * l_s[nat, :] + w_g * l_g
                    a_s[nat, :] = w_o * a_s[nat, :] + w_g * a_g

    o_ref[0] = (a_s[...] / l_s[...]).astype(o_ref.dtype)


def _dilated_attention(h3):
    bsz, seq, _ = h3.shape
    blk = lambda off: pl.BlockSpec((1, seq, LANES), lambda b, p: (b, 0, off // LANES + p))
    return pl.pallas_call(
        functools.partial(_dil_kernel, seq=seq),
        out_shape=jax.ShapeDtypeStruct((bsz, seq, B_HEADS * B_DIM), BF16),
        grid=(bsz, B_HEADS // 2),
        in_specs=[blk(QB), blk(KB), blk(VB)],
        out_specs=pl.BlockSpec((1, seq, LANES), lambda b, p: (b, 0, p)),
        scratch_shapes=([pltpu.VMEM((seq, LANES), F32)] * 3 + [pltpu.VMEM((seq, LANES), BF16)] * 3
                        + [pltpu.VMEM((seq, LANES), F32)] * 3),
        compiler_params=_cparams(("parallel", "parallel")),
        name="dilated_attention",
    )(h3, h3, h3)


def _gla_kernel(q_ref, k_ref, v_ref, gate_ref, rc_ref, wa_ref, ba_ref, gn_ref, rcat_ref, o_ref, st, *, tb):
    t = pl.program_id(2)

    @pl.when(t == 0)
    def _():
        st[...] = jnp.zeros_like(st)

    ch, half, nc = C_CHUNK, C_CHUNK // 2, tb // C_CHUNK
    z = jnp.dot(rc_ref[0], wa_ref[0], preferred_element_type=F32) + ba_ref[0]
    la = (jnp.minimum(z, 0.0) - jnp.log(1.0 + jnp.exp(-jnp.abs(z)))) * (LOG2_E / C_TAU)
    ri = lax.broadcasted_iota(jnp.int32, (LANES, LANES), 0)
    ci = lax.broadcasted_iota(jnp.int32, (LANES, LANES), 1)
    same = (ri // ch) == (ci // ch)
    tri = jnp.where(same & (ci <= ri), 1.0, 0.0).astype(F32)
    blk = jnp.where(same, 1.0, 0.0).astype(F32)
    sums = lambda m: jnp.concatenate(
        [jnp.dot(m, la[r:r + LANES], preferred_element_type=F32, precision=lax.Precision.HIGHEST)
         for r in range(0, tb, LANES)], axis=0)
    b = sums(tri)
    bl = sums(blk)
    qs = q_ref[0].astype(F32) * (C_K ** -0.5)
    kf = k_ref[0].astype(F32)
    vb = v_ref[0]
    qt = (qs * jnp.exp2(b)).astype(BF16)
    kt = kf * jnp.exp2(bl - b)
    gl = jnp.exp2(bl)
    head0 = lax.broadcasted_iota(jnp.int32, (tb, LANES), 1) < C_KP
    kt_h = (jnp.where(head0, kt, 0.0).astype(BF16), jnp.where(head0, 0.0, kt).astype(BF16))

    def chunk_row(x, j, rows):
        xj = x.reshape(nc, ch, LANES)[:, j:j + 1, :]
        return jnp.broadcast_to(xj, (nc, rows, LANES)).reshape(nc * rows, LANES)

    def upper(x):
        return x.reshape(nc, 2, half, LANES)[:, 1].reshape(nc * half, LANES)

    rowi = lax.broadcasted_iota(jnp.int32, (tb, LANES), 0) % ch
    rowu = lax.broadcasted_iota(jnp.int32, (nc * half, LANES), 0) % half + half
    b_u, qs_u = upper(b), upper(qs)
    terms = []
    for j in range(ch):
        if j < half:
            e = jnp.exp2(jnp.where(rowi >= j, b - chunk_row(b, j, ch), NEG_INF))
            term = qs * e * chunk_row(kf, j, ch)
        else:
            e = jnp.exp2(jnp.where(rowu >= j, b_u - chunk_row(b, j, half), NEG_INF))
            tu = (qs_u * e * chunk_row(kf, j, half)).reshape(nc, 1, half, LANES)
            term = jnp.concatenate([jnp.zeros_like(tu), tu], axis=1).reshape(tb, LANES)
        terms.append(term.astype(BF16))
    s_tiled = jnp.dot(jnp.concatenate(terms, axis=1), rcat_ref[...], preferred_element_type=F32)
    rr = lax.broadcasted_iota(jnp.int32, (tb, 2 * LANES), 0)
    cc = lax.broadcasted_iota(jnp.int32, (tb, 2 * LANES), 1)
    m = jnp.where((rr % LANES) // ch == (cc % LANES) // ch, s_tiled, 0.0).astype(BF16)
    o_intra = jnp.concatenate(
        [jnp.concatenate([jnp.dot(m[r:r + LANES, h * LANES:(h + 1) * LANES], vb[r:r + LANES, h * C_VP:(h + 1) * C_VP],
                                  preferred_element_type=F32) for h in range(2)], axis=1)
         for r in range(0, tb, LANES)], axis=0)

    chunks = [slice(c * ch, (c + 1) * ch) for c in range(nc)]
    uts = [jnp.concatenate(
        [lax.dot_general(vb[rows, h * C_VP:(h + 1) * C_VP], kt_h[h][rows], (((0,), (0,)), ((), ())),
                         preferred_element_type=F32) for h in range(2)], axis=0) for rows in chunks]
    s_prev = st[...]
    states = []
    for c in range(nc):
        states.append(s_prev.astype(BF16))
        s_prev = gl[c * ch:c * ch + 1, :] * s_prev + uts[c]
    st[...] = s_prev
    o_inter = [_dot_nt(qt[rows], states[c]) for c, rows in enumerate(chunks)]

    o = o_intra + jnp.concatenate(o_inter, axis=0)
    gate = gate_ref[0].astype(F32)
    outs = []
    for hh in range(2):
        oh = o[:, hh * C_VP:(hh + 1) * C_VP]
        ms = jnp.sum(oh * oh, axis=-1, keepdims=True) * (1.0 / C_V)
        gh = gate[:, hh * C_VP:(hh + 1) * C_VP]
        outs.append(oh * lax.rsqrt(ms + LN_EPS) * gn_ref[...] * (gh * jax.nn.sigmoid(gh)))
    o_ref[0] = jnp.concatenate(outs, axis=1).astype(o_ref.dtype)


def _gla_reduce_matrix():
    r = np.arange(C_CHUNK * LANES)[:, None]
    c = np.arange(2 * LANES)[None, :]
    hit = ((r % LANES) // C_KP == c // LANES) & (r // LANES == (c % LANES) % C_CHUNK)
    return jnp.asarray(hit, dtype=BF16)


def _gla(h3, wa, ba, gn, tb=512):
    bsz, seq, _ = h3.shape
    return pl.pallas_call(
        functools.partial(_gla_kernel, tb=tb),
        out_shape=jax.ShapeDtypeStruct((bsz, seq, C_HEADS * C_VP), BF16),
        grid=(bsz, C_HEADS // 2, seq // tb),
        in_specs=[pl.BlockSpec((1, tb, LANES), lambda b, p, t: (b, t, QC // LANES + p)),
                  pl.BlockSpec((1, tb, LANES), lambda b, p, t: (b, t, KC // LANES + p)),
                  pl.BlockSpec((1, tb, 2 * C_VP), lambda b, p, t: (b, t, VC // (2 * C_VP) + p)),
                  pl.BlockSpec((1, tb, 2 * C_VP), lambda b, p, t: (b, t, GC // (2 * C_VP) + p)),
                  pl.BlockSpec((1, tb, LANES), lambda b, p, t: (b, t, RC // LANES)),
                  pl.BlockSpec((1, LANES, LANES), lambda b, p, t: (p, 0, 0)),
                  pl.BlockSpec((1, 1, LANES), lambda b, p, t: (p, 0, 0)),
                  pl.BlockSpec((1, C_VP), lambda b, p, t: (0, 0)),
                  pl.BlockSpec((C_CHUNK * LANES, 2 * LANES), lambda b, p, t: (0, 0))],
        out_specs=pl.BlockSpec((1, tb, 2 * C_VP), lambda b, p, t: (b, t, p)),
        scratch_shapes=[pltpu.VMEM((2 * C_VP, LANES), F32)],
        compiler_params=_cparams(("parallel", "parallel", "arbitrary")),
        name="gla",
    )(h3, h3, h3, h3, h3, wa, ba, gn, _gla_reduce_matrix())


def _outproj_kernel(oa_ref, ob_ref, oc_ref, wa_ref, wb_ref, wc_ref, x_ref, g_ref, b_ref, y_ref, yb_ref):
    acc = jnp.dot(oa_ref[...], wa_ref[...], preferred_element_type=F32)
    acc += jnp.dot(ob_ref[...], wb_ref[...], preferred_element_type=F32)
    acc += jnp.dot(oc_ref[...], wc_ref[...], preferred_element_type=F32)
    y = _layer_norm(ALPHA * x_ref[...] + acc, g_ref[...], b_ref[...])
    y_ref[...] = y
    yb_ref[...] = y.astype(BF16)


def _outproj_ln(oa, ob, oc, wa, wb, wc, x, g, b, tm=512):
    t = x.shape[0]
    row = lambda n: pl.BlockSpec((tm, n), lambda i: (i, 0))
    full = lambda a: pl.BlockSpec(a.shape, lambda i: (0, 0))
    return pl.pallas_call(
        _outproj_kernel,
        out_shape=(jax.ShapeDtypeStruct((t, D_MODEL), F32), jax.ShapeDtypeStruct((t, D_MODEL), BF16)),
        grid=(t // tm,),
        in_specs=[row(oa.shape[1]), row(ob.shape[1]), row(oc.shape[1]), full(wa), full(wb), full(wc),
                  row(D_MODEL), full(g), full(b)],
        out_specs=(row(D_MODEL), row(D_MODEL)),
        compiler_params=_cparams(("parallel",)),
        name="outproj_ln",
    )(oa, ob, oc, wa, wb, wc, x, g, b)


def _matmul_kernel(a_ref, w_ref, o_ref):
    o_ref[...] = jnp.dot(a_ref[...], w_ref[...], preferred_element_type=F32).astype(o_ref.dtype)


def _matmul(a, w, out_dtype, tm=256, tn=512):
    m, k = a.shape
    n = w.shape[1]
    return pl.pallas_call(
        _matmul_kernel,
        out_shape=jax.ShapeDtypeStruct((m, n), out_dtype),
        grid=(m // tm, n // tn),
        in_specs=[pl.BlockSpec((tm, k), lambda i, j: (i, 0)), pl.BlockSpec((k, tn), lambda i, j: (0, j))],
        out_specs=pl.BlockSpec((tm, tn), lambda i, j: (i, j)),
        compiler_params=_cparams(("parallel", "arbitrary")),
        name="matmul",
    )(a, w)


def _xattn_kernel(xb_ref, x_ref, wq_ref, k_ref, v_ref, wo_ref, g_ref, b_ref, y_ref, yb_ref):
    q = jnp.dot(xb_ref[0], wq_ref[...], preferred_element_type=F32).astype(BF16)
    outs = []
    for h in range(XA_HEADS):
        cols = slice(h * XA_DIM, (h + 1) * XA_DIM)
        s = _dot_nt(q[:, cols], k_ref[0, :, cols]) * (XA_DIM ** -0.5)
        m = jnp.max(s, axis=-1, keepdims=True)
        p = jnp.exp(s - m)
        l = jnp.sum(p, axis=-1, keepdims=True)
        o = jnp.dot(p.astype(BF16), v_ref[0, :, cols], preferred_element_type=F32)
        outs.append((o / l).astype(BF16))
    o = jnp.concatenate(outs, axis=1)
    xa = jnp.dot(o, wo_ref[...], preferred_element_type=F32)
    y = _layer_norm(ALPHA * x_ref[0] + xa, g_ref[...], b_ref[...])
    y_ref[0] = y
    yb_ref[0] = y.astype(BF16)


def _xattn_ln(xb3, x3, wq, kv3, wo, g, b, tm=512):
    bsz, seq, _ = x3.shape
    mem_len = kv3.shape[1]
    row = pl.BlockSpec((1, tm, D_MODEL), lambda bi, i: (bi, i, 0))
    full = lambda a: pl.BlockSpec(a.shape, lambda bi, i: (0, 0))
    return pl.pallas_call(
        _xattn_kernel,
        out_shape=(jax.ShapeDtypeStruct((bsz, seq, D_MODEL), F32), jax.ShapeDtypeStruct((bsz, seq, D_MODEL), BF16)),
        grid=(bsz, seq // tm),
        in_specs=[row, row, full(wq),
                  pl.BlockSpec((1, mem_len, D_MODEL), lambda bi, i: (bi, 0, 0)),
                  pl.BlockSpec((1, mem_len, D_MODEL), lambda bi, i: (bi, 0, 1)),
                  full(wo), full(g), full(b)],
        out_specs=(row, row),
        compiler_params=_cparams(("parallel", "arbitrary")),
        name="xattn_ln",
    )(xb3, x3, wq, kv3, kv3, wo, g, b)


def _swiglu_chunk(xb, w1, w3, w2):
    hg = jnp.dot(xb, w1, preferred_element_type=F32)
    hu = jnp.dot(xb, w3, preferred_element_type=F32)
    a = hg * jax.nn.sigmoid(hg) * hu
    return jnp.dot(a.astype(BF16), w2, preferred_element_type=F32)


def _ffn_kernel(xb_ref, x_ref, w1_ref, w3_ref, w2_ref, g_ref, b_ref, y_ref, yb_ref, acc_ref):
    j = pl.program_id(1)

    @pl.when(j == 0)
    def _():
        acc_ref[...] = jnp.zeros_like(acc_ref)

    acc_ref[...] += _swiglu_chunk(xb_ref[...], w1_ref[...], w3_ref[...], w2_ref[...])

    @pl.when(j == pl.num_programs(1) - 1)
    def _():
        y = _layer_norm(ALPHA * x_ref[...] + acc_ref[...], g_ref[...], b_ref[...])
        y_ref[...] = y
        yb_ref[...] = y.astype(BF16)


def _ffn_ln(xb, x, w13, w2, g, b, tm=512, tf=1408):
    t = x.shape[0]
    nf = w2.shape[0] // tf
    row = pl.BlockSpec((tm, D_MODEL), lambda i, j: (i, 0))
    vec = lambda a: pl.BlockSpec(a.shape, lambda i, j: (0, 0))
    return pl.pallas_call(
        _ffn_kernel,
        out_shape=(jax.ShapeDtypeStruct((t, D_MODEL), F32), jax.ShapeDtypeStruct((t, D_MODEL), BF16)),
        grid=(t // tm, nf),
        in_specs=[row, row,
                  pl.BlockSpec((D_MODEL, tf), lambda i, j: (0, j)),
                  pl.BlockSpec((D_MODEL, tf), lambda i, j: (0, j + nf)),
                  pl.BlockSpec((tf, D_MODEL), lambda i, j: (j, 0)),
                  vec(g), vec(b)],
        out_specs=(row, row),
        scratch_shapes=[pltpu.VMEM((tm, D_MODEL), F32)],
        compiler_params=_cparams(("parallel", "arbitrary")),
        name="ffn_ln",
    )(xb, x, w13, w13, w2, g, b)


def _router_kernel(x_ref, w_ref, route_ref):
    logits = jnp.dot(x_ref[...], w_ref[...], preferred_element_type=F32, precision=lax.Precision.HIGHEST)
    lane = lax.broadcasted_iota(jnp.int32, logits.shape, 1)
    logits = jnp.where(lane < N_EXPERTS, logits, -jnp.inf)
    v1 = jnp.max(logits, axis=-1, keepdims=True)
    i1 = jnp.min(jnp.where(logits == v1, lane, LANES), axis=-1, keepdims=True)
    rest = jnp.where(lane == i1, -jnp.inf, logits)
    v2 = jnp.max(rest, axis=-1, keepdims=True)
    i2 = jnp.min(jnp.where(rest == v2, lane, LANES), axis=-1, keepdims=True)
    e2 = jnp.exp(v2 - v1)
    w1 = 1.0 / (1.0 + e2)
    w2 = e2 / (1.0 + e2)
    route_ref[...] = (jnp.where(lane == 0, w1, 0.0) + jnp.where(lane == 1, w2, 0.0)
                      + jnp.where(lane == 2, i1.astype(F32), 0.0) + jnp.where(lane == 3, i2.astype(F32), 0.0))


def _router(x, w, tm=1024):
    t = x.shape[0]
    return pl.pallas_call(
        _router_kernel,
        out_shape=jax.ShapeDtypeStruct((t, LANES), F32),
        grid=(t // tm,),
        in_specs=[pl.BlockSpec((tm, D_MODEL), lambda i: (i, 0)), pl.BlockSpec(w.shape, lambda i: (0, 0))],
        out_specs=pl.BlockSpec((tm, LANES), lambda i: (i, 0)),
        compiler_params=_cparams(("parallel",)),
        name="router",
    )(x, w)


def _dispatch_plan(route, tm):
    t = route.shape[0]
    expert = route[:, 2:4].astype(jnp.int32).reshape(-1)
    onehot = (expert[:, None] == jnp.arange(N_EXPERTS, dtype=jnp.int32)[None, :]).astype(jnp.int32)
    csum = jnp.cumsum(onehot, axis=0)
    rank = jnp.sum((csum - 1) * onehot, axis=1)
    counts = csum[-1]
    padded = (counts + tm - 1) // tm * tm
    ends = jnp.cumsum(padded)
    dest = (ends - padded)[expert] + rank
    n_tiles = TOP_K * t // tm + N_EXPERTS
    src = jnp.zeros((n_tiles * tm,), jnp.int32).at[dest].set(jnp.arange(TOP_K * t, dtype=jnp.int32) // TOP_K)
    tile_start = jnp.arange(n_tiles, dtype=jnp.int32) * tm
    tile_expert = jnp.sum((tile_start[:, None] >= ends[None, :]).astype(jnp.int32), axis=1)
    tile_expert = jnp.minimum(tile_expert, N_EXPERTS - 1)
    tile_used = (tile_start < ends[-1]).astype(jnp.int32)
    return src, dest.reshape(t, TOP_K).astype(jnp.int32), tile_expert, tile_used


ROW_SLABS = D_MODEL // LANES


def _slab(ref, s, rows):
    return ref.at[pl.ds(s, rows, stride=ROW_SLABS), :]


def _row_tile(ref, r):
    return ref.at[pl.ds(r * ROW_SLABS, ROW_SLABS), :]


DMA_THREADS = 2


def _moe_ffn_kernel(te_ref, tu_ref, src_ref, nxt_ref, x_hbm, w1_ref, w3_ref, w2_ref, y_ref,
                    xbuf, xs_ref, acc_ref, sem, *, tm):
    i = pl.program_id(0)
    j = pl.program_id(1)
    n_tiles = pl.num_programs(0)
    last = pl.num_programs(1) - 1
    slot = i % 2

    def start_rows(idx_ref, dst_slot):
        def issue(r, carry):
            for par in range(DMA_THREADS):
                row = DMA_THREADS * r + par
                pltpu.make_async_copy(_row_tile(x_hbm, idx_ref[0, 0, row]), _row_tile(xbuf.at[dst_slot], row),
                                      sem.at[dst_slot]).start(priority=par)
            return carry
        lax.fori_loop(0, tm // DMA_THREADS, issue, 0, unroll=4)

    @pl.when(j == 0)
    def _():
        @pl.when((i == 0) & (tu_ref[0] > 0))
        def _():
            start_rows(src_ref, 0)

        @pl.when(tu_ref[i] > 0)
        def _():
            pltpu.make_async_copy(x_hbm.at[pl.ds(0, tm * ROW_SLABS), :], xbuf.at[slot], sem.at[slot]).wait()
            for s in range(ROW_SLABS):
                xs_ref[:, s * LANES:(s + 1) * LANES] = _slab(xbuf.at[slot], s, tm)[...].astype(xs_ref.dtype)

        @pl.when((i + 1 < n_tiles) & (tu_ref[jnp.minimum(i + 1, n_tiles - 1)] > 0))
        def _():
            start_rows(nxt_ref, 1 - slot)

        acc_ref[...] = jnp.zeros_like(acc_ref)

    @pl.when(tu_ref[i] > 0)
    def _():
        acc_ref[...] += _swiglu_chunk(xs_ref[...], w1_ref[0], w3_ref[0], w2_ref[0])

    @pl.when(j == last)
    def _():
        for s in range(ROW_SLABS):
            _slab(y_ref, s, tm)[...] = acc_ref[:, s * LANES:(s + 1) * LANES]


def _moe_ffn(tile_expert, tile_used, src, x3, w13, w2, tm, tf):
    n_tiles = src.shape[0]
    nf = w2.shape[1] // tf
    chunk = lambda i, j, tu: jnp.where(tu[i] > 0, j, nf - 1)
    idx = lambda shift: pl.BlockSpec((1, 1, tm), lambda i, j, te, tu: (jnp.minimum(i + shift, n_tiles - 1), 0, 0),
                                     memory_space=pltpu.SMEM)
    return pl.pallas_call(
        functools.partial(_moe_ffn_kernel, tm=tm),
        out_shape=jax.ShapeDtypeStruct((n_tiles * tm * ROW_SLABS, LANES), F32),
        grid_spec=pltpu.PrefetchScalarGridSpec(
            num_scalar_prefetch=2, grid=(n_tiles, nf),
            in_specs=[idx(0), idx(1), pl.BlockSpec(memory_space=pl.ANY),
                      pl.BlockSpec((1, D_MODEL, tf), lambda i, j, te, tu: (te[i], 0, chunk(i, j, tu))),
                      pl.BlockSpec((1, D_MODEL, tf), lambda i, j, te, tu: (te[i], 0, chunk(i, j, tu) + nf)),
                      pl.BlockSpec((1, tf, D_MODEL), lambda i, j, te, tu: (te[i], chunk(i, j, tu), 0))],
            out_specs=pl.BlockSpec((tm * ROW_SLABS, LANES), lambda i, j, te, tu: (i, 0)),
            scratch_shapes=[pltpu.VMEM((2, tm * ROW_SLABS, LANES), F32), pltpu.VMEM((tm, D_MODEL), BF16),
                            pltpu.VMEM((tm, D_MODEL), F32), pltpu.SemaphoreType.DMA((2,))]),
        compiler_params=_cparams(("arbitrary", "arbitrary")),
        name="moe_ffn",
    )(tile_expert, tile_used, src, src, x3, w13, w13, w2)


def _combine_kernel(d0_ref, d1_ref, y_hbm, route_ref, x_ref, g_ref, b_ref, o_ref, ob_ref, buf0, buf1, sem, *, tm):
    def issue(r, carry):
        pltpu.make_async_copy(_row_tile(y_hbm, d0_ref[0, 0, r]), _row_tile(buf0, r), sem).start(priority=0)
        pltpu.make_async_copy(_row_tile(y_hbm, d1_ref[0, 0, r]), _row_tile(buf1, r), sem).start(priority=1)
        return carry

    lax.fori_loop(0, tm, issue, 0, unroll=8)
    pltpu.make_async_copy(y_hbm.at[pl.ds(0, tm * ROW_SLABS), :], buf0, sem).wait()
    pltpu.make_async_copy(y_hbm.at[pl.ds(0, tm * ROW_SLABS), :], buf1, sem).wait()
    route = route_ref[...]
    w0, w1 = route[:, 0:1], route[:, 1:2]
    f = jnp.concatenate([w0 * _slab(buf0, s, tm)[...] + w1 * _slab(buf1, s, tm)[...] for s in range(ROW_SLABS)],
                        axis=1)
    y = _layer_norm(ALPHA * x_ref[...] + f, g_ref[...], b_ref[...])
    o_ref[...] = y
    ob_ref[...] = y.astype(BF16)


def _combine_ln(dest, ys, route, x, g, b, tm=512):
    t = x.shape[0]
    row = pl.BlockSpec((tm, D_MODEL), lambda i: (i, 0))
    vec = lambda a: pl.BlockSpec(a.shape, lambda i: (0, 0))
    idx = pl.BlockSpec((1, 1, tm), lambda i: (i, 0, 0), memory_space=pltpu.SMEM)
    return pl.pallas_call(
        functools.partial(_combine_kernel, tm=tm),
        out_shape=(jax.ShapeDtypeStruct((t, D_MODEL), F32), jax.ShapeDtypeStruct((t, D_MODEL), BF16)),
        grid=(t // tm,),
        in_specs=[idx, idx, pl.BlockSpec(memory_space=pl.ANY),
                  pl.BlockSpec((tm, LANES), lambda i: (i, 0)),
                  row, vec(g), vec(b)],
        out_specs=(row, row),
        scratch_shapes=[pltpu.VMEM((tm * ROW_SLABS, LANES), F32), pltpu.VMEM((tm * ROW_SLABS, LANES), F32),
                        pltpu.SemaphoreType.DMA],
        compiler_params=_cparams(("arbitrary",)),
        name="moe_combine_ln",
    )(dest[:, 0].reshape(-1, 1, tm), dest[:, 1].reshape(-1, 1, tm), ys, route, x, g, b)


def _moe_ln(x, router_w, w13, w2, g, b, tm=1024, tf=896):
    route = _router(x, router_w)
    src, dest, tile_expert, tile_used = _dispatch_plan(route, tm)
    ys = _moe_ffn(tile_expert, tile_used, src.reshape(-1, 1, tm), x.reshape(-1, LANES), w13, w2, tm, tf)
    return _combine_ln(dest, ys, route, x, g, b)


def _pad_heads(w, heads, width, padded, axis):
    shape = list(w.shape)
    shape[axis:axis + 1] = [heads, width]
    w = w.reshape(shape)
    pad = [(0, 0)] * w.ndim
    pad[axis + 1] = (0, padded - width)
    w = jnp.pad(w, pad)
    shape[axis:axis + 2] = [heads * padded]
    return w.reshape(shape)


def _layout_w_in(w):
    bounds = np.cumsum([0, 256, 256, 256, 384, 384, 384, 192, 192, 384, 384, 16])
    qa, ka, va, qb, kb, vb, qc, kc, vc, gc, rc = [w[:, bounds[i]:bounds[i + 1]] for i in range(11)]
    rc = jnp.pad(rc, ((0, 0), (0, LANES - C_RANK)))
    qc = _pad_heads(qc, C_HEADS, C_K, C_KP, 1)
    kc = _pad_heads(kc, C_HEADS, C_K, C_KP, 1)
    vc = _pad_heads(vc, C_HEADS, C_V, C_VP, 1)
    gc = _pad_heads(gc, C_HEADS, C_V, C_VP, 1)
    return jnp.concatenate([qa, ka, qb, kb, va, vb, rc, qc, kc, vc, gc], axis=1).astype(BF16)


def _rope_tables(positions, dim):
    rot = dim // ROPE_FRAC
    inv_freq = ROPE_THETA ** (-jnp.arange(0, rot, 2, dtype=F32) / rot)
    ang = positions.astype(F32).reshape(-1, 1) * inv_freq
    cos, sin = jnp.cos(ang), jnp.sin(ang)
    t = ang.shape[0]
    c = jnp.concatenate([cos, cos, jnp.ones((t, dim - rot), F32)], axis=1)
    s = jnp.concatenate([-sin, sin, jnp.zeros((t, dim - rot), F32)], axis=1)
    reps = LANES // dim
    return jnp.tile(c, (1, reps)), jnp.tile(s, (1, reps))


def kernel(x, mem, positions, w_in, lam_q1, lam_k1, lam_q2, lam_k2, diff_norm_g, gla_wa2, gla_ba, gla_norm_g,
           w_out, ln_mix_g, ln_mix_b, xa_wq, xa_wkv, xa_wo, ln_xa_g, ln_xa_b, ffd_w13, ffd_w2, moe_router,
           moe_w13, moe_w2, ln_ffn_g, ln_ffn_b):
    bsz, seq, _ = x.shape
    t = bsz * seq
    ca, sa = _rope_tables(positions, A_QK)
    cb, sb = _rope_tables(positions, B_DIM)
    tabs = (ca, sa, cb, sb)
    memb = mem.reshape(-1, D_MODEL).astype(BF16)
    mem_len = mem.shape[1]
    row = lambda v: v.reshape(1, -1).astype(F32)

    xf = x.reshape(t, D_MODEL)
    xb = xf.astype(BF16)
    for l in range(DEPTH):
        lam_init = 0.8 - 0.6 * math.exp(-0.3 * l)
        lam = (jnp.exp(jnp.sum(lam_q1[l] * lam_k1[l])) - jnp.exp(jnp.sum(lam_q2[l] * lam_k2[l]))
               + lam_init).reshape(1, 1).astype(F32)
        h = _inproj(xb, _layout_w_in(w_in[l]), tabs)
        h3 = h.reshape(bsz, seq, H_COLS)
        oa = _diff_attention(h3, lam, jnp.tile(row(diff_norm_g[l]), (1, 2)), lam_init)
        ob = _dilated_attention(h3)
        wa = _pad_heads(gla_wa2[l], C_HEADS, C_K, C_KP, 1)
        wa = jnp.pad(wa, ((0, LANES - C_RANK), (0, 0))).reshape(LANES, 2, LANES).transpose(1, 0, 2).astype(BF16)
        ba = _pad_heads(gla_ba[l], C_HEADS, C_K, C_KP, 0).reshape(2, 1, LANES).astype(F32)
        gn = jnp.pad(row(gla_norm_g[l]), ((0, 0), (0, C_VP - C_V)))
        oc = _gla(h3, wa, ba, gn)
        wo = w_out[l]
        wo_a = wo[:A_HEADS * A_V].astype(BF16)
        wo_b = wo[A_HEADS * A_V:A_HEADS * A_V + B_HEADS * B_DIM].astype(BF16)
        wo_c = _pad_heads(wo[A_HEADS * A_V + B_HEADS * B_DIM:], C_HEADS, C_V, C_VP, 0).astype(BF16)
        xf, xb = _outproj_ln(oa.reshape(t, -1), ob.reshape(t, -1), oc.reshape(t, -1), wo_a, wo_b, wo_c,
                             xf, row(ln_mix_g[l]), row(ln_mix_b[l]))

        kv = _matmul(memb, xa_wkv[l].astype(BF16), BF16)
        x3, xb3 = _xattn_ln(xb.reshape(bsz, seq, D_MODEL), xf.reshape(bsz, seq, D_MODEL), xa_wq[l].astype(BF16),
                            kv.reshape(bsz, mem_len, 2 * D_MODEL), xa_wo[l].astype(BF16),
                            row(ln_xa_g[l]), row(ln_xa_b[l]))
        xf, xb = x3.reshape(t, D_MODEL), xb3.reshape(t, D_MODEL)

        if l % 2 == 0:
            xf, xb = _ffn_ln(xb, xf, ffd_w13[l // 2].astype(BF16), ffd_w2[l // 2].astype(BF16),
                             row(ln_ffn_g[l]), row(ln_ffn_b[l]))
        else:
            rw = jnp.pad(moe_router[l // 2], ((0, 0), (0, LANES - N_EXPERTS)))
            xf, xb = _moe_ln(xf, rw, moe_w13[l // 2].astype(BF16), moe_w2[l // 2].astype(BF16),
                             row(ln_ffn_g[l]), row(ln_ffn_b[l]))
    return xf.reshape(bsz, seq, D_MODEL)
```

```python
import functools
import math

import numpy as np
import jax
import jax.numpy as jnp
from jax import lax
from jax.experimental import pallas as pl
from jax.experimental.pallas import tpu as pltpu

F32 = jnp.float32
BF16 = jnp.bfloat16

LANES = 128
VMEM_LIMIT_BYTES = 56 * 1024 * 1024

D_MODEL = 1024
DEPTH = 2
ROPE_THETA = 500000.0
ROPE_FRAC = 4
LN_EPS = 1e-5
NEG_INF = -1e30
ALPHA = (2 * DEPTH) ** 0.25
LOG2_E = math.log2(math.e)

A_HEADS, A_QK, A_V = 4, 32, 64
B_HEADS, B_DIM = 6, 64
B_SPAN = 128
B_DILATIONS = (1, 4, 16)
C_HEADS, C_K, C_V = 4, 48, 96
C_KP, C_VP = 64, 128
C_RANK, C_TAU, C_CHUNK = 16, 16.0, 16
XA_HEADS, XA_DIM = 4, 256
N_EXPERTS, TOP_K = 8, 2

QA, KA, QB, KB, VA, VB, RC, QC, KC, VC, GC, H_COLS = (
    0, 256, 512, 896, 1280, 1536, 1920, 2048, 2304, 2560, 3072, 3584)
IN_TN = 256
ROPE_A_TILES = (QB - QA) // IN_TN
ROPE_B_TILES = (VA - QA) // IN_TN


def _cparams(sem):
    return pltpu.CompilerParams(dimension_semantics=sem, vmem_limit_bytes=VMEM_LIMIT_BYTES)


def _layer_norm(y, g, b):
    mu = jnp.mean(y, axis=-1, keepdims=True)
    yc = y - mu
    var = jnp.mean(yc * yc, axis=-1, keepdims=True)
    return yc * lax.rsqrt(var + LN_EPS) * g + b


def _dot_nt(a, b):
    return lax.dot_general(a, b, (((1,), (1,)), ((), ())), preferred_element_type=F32)


def _inproj_kernel(x_ref, w_ref, ca_ref, sa_ref, cb_ref, sb_ref, o_ref):
    x = x_ref[...]
    lane = lax.broadcasted_iota(jnp.int32, (x.shape[0], LANES), 1)

    def rope_cols(lo, k_lo, hi, c_ref, s_ref, half, period):
        acc = jnp.dot(x, w_ref[:, lo:hi], preferred_element_type=F32)
        c, s = c_ref[...], s_ref[...]
        first = lane % period < half
        for k in range((hi - lo) // LANES):
            a = acc[:, k * LANES:(k + 1) * LANES]
            up = pltpu.roll(a, LANES - half, 1)
            dn = pltpu.roll(a, half, 1)
            r = a * c + jnp.where(first, up, dn) * s
            if lo + k * LANES < k_lo:
                r = r * (period ** -0.5 * LOG2_E)
            o_ref[:, lo + k * LANES:lo + (k + 1) * LANES] = r.astype(o_ref.dtype)

    rope_cols(QA, KA, QB, ca_ref, sa_ref, A_QK // ROPE_FRAC // 2, A_QK)
    rope_cols(QB, KB, VA, cb_ref, sb_ref, B_DIM // ROPE_FRAC // 2, B_DIM)
    o_ref[:, VA:] = jnp.dot(x, w_ref[:, VA:], preferred_element_type=F32).astype(o_ref.dtype)


def _inproj(xb, w, tabs, tm=512):
    t = xb.shape[0]
    tab_spec = pl.BlockSpec((tm, LANES), lambda i: (i, 0))
    return pl.pallas_call(
        _inproj_kernel,
        out_shape=jax.ShapeDtypeStruct((t, H_COLS), BF16),
        grid=(t // tm,),
        in_specs=[pl.BlockSpec((tm, D_MODEL), lambda i: (i, 0)),
                  pl.BlockSpec((D_MODEL, H_COLS), lambda i: (0, 0)),
                  tab_spec, tab_spec, tab_spec, tab_spec],
        out_specs=pl.BlockSpec((tm, H_COLS), lambda i: (i, 0)),
        compiler_params=_cparams(("parallel",)),
        name="inproj_rope",
    )(xb, w, *tabs)


def _diff_kernel(lam_ref, q_ref, k_ref, v_ref, g_ref, o_ref, *, seq, tq, out_scale):
    lane = lax.broadcasted_iota(jnp.int32, (tq, LANES), 1)
    row = lax.broadcasted_iota(jnp.int32, (tq, tq), 0)
    col = lax.broadcasted_iota(jnp.int32, (tq, tq), 1)
    causal = col <= row
    lo = lane < A_V
    lam = lam_ref[0, 0]

    for i in range(seq // tq):
        past = i * tq
        q = q_ref[0, past:past + tq, :]
        k_d = k_ref[0, past:past + tq, :]
        v_d = v_ref[0, past:past + tq, :]

        def attend(c):
            qm = jnp.where(lane // A_QK == c, q, jnp.zeros_like(q))
            s_d = jnp.where(causal, _dot_nt(qm, k_d), NEG_INF)
            m = jnp.max(s_d, axis=-1, keepdims=True)
            if past:
                s_p = _dot_nt(qm, k_ref[0, :past, :])
                m = jnp.maximum(m, jnp.max(s_p, axis=-1, keepdims=True))
            p_d = jnp.exp2(s_d - m)
            l = jnp.sum(p_d, axis=-1, keepdims=True)
            acc = jnp.dot(p_d.astype(BF16), v_d, preferred_element_type=F32)
            if past:
                p_p = jnp.exp2(s_p - m)
                l = l + jnp.sum(p_p, axis=-1, keepdims=True)
                acc = acc + jnp.dot(p_p.astype(BF16), v_ref[0, :past, :], preferred_element_type=F32)
            return acc / l

        o_h0 = attend(0) - lam * attend(1)
        o_h1 = attend(2) - lam * attend(3)
        o = jnp.where(lo, o_h0, o_h1)
        sq = o * o
        s_lo = jnp.sum(jnp.where(lo, sq, 0.0), axis=-1, keepdims=True)
        s_all = jnp.sum(sq, axis=-1, keepdims=True)
        ms = jnp.where(lo, s_lo, s_all - s_lo) * (1.0 / A_V)
        o_ref[0, past:past + tq, :] = (o * lax.rsqrt(ms + LN_EPS) * g_ref[...] * out_scale).astype(o_ref.dtype)


def _diff_attention(h3, lam, g, lam_init, tq=512):
    bsz, seq, _ = h3.shape
    kern = functools.partial(_diff_kernel, seq=seq, tq=tq, out_scale=1.0 - lam_init)
    blk = lambda off: pl.BlockSpec((1, seq, LANES), lambda b, p: (b, 0, off // LANES + p))
    return pl.pallas_call(
        kern,
        out_shape=jax.ShapeDtypeStruct((bsz, seq, A_HEADS * A_V), BF16),
        grid=(bsz, A_HEADS // 2),
        in_specs=[pl.BlockSpec(memory_space=pltpu.SMEM), blk(QA), blk(KA), blk(VA),
                  pl.BlockSpec((1, LANES), lambda b, p: (0, 0))],
        out_specs=pl.BlockSpec((1, seq, LANES), lambda b, p: (b, 0, p)),
        compiler_params=_cparams(("parallel", "parallel")),
        name="diff_attention",
    )(lam, h3, h3, h3, g)


def _dil_kernel(q_ref, k_ref, v_ref, o_ref, qf, kf, vf, qs, ks, vs, m_s, l_s, a_s, *, seq):
    qf[...] = q_ref[0].astype(F32)
    kf[...] = k_ref[0].astype(F32)
    vf[...] = v_ref[0].astype(F32)
    span = B_SPAN
    lo = lax.broadcasted_iota(jnp.int32, (span, LANES), 1) < B_DIM
    own = ((lax.broadcasted_iota(jnp.int32, (2 * span, LANES), 1) < B_DIM)
           == (lax.broadcasted_iota(jnp.int32, (2 * span, LANES), 0) < span))
    qi = lax.broadcasted_iota(jnp.int32, (2 * span, 2 * span), 0) % span
    ki = lax.broadcasted_iota(jnp.int32, (2 * span, 2 * span), 1)
    band = (ki >= qi) & (ki <= qi + span)
    first = (lax.broadcasted_iota(jnp.int32, (2 * span, span), 1)
             <= lax.broadcasted_iota(jnp.int32, (2 * span, span), 0) % span)

    for g, r in enumerate(B_DILATIONS):
        sub = seq // r
        for rho in range(r):
            rows = pl.ds(rho, sub, stride=r) if r > 1 else pl.ds(0, sub)
            qs[rho * sub:(rho + 1) * sub, :] = qf[rows, :].astype(BF16)
            ks[rho * sub:(rho + 1) * sub, :] = kf[rows, :].astype(BF16)
            vs[rho * sub:(rho + 1) * sub, :] = vf[rows, :].astype(BF16)
        for rho in range(r):
            for n in range(sub // span):
                base = rho * sub + n * span
                qb = qs[base:base + span, :]
                lhs = jnp.where(own, jnp.concatenate([qb, qb], axis=0), jnp.zeros((), BF16))
                start = base - span if n else base
                s = jnp.where(band if n else first, _dot_nt(lhs, ks[start:base + span, :]), NEG_INF)
                m = jnp.max(s, axis=-1, keepdims=True)
                p = jnp.exp2(s - m)
                l = jnp.sum(p, axis=-1, keepdims=True)
                a = jnp.dot(p.astype(BF16), vs[start:base + span, :], preferred_element_type=F32)
                m = jnp.broadcast_to(m, (2 * span, LANES))
                l = jnp.broadcast_to(l, (2 * span, LANES))
                m_g = jnp.where(lo, m[:span], m[span:])
                l_g = jnp.where(lo, l[:span], l[span:])
                a_g = jnp.where(lo, a[:span], a[span:])
                nat = pl.ds(rho + r * span * n, span, stride=r) if r > 1 else pl.ds(n * span, span)
                if g == 0:
                    m_s[nat, :] = m_g
                    l_s[nat, :] = l_g
                    a_s[nat, :] = a_g
                else:
                    m_o = m_s[nat, :]
                    m_n = jnp.maximum(m_o, m_g)
                    w_o = jnp.exp2(m_o - m_n)
                    w_g = jnp.exp2(m_g - m_n)
                    m_s[nat, :] = m_n
                    l_s[nat, :] = w_o * l_s[nat, :] + w_g * l_g
                    a_s[nat, :] = w_o * a_s[nat, :] + w_g * a_g

    o_ref[0] = (a_s[...] / l_s[...]).astype(o_ref.dtype)


def _dilated_attention(h3):
    bsz, seq, _ = h3.shape
    blk = lambda off: pl.BlockSpec((1, seq, LANES), lambda b, p: (b, 0, off // LANES + p))
    return pl.pallas_call(
        functools.partial(_dil_kernel, seq=seq),
        out_shape=jax.ShapeDtypeStruct((bsz, seq, B_HEADS * B_DIM), BF16),
        grid=(bsz, B_HEADS // 2),
        in_specs=[blk(QB), blk(KB), blk(VB)],
        out_specs=pl.BlockSpec((1, seq, LANES), lambda b, p: (b, 0, p)),
        scratch_shapes=([pltpu.VMEM((seq, LANES), F32)] * 3 + [pltpu.VMEM((seq, LANES), BF16)] * 3
                        + [pltpu.VMEM((seq, LANES), F32)] * 3),
        compiler_params=_cparams(("parallel", "parallel")),
        name="dilated_attention",
    )(h3, h3, h3)


def _gla_kernel(q_ref, k_ref, v_ref, gate_ref, rc_ref, wa_ref, ba_ref, gn_ref, rcat_ref, o_ref, st, *, tb):
    t = pl.program_id(2)

    @pl.when(t == 0)
    def _():
        st[...] = jnp.zeros_like(st)

    ch, half, nc = C_CHUNK, C_CHUNK // 2, tb // C_CHUNK
    z = jnp.dot(rc_ref[0], wa_ref[0], preferred_element_type=F32) + ba_ref[0]
    la = (jnp.minimum(z, 0.0) - jnp.log(1.0 + jnp.exp(-jnp.abs(z)))) * (LOG2_E / C_TAU)
    ri = lax.broadcasted_iota(jnp.int32, (LANES, LANES), 0)
    ci = lax.broadcasted_iota(jnp.int32, (LANES, LANES), 1)
    same = (ri // ch) == (ci // ch)
    tri = jnp.where(same & (ci <= ri), 1.0, 0.0).astype(F32)
    blk = jnp.where(same, 1.0, 0.0).astype(F32)
    sums = lambda m: jnp.concatenate(
        [jnp.dot(m, la[r:r + LANES], preferred_element_type=F32, precision=lax.Precision.HIGHEST)
         for r in range(0, tb, LANES)], axis=0)
    b = sums(tri)
    bl = sums(blk)
    qs = q_ref[0].astype(F32) * (C_K ** -0.5)
    kf = k_ref[0].astype(F32)
    vb = v_ref[0]
    qt = (qs * jnp.exp2(b)).astype(BF16)
    kt = kf * jnp.exp2(bl - b)
    gl = jnp.exp2(bl)
    head0 = lax.broadcasted_iota(jnp.int32, (tb, LANES), 1) < C_KP
    kt_h = (jnp.where(head0, kt, 0.0).astype(BF16), jnp.where(head0, 0.0, kt).astype(BF16))

    def chunk_row(x, j, rows):
        xj = x.reshape(nc, ch, LANES)[:, j:j + 1, :]
        return jnp.broadcast_to(xj, (nc, rows, LANES)).reshape(nc * rows, LANES)

    def upper(x):
        return x.reshape(nc, 2, half, LANES)[:, 1].reshape(nc * half, LANES)

    rowi = lax.broadcasted_iota(jnp.int32, (tb, LANES), 0) % ch
    rowu = lax.broadcasted_iota(jnp.int32, (nc * half, LANES), 0) % half + half
    b_u, qs_u = upper(b), upper(qs)
    terms = []
    for j in range(ch):
        if j < half:
            e = jnp.exp2(jnp.where(rowi >= j, b - chunk_row(b, j, ch), NEG_INF))
            term = qs * e * chunk_row(kf, j, ch)
        else:
            e = jnp.exp2(jnp.where(rowu >= j, b_u - chunk_row(b, j, half), NEG_INF))
            tu = (qs_u * e * chunk_row(kf, j, half)).reshape(nc, 1, half, LANES)
            term = jnp.concatenate([jnp.zeros_like(tu), tu], axis=1).reshape(tb, LANES)
        terms.append(term.astype(BF16))
    s_tiled = jnp.dot(jnp.concatenate(terms, axis=1), rcat_ref[...], preferred_element_type=F32)
    rr = lax.broadcasted_iota(jnp.int32, (tb, 2 * LANES), 0)
    cc = lax.broadcasted_iota(jnp.int32, (tb, 2 * LANES), 1)
    m = jnp.where((rr % LANES) // ch == (cc % LANES) // ch, s_tiled, 0.0).astype(BF16)
    o_intra = jnp.concatenate(
        [jnp.concatenate([jnp.dot(m[r:r + LANES, h * LANES:(h + 1) * LANES], vb[r:r + LANES, h * C_VP:(h + 1) * C_VP],
                                  preferred_element_type=F32) for h in range(2)], axis=1)
         for r in range(0, tb, LANES)], axis=0)

    chunks = [slice(c * ch, (c + 1) * ch) for c in range(nc)]
    uts = [jnp.concatenate(
        [lax.dot_general(vb[rows, h * C_VP:(h + 1) * C_VP], kt_h[h][rows], (((0,), (0,)), ((), ())),
                         preferred_element_type=F32) for h in range(2)], axis=0) for rows in chunks]
    s_prev = st[...]
    states = []
    for c in range(nc):
        states.append(s_prev.astype(BF16))
        s_prev = gl[c * ch:c * ch + 1, :] * s_prev + uts[c]
    st[...] = s_prev
    o_inter = [_dot_nt(qt[rows], states[c]) for c, rows in enumerate(chunks)]

    o = o_intra + jnp.concatenate(o_inter, axis=0)
    gate = gate_ref[0].astype(F32)
    outs = []
    for hh in range(2):
        oh = o[:, hh * C_VP:(hh + 1) * C_VP]
        ms = jnp.sum(oh * oh, axis=-1, keepdims=True) * (1.0 / C_V)
        gh = gate[:, hh * C_VP:(hh + 1) * C_VP]
        outs.append(oh * lax.rsqrt(ms + LN_EPS) * gn_ref[...] * (gh * jax.nn.sigmoid(gh)))
    o_ref[0] = jnp.concatenate(outs, axis=1).astype(o_ref.dtype)


def _gla_reduce_matrix():
    r = np.arange(C_CHUNK * LANES)[:, None]
    c = np.arange(2 * LANES)[None, :]
    hit = ((r % LANES) // C_KP == c // LANES) & (r // LANES == (c % LANES) % C_CHUNK)
    return jnp.asarray(hit, dtype=BF16)


def _gla(h3, wa, ba, gn, tb=512):
    bsz, seq, _ = h3.shape
    return pl.pallas_call(
        functools.partial(_gla_kernel, tb=tb),
        out_shape=jax.ShapeDtypeStruct((bsz, seq, C_HEADS * C_VP), BF16),
        grid=(bsz, C_HEADS // 2, seq // tb),
        in_specs=[pl.BlockSpec((1, tb, LANES), lambda b, p, t: (b, t, QC // LANES + p)),
                  pl.BlockSpec((1, tb, LANES), lambda b, p, t: (b, t, KC // LANES + p)),
                  pl.BlockSpec((1, tb, 2 * C_VP), lambda b, p, t: (b, t, VC // (2 * C_VP) + p)),
                  pl.BlockSpec((1, tb, 2 * C_VP), lambda b, p, t: (b, t, GC // (2 * C_VP) + p)),
                  pl.BlockSpec((1, tb, LANES), lambda b, p, t: (b, t, RC // LANES)),
                  pl.BlockSpec((1, LANES, LANES), lambda b, p, t: (p, 0, 0)),
                  pl.BlockSpec((1, 1, LANES), lambda b, p, t: (p, 0, 0)),
                  pl.BlockSpec((1, C_VP), lambda b, p, t: (0, 0)),
                  pl.BlockSpec((C_CHUNK * LANES, 2 * LANES), lambda b, p, t: (0, 0))],
        out_specs=pl.BlockSpec((1, tb, 2 * C_VP), lambda b, p, t: (b, t, p)),
        scratch_shapes=[pltpu.VMEM((2 * C_VP, LANES), F32)],
        compiler_params=_cparams(("parallel", "parallel", "arbitrary")),
        name="gla",
    )(h3, h3, h3, h3, h3, wa, ba, gn, _gla_reduce_matrix())


def _matmul_kernel(a_ref, w_ref, o_ref):
    o_ref[...] = jnp.dot(a_ref[...], w_ref[...], preferred_element_type=F32).astype(o_ref.dtype)


def _matmul(a, w, out_dtype, tm=256, tn=512):
    m, k = a.shape
    n = w.shape[1]
    return pl.pallas_call(
        _matmul_kernel,
        out_shape=jax.ShapeDtypeStruct((m, n), out_dtype),
        grid=(m // tm, n // tn),
        in_specs=[pl.BlockSpec((tm, k), lambda i, j: (i, 0)), pl.BlockSpec((k, tn), lambda i, j: (0, j))],
        out_specs=pl.BlockSpec((tm, tn), lambda i, j: (i, j)),
        compiler_params=_cparams(("parallel", "arbitrary")),
        name="matmul",
    )(a, w)


def _route_top2(x, rw):
    logits = jnp.dot(x, rw, preferred_element_type=F32, precision=lax.Precision.HIGHEST)
    lane = lax.broadcasted_iota(jnp.int32, logits.shape, 1)
    logits = jnp.where(lane < N_EXPERTS, logits, -jnp.inf)
    v1 = jnp.max(logits, axis=-1, keepdims=True)
    i1 = jnp.min(jnp.where(logits == v1, lane, LANES), axis=-1, keepdims=True)
    rest = jnp.where(lane == i1, -jnp.inf, logits)
    v2 = jnp.max(rest, axis=-1, keepdims=True)
    i2 = jnp.min(jnp.where(rest == v2, lane, LANES), axis=-1, keepdims=True)
    e2 = jnp.exp(v2 - v1)
    w1 = 1.0 / (1.0 + e2)
    w2 = e2 / (1.0 + e2)
    return (jnp.where(lane == 0, w1, 0.0) + jnp.where(lane == 1, w2, 0.0)
            + jnp.where(lane == 2, i1.astype(F32), 0.0) + jnp.where(lane == 3, i2.astype(F32), 0.0))


def _mix_xattn_kernel(oa_ref, ob_ref, oc_ref, wa_ref, wb_ref, wc_ref, x_ref, g1_ref, b1_ref,
                      wq_ref, k_ref, v_ref, wo_ref, g2_ref, b2_ref, *rest, route):
    if route:
        rw_ref, y_ref, yb_ref, ys_ref, route_ref = rest
    else:
        y_ref, yb_ref = rest
    acc = jnp.dot(oa_ref[0], wa_ref[...], preferred_element_type=F32)
    acc += jnp.dot(ob_ref[0], wb_ref[...], preferred_element_type=F32)
    acc += jnp.dot(oc_ref[0], wc_ref[...], preferred_element_type=F32)
    x1 = _layer_norm(ALPHA * x_ref[0] + acc, g1_ref[...], b1_ref[...])
    q = jnp.dot(x1.astype(BF16), wq_ref[...], preferred_element_type=F32).astype(BF16)
    outs = []
    for h in range(XA_HEADS):
        cols = slice(h * XA_DIM, (h + 1) * XA_DIM)
        s = _dot_nt(q[:, cols], k_ref[0, :, cols]) * (XA_DIM ** -0.5)
        m = jnp.max(s, axis=-1, keepdims=True)
        p = jnp.exp(s - m)
        l = jnp.sum(p, axis=-1, keepdims=True)
        o = jnp.dot(p.astype(BF16), v_ref[0, :, cols], preferred_element_type=F32)
        outs.append((o / l).astype(BF16))
    xa = jnp.dot(jnp.concatenate(outs, axis=1), wo_ref[...], preferred_element_type=F32)
    y = _layer_norm(ALPHA * x1 + xa, g2_ref[...], b2_ref[...])
    y_ref[0] = y
    yb_ref[0] = y.astype(BF16)
    if route:
        route_ref[0] = _route_top2(y, rw_ref[...])
        for s in range(ROW_SLABS):
            _slab(ys_ref, s, y.shape[0])[...] = y[:, s * LANES:(s + 1) * LANES]


def _mix_xattn_ln(oa, ob, oc, wo_parts, x3, ln1, wq, kv3, wo, ln2, router_w=None, tm=512):
    bsz, seq, _ = x3.shape
    mem_len = kv3.shape[1]
    route = router_w is not None
    row = lambda n: pl.BlockSpec((1, tm, n), lambda bi, i: (bi, i, 0))
    full = lambda a: pl.BlockSpec(a.shape, lambda bi, i: (0,) * a.ndim)
    steps = seq // tm
    ins = [oa, ob, oc, *wo_parts, x3, *ln1, wq, kv3, kv3, wo, *ln2]
    in_specs = [row(oa.shape[2]), row(ob.shape[2]), row(oc.shape[2]), *[full(w) for w in wo_parts],
                row(D_MODEL), *[full(v) for v in ln1], full(wq),
                pl.BlockSpec((1, mem_len, D_MODEL), lambda bi, i: (bi, 0, 0)),
                pl.BlockSpec((1, mem_len, D_MODEL), lambda bi, i: (bi, 0, 1)),
                full(wo), *[full(v) for v in ln2]]
    out_shape = [jax.ShapeDtypeStruct((bsz, seq, D_MODEL), F32), jax.ShapeDtypeStruct((bsz, seq, D_MODEL), BF16)]
    out_specs = [row(D_MODEL), row(D_MODEL)]
    if route:
        ins.append(router_w)
        in_specs.append(full(router_w))
        out_shape += [jax.ShapeDtypeStruct((bsz * seq * ROW_SLABS, LANES), F32),
                      jax.ShapeDtypeStruct((bsz, seq, LANES), F32)]
        out_specs += [pl.BlockSpec((tm * ROW_SLABS, LANES), lambda bi, i: (bi * steps + i, 0)), row(LANES)]
    return pl.pallas_call(
        functools.partial(_mix_xattn_kernel, route=route),
        out_shape=tuple(out_shape),
        grid=(bsz, steps),
        in_specs=in_specs,
        out_specs=tuple(out_specs),
        compiler_params=_cparams(("parallel", "arbitrary")),
        name="mix_xattn_ln",
    )(*ins)


def _swiglu_chunk(xb, w1, w3, w2):
    hg = jnp.dot(xb, w1, preferred_element_type=F32)
    hu = jnp.dot(xb, w3, preferred_element_type=F32)
    a = hg * jax.nn.sigmoid(hg) * hu
    return jnp.dot(a.astype(BF16), w2, preferred_element_type=F32)


def _ffn_kernel(xb_ref, x_ref, w1_ref, w3_ref, w2_ref, g_ref, b_ref, y_ref, yb_ref, acc_ref):
    j = pl.program_id(1)

    @pl.when(j == 0)
    def _():
        acc_ref[...] = jnp.zeros_like(acc_ref)

    acc_ref[...] += _swiglu_chunk(xb_ref[...], w1_ref[...], w3_ref[...], w2_ref[...])

    @pl.when(j == pl.num_programs(1) - 1)
    def _():
        y = _layer_norm(ALPHA * x_ref[...] + acc_ref[...], g_ref[...], b_ref[...])
        y_ref[...] = y
        yb_ref[...] = y.astype(BF16)


def _ffn_ln(xb, x, w13, w2, g, b, tm=512, tf=1408):
    t = x.shape[0]
    nf = w2.shape[0] // tf
    row = pl.BlockSpec((tm, D_MODEL), lambda i, j: (i, 0))
    vec = lambda a: pl.BlockSpec(a.shape, lambda i, j: (0, 0))
    return pl.pallas_call(
        _ffn_kernel,
        out_shape=(jax.ShapeDtypeStruct((t, D_MODEL), F32), jax.ShapeDtypeStruct((t, D_MODEL), BF16)),
        grid=(t // tm, nf),
        in_specs=[row, row,
                  pl.BlockSpec((D_MODEL, tf), lambda i, j: (0, j)),
                  pl.BlockSpec((D_MODEL, tf), lambda i, j: (0, j + nf)),
                  pl.BlockSpec((tf, D_MODEL), lambda i, j: (j, 0)),
                  vec(g), vec(b)],
        out_specs=(row, row),
        scratch_shapes=[pltpu.VMEM((tm, D_MODEL), F32)],
        compiler_params=_cparams(("parallel", "arbitrary")),
        name="ffn_ln",
    )(xb, x, w13, w13, w2, g, b)


def _dispatch_plan(route, tm):
    t = route.shape[0]
    expert = route[:, 2:4].astype(jnp.int32).reshape(-1)
    onehot = (expert[:, None] == jnp.arange(N_EXPERTS, dtype=jnp.int32)[None, :]).astype(jnp.int32)
    csum = jnp.cumsum(onehot, axis=0)
    rank = jnp.sum((csum - 1) * onehot, axis=1)
    counts = csum[-1]
    padded = (counts + tm - 1) // tm * tm
    ends = jnp.cumsum(padded)
    dest = (ends - padded)[expert] + rank
    n_tiles = TOP_K * t // tm + N_EXPERTS
    src = jnp.zeros((n_tiles * tm,), jnp.int32).at[dest].set(jnp.arange(TOP_K * t, dtype=jnp.int32) // TOP_K)
    tile_start = jnp.arange(n_tiles, dtype=jnp.int32) * tm
    tile_expert = jnp.sum((tile_start[:, None] >= ends[None, :]).astype(jnp.int32), axis=1)
    tile_expert = jnp.minimum(tile_expert, N_EXPERTS - 1)
    tile_used = (tile_start < ends[-1]).astype(jnp.int32)
    return src, dest.reshape(t, TOP_K).astype(jnp.int32), tile_expert, tile_used


ROW_SLABS = D_MODEL // LANES


def _slab(ref, s, rows):
    return ref.at[pl.ds(s, rows, stride=ROW_SLABS), :]


def _row_tile(ref, r):
    return ref.at[pl.ds(r * ROW_SLABS, ROW_SLABS), :]


DMA_THREADS = 2


def _moe_ffn_kernel(te_ref, tu_ref, src_ref, nxt_ref, x_hbm, w1_ref, w3_ref, w2_ref, y_ref,
                    xbuf, xs_ref, acc_ref, sem, *, tm):
    i = pl.program_id(0)
    j = pl.program_id(1)
    n_tiles = pl.num_programs(0)
    last = pl.num_programs(1) - 1
    slot = i % 2

    def start_rows(idx_ref, dst_slot):
        def issue(r, carry):
            for par in range(DMA_THREADS):
                row = DMA_THREADS * r + par
                pltpu.make_async_copy(_row_tile(x_hbm, idx_ref[0, 0, row]), _row_tile(xbuf.at[dst_slot], row),
                                      sem.at[dst_slot]).start(priority=par)
            return carry
        lax.fori_loop(0, tm // DMA_THREADS, issue, 0, unroll=4)

    @pl.when(j == 0)
    def _():
        @pl.when((i == 0) & (tu_ref[0] > 0))
        def _():
            start_rows(src_ref, 0)

        @pl.when(tu_ref[i] > 0)
        def _():
            pltpu.make_async_copy(x_hbm.at[pl.ds(0, tm * ROW_SLABS), :], xbuf.at[slot], sem.at[slot]).wait()
            for s in range(ROW_SLABS):
                xs_ref[:, s * LANES:(s + 1) * LANES] = _slab(xbuf.at[slot], s, tm)[...].astype(xs_ref.dtype)

        @pl.when((i + 1 < n_tiles) & (tu_ref[jnp.minimum(i + 1, n_tiles - 1)] > 0))
        def _():
            start_rows(nxt_ref, 1 - slot)

        acc_ref[...] = jnp.zeros_like(acc_ref)

    @pl.when(tu_ref[i] > 0)
    def _():
        acc_ref[...] += _swiglu_chunk(xs_ref[...], w1_ref[0], w3_ref[0], w2_ref[0])

    @pl.when(j == last)
    def _():
        for s in range(ROW_SLABS):
            _slab(y_ref, s, tm)[...] = acc_ref[:, s * LANES:(s + 1) * LANES]


def _moe_ffn(tile_expert, tile_used, src, x3, w13, w2, tm, tf):
    n_tiles = src.shape[0]
    nf = w2.shape[1] // tf
    chunk = lambda i, j, tu: jnp.where(tu[i] > 0, j, nf - 1)
    idx = lambda shift: pl.BlockSpec((1, 1, tm), lambda i, j, te, tu: (jnp.minimum(i + shift, n_tiles - 1), 0, 0),
                                     memory_space=pltpu.SMEM)
    return pl.pallas_call(
        functools.partial(_moe_ffn_kernel, tm=tm),
        out_shape=jax.ShapeDtypeStruct((n_tiles * tm * ROW_SLABS, LANES), F32),
        grid_spec=pltpu.PrefetchScalarGridSpec(
            num_scalar_prefetch=2, grid=(n_tiles, nf),
            in_specs=[idx(0), idx(1), pl.BlockSpec(memory_space=pl.ANY),
                      pl.BlockSpec((1, D_MODEL, tf), lambda i, j, te, tu: (te[i], 0, chunk(i, j, tu))),
                      pl.BlockSpec((1, D_MODEL, tf), lambda i, j, te, tu: (te[i], 0, chunk(i, j, tu) + nf)),
                      pl.BlockSpec((1, tf, D_MODEL), lambda i, j, te, tu: (te[i], chunk(i, j, tu), 0))],
            out_specs=pl.BlockSpec((tm * ROW_SLABS, LANES), lambda i, j, te, tu: (i, 0)),
            scratch_shapes=[pltpu.VMEM((2, tm * ROW_SLABS, LANES), F32), pltpu.VMEM((tm, D_MODEL), BF16),
                            pltpu.VMEM((tm, D_MODEL), F32), pltpu.SemaphoreType.DMA((2,))]),
        compiler_params=_cparams(("arbitrary", "arbitrary")),
        name="moe_ffn",
    )(tile_expert, tile_used, src, src, x3, w13, w13, w2)


def _combine_kernel(d0_ref, d1_ref, y_hbm, route_ref, x_ref, g_ref, b_ref, o_ref, ob_ref, buf0, buf1, sem, *, tm):
    def issue(r, carry):
        pltpu.make_async_copy(_row_tile(y_hbm, d0_ref[0, 0, r]), _row_tile(buf0, r), sem).start(priority=0)
        pltpu.make_async_copy(_row_tile(y_hbm, d1_ref[0, 0, r]), _row_tile(buf1, r), sem).start(priority=1)
        return carry

    lax.fori_loop(0, tm, issue, 0, unroll=8)
    pltpu.make_async_copy(y_hbm.at[pl.ds(0, tm * ROW_SLABS), :], buf0, sem).wait()
    pltpu.make_async_copy(y_hbm.at[pl.ds(0, tm * ROW_SLABS), :], buf1, sem).wait()
    route = route_ref[...]
    w0, w1 = route[:, 0:1], route[:, 1:2]
    f = jnp.concatenate([w0 * _slab(buf0, s, tm)[...] + w1 * _slab(buf1, s, tm)[...] for s in range(ROW_SLABS)],
                        axis=1)
    y = _layer_norm(ALPHA * x_ref[...] + f, g_ref[...], b_ref[...])
    o_ref[...] = y
    ob_ref[...] = y.astype(BF16)


def _combine_ln(dest, ys, route, x, g, b, tm=512):
    t = x.shape[0]
    row = pl.BlockSpec((tm, D_MODEL), lambda i: (i, 0))
    vec = lambda a: pl.BlockSpec(a.shape, lambda i: (0, 0))
    idx = pl.BlockSpec((1, 1, tm), lambda i: (i, 0, 0), memory_space=pltpu.SMEM)
    return pl.pallas_call(
        functools.partial(_combine_kernel, tm=tm),
        out_shape=(jax.ShapeDtypeStruct((t, D_MODEL), F32), jax.ShapeDtypeStruct((t, D_MODEL), BF16)),
        grid=(t // tm,),
        in_specs=[idx, idx, pl.BlockSpec(memory_space=pl.ANY),
                  pl.BlockSpec((tm, LANES), lambda i: (i, 0)),
                  row, vec(g), vec(b)],
        out_specs=(row, row),
        scratch_shapes=[pltpu.VMEM((tm * ROW_SLABS, LANES), F32), pltpu.VMEM((tm * ROW_SLABS, LANES), F32),
                        pltpu.SemaphoreType.DMA],
        compiler_params=_cparams(("arbitrary",)),
        name="moe_combine_ln",
    )(dest[:, 0].reshape(-1, 1, tm), dest[:, 1].reshape(-1, 1, tm), ys, route, x, g, b)


def _moe_ln(x, x_slab, route, w13, w2, g, b, tm=1024, tf=896):
    src, dest, tile_expert, tile_used = _dispatch_plan(route, tm)
    ys = _moe_ffn(tile_expert, tile_used, src.reshape(-1, 1, tm), x_slab, w13, w2, tm, tf)
    return _combine_ln(dest, ys, route, x, g, b)


def _pad_heads(w, heads, width, padded, axis):
    shape = list(w.shape)
    shape[axis:axis + 1] = [heads, width]
    w = w.reshape(shape)
    pad = [(0, 0)] * w.ndim
    pad[axis + 1] = (0, padded - width)
    w = jnp.pad(w, pad)
    shape[axis:axis + 2] = [heads * padded]
    return w.reshape(shape)


def _layout_w_in(w):
    bounds = np.cumsum([0, 256, 256, 256, 384, 384, 384, 192, 192, 384, 384, 16])
    qa, ka, va, qb, kb, vb, qc, kc, vc, gc, rc = [w[:, bounds[i]:bounds[i + 1]] for i in range(11)]
    rc = jnp.pad(rc, ((0, 0), (0, LANES - C_RANK)))
    qc = _pad_heads(qc, C_HEADS, C_K, C_KP, 1)
    kc = _pad_heads(kc, C_HEADS, C_K, C_KP, 1)
    vc = _pad_heads(vc, C_HEADS, C_V, C_VP, 1)
    gc = _pad_heads(gc, C_HEADS, C_V, C_VP, 1)
    return jnp.concatenate([qa, ka, qb, kb, va, vb, rc, qc, kc, vc, gc], axis=1).astype(BF16)


def _rope_tables(positions, dim):
    rot = dim // ROPE_FRAC
    inv_freq = ROPE_THETA ** (-jnp.arange(0, rot, 2, dtype=F32) / rot)
    ang = positions.astype(F32).reshape(-1, 1) * inv_freq
    cos, sin = jnp.cos(ang), jnp.sin(ang)
    t = ang.shape[0]
    c = jnp.concatenate([cos, cos, jnp.ones((t, dim - rot), F32)], axis=1)
    s = jnp.concatenate([-sin, sin, jnp.zeros((t, dim - rot), F32)], axis=1)
    reps = LANES // dim
    return jnp.tile(c, (1, reps)), jnp.tile(s, (1, reps))


def kernel(x, mem, positions, w_in, lam_q1, lam_k1, lam_q2, lam_k2, diff_norm_g, gla_wa2, gla_ba, gla_norm_g,
           w_out, ln_mix_g, ln_mix_b, xa_wq, xa_wkv, xa_wo, ln_xa_g, ln_xa_b, ffd_w13, ffd_w2, moe_router,
           moe_w13, moe_w2, ln_ffn_g, ln_ffn_b):
    bsz, seq, _ = x.shape
    t = bsz * seq
    ca, sa = _rope_tables(positions, A_QK)
    cb, sb = _rope_tables(positions, B_DIM)
    tabs = (ca, sa, cb, sb)
    memb = mem.reshape(-1, D_MODEL).astype(BF16)
    mem_len = mem.shape[1]
    row = lambda v: v.reshape(1, -1).astype(F32)

    xf = x.reshape(t, D_MODEL)
    xb = xf.astype(BF16)
    for l in range(DEPTH):
        lam_init = 0.8 - 0.6 * math.exp(-0.3 * l)
        lam = (jnp.exp(jnp.sum(lam_q1[l] * lam_k1[l])) - jnp.exp(jnp.sum(lam_q2[l] * lam_k2[l]))
               + lam_init).reshape(1, 1).astype(F32)
        h = _inproj(xb, _layout_w_in(w_in[l]), tabs)
        h3 = h.reshape(bsz, seq, H_COLS)
        oa = _diff_attention(h3, lam, jnp.tile(row(diff_norm_g[l]), (1, 2)), lam_init)
        ob = _dilated_attention(h3)
        wa = _pad_heads(gla_wa2[l], C_HEADS, C_K, C_KP, 1)
        wa = jnp.pad(wa, ((0, LANES - C_RANK), (0, 0))).reshape(LANES, 2, LANES).transpose(1, 0, 2).astype(BF16)
        ba = _pad_heads(gla_ba[l], C_HEADS, C_K, C_KP, 0).reshape(2, 1, LANES).astype(F32)
        gn = jnp.pad(row(gla_norm_g[l]), ((0, 0), (0, C_VP - C_V)))
        oc = _gla(h3, wa, ba, gn)
        wo = w_out[l]
        wo_a = wo[:A_HEADS * A_V].astype(BF16)
        wo_b = wo[A_HEADS * A_V:A_HEADS * A_V + B_HEADS * B_DIM].astype(BF16)
        wo_c = _pad_heads(wo[A_HEADS * A_V + B_HEADS * B_DIM:], C_HEADS, C_V, C_VP, 0).astype(BF16)
        kv = _matmul(memb, xa_wkv[l].astype(BF16), BF16).reshape(bsz, mem_len, 2 * D_MODEL)
        moe = l % 2 == 1
        rw = jnp.pad(moe_router[l // 2], ((0, 0), (0, LANES - N_EXPERTS))) if moe else None
        outs = _mix_xattn_ln(oa, ob, oc, (wo_a, wo_b, wo_c), xf.reshape(bsz, seq, D_MODEL),
                             (row(ln_mix_g[l]), row(ln_mix_b[l])), xa_wq[l].astype(BF16), kv, xa_wo[l].astype(BF16),
                             (row(ln_xa_g[l]), row(ln_xa_b[l])), router_w=rw)
        xf, xb = outs[0].reshape(t, D_MODEL), outs[1].reshape(t, D_MODEL)
        if moe:
            xf, xb = _moe_ln(xf, outs[2], outs[3].reshape(t, LANES), moe_w13[l // 2].astype(BF16),
                             moe_w2[l // 2].astype(BF16), row(ln_ffn_g[l]), row(ln_ffn_b[l]))
        else:
            xf, xb = _ffn_ln(xb, xf, ffd_w13[l // 2].astype(BF16), ffd_w2[l // 2].astype(BF16),
                             row(ln_ffn_g[l]), row(ln_ffn_b[l]))
    return xf.reshape(bsz, seq, D_MODEL)
```

```python
import functools
import math

import numpy as np
import jax
import jax.numpy as jnp
from jax import lax
from jax.experimental import pallas as pl
from jax.experimental.pallas import tpu as pltpu

F32 = jnp.float32
BF16 = jnp.bfloat16

LANES = 128
VMEM_LIMIT_BYTES = 56 * 1024 * 1024

D_MODEL = 1024
DEPTH = 2
ROPE_THETA = 500000.0
ROPE_FRAC = 4
LN_EPS = 1e-5
NEG_INF = -1e30
ALPHA = (2 * DEPTH) ** 0.25
LOG2_E = math.log2(math.e)

A_HEADS, A_QK, A_V = 4, 32, 64
B_HEADS, B_DIM = 6, 64
B_SPAN = 128
B_DILATIONS = (1, 4, 16)
C_HEADS, C_K, C_V = 4, 48, 96
C_KP, C_VP = 64, 128
C_RANK, C_TAU, C_CHUNK = 16, 16.0, 16
XA_HEADS, XA_DIM = 4, 256
N_EXPERTS, TOP_K = 8, 2

QA, KA, QB, KB, VA, VB, RC, QC, KC, VC, GC, H_COLS = (
    0, 256, 512, 896, 1280, 1536, 1920, 2048, 2304, 2560, 3072, 3584)
IN_TN = 256
ROPE_A_TILES = (QB - QA) // IN_TN
ROPE_B_TILES = (VA - QA) // IN_TN


def _cparams(sem):
    return pltpu.CompilerParams(dimension_semantics=sem, vmem_limit_bytes=VMEM_LIMIT_BYTES)


def _layer_norm(y, g, b):
    mu = jnp.mean(y, axis=-1, keepdims=True)
    yc = y - mu
    var = jnp.mean(yc * yc, axis=-1, keepdims=True)
    return yc * lax.rsqrt(var + LN_EPS) * g + b


def _dot_nt(a, b):
    return lax.dot_general(a, b, (((1,), (1,)), ((), ())), preferred_element_type=F32)


def _inproj_kernel(x_ref, w_ref, ca_ref, sa_ref, cb_ref, sb_ref, o_ref):
    x = x_ref[...]
    lane = lax.broadcasted_iota(jnp.int32, (x.shape[0], LANES), 1)

    def rope_cols(lo, k_lo, hi, c_ref, s_ref, half, period):
        acc = jnp.dot(x, w_ref[:, lo:hi], preferred_element_type=F32)
        c, s = c_ref[...], s_ref[...]
        first = lane % period < half
        for k in range((hi - lo) // LANES):
            a = acc[:, k * LANES:(k + 1) * LANES]
            up = pltpu.roll(a, LANES - half, 1)
            dn = pltpu.roll(a, half, 1)
            r = a * c + jnp.where(first, up, dn) * s
            if lo + k * LANES < k_lo:
                r = r * (period ** -0.5 * LOG2_E)
            o_ref[:, lo + k * LANES:lo + (k + 1) * LANES] = r.astype(o_ref.dtype)

    rope_cols(QA, KA, QB, ca_ref, sa_ref, A_QK // ROPE_FRAC // 2, A_QK)
    rope_cols(QB, KB, VA, cb_ref, sb_ref, B_DIM // ROPE_FRAC // 2, B_DIM)
    o_ref[:, VA:] = jnp.dot(x, w_ref[:, VA:], preferred_element_type=F32).astype(o_ref.dtype)


def _inproj(xb, w, tabs, tm=512):
    t = xb.shape[0]
    tab_spec = pl.BlockSpec((tm, LANES), lambda i: (i, 0))
    return pl.pallas_call(
        _inproj_kernel,
        out_shape=jax.ShapeDtypeStruct((t, H_COLS), BF16),
        grid=(t // tm,),
        in_specs=[pl.BlockSpec((tm, D_MODEL), lambda i: (i, 0)),
                  pl.BlockSpec((D_MODEL, H_COLS), lambda i: (0, 0)),
                  tab_spec, tab_spec, tab_spec, tab_spec],
        out_specs=pl.BlockSpec((tm, H_COLS), lambda i: (i, 0)),
        compiler_params=_cparams(("parallel",)),
        name="inproj_rope",
    )(xb, w, *tabs)


def _diff_kernel(lam_ref, q_ref, k_ref, v_ref, g_ref, o_ref, *, seq, tq, out_scale):
    lane = lax.broadcasted_iota(jnp.int32, (tq, LANES), 1)
    row = lax.broadcasted_iota(jnp.int32, (tq, tq), 0)
    col = lax.broadcasted_iota(jnp.int32, (tq, tq), 1)
    causal = col <= row
    lo = lane < A_V
    lam = lam_ref[0, 0]

    for i in range(seq // tq):
        past = i * tq
        q = q_ref[0, past:past + tq, :]
        k_d = k_ref[0, past:past + tq, :]
        v_d = v_ref[0, past:past + tq, :]

        def attend(c):
            qm = jnp.where(lane // A_QK == c, q, jnp.zeros_like(q))
            s_d = jnp.where(causal, _dot_nt(qm, k_d), NEG_INF)
            m = jnp.max(s_d, axis=-1, keepdims=True)
            if past:
                s_p = _dot_nt(qm, k_ref[0, :past, :])
                m = jnp.maximum(m, jnp.max(s_p, axis=-1, keepdims=True))
            p_d = jnp.exp2(s_d - m)
            l = jnp.sum(p_d, axis=-1, keepdims=True)
            acc = jnp.dot(p_d.astype(BF16), v_d, preferred_element_type=F32)
            if past:
                p_p = jnp.exp2(s_p - m)
                l = l + jnp.sum(p_p, axis=-1, keepdims=True)
                acc = acc + jnp.dot(p_p.astype(BF16), v_ref[0, :past, :], preferred_element_type=F32)
            return acc / l

        o_h0 = attend(0) - lam * attend(1)
        o_h1 = attend(2) - lam * attend(3)
        o = jnp.where(lo, o_h0, o_h1)
        sq = o * o
        s_lo = jnp.sum(jnp.where(lo, sq, 0.0), axis=-1, keepdims=True)
        s_all = jnp.sum(sq, axis=-1, keepdims=True)
        ms = jnp.where(lo, s_lo, s_all - s_lo) * (1.0 / A_V)
        o_ref[0, past:past + tq, :] = (o * lax.rsqrt(ms + LN_EPS) * g_ref[...] * out_scale).astype(o_ref.dtype)


def _diff_attention(h3, lam, g, lam_init, tq=512):
    bsz, seq, _ = h3.shape
    kern = functools.partial(_diff_kernel, seq=seq, tq=tq, out_scale=1.0 - lam_init)
    blk = lambda off: pl.BlockSpec((1, seq, LANES), lambda b, p: (b, 0, off // LANES + p))
    return pl.pallas_call(
        kern,
        out_shape=jax.ShapeDtypeStruct((bsz, seq, A_HEADS * A_V), BF16),
        grid=(bsz, A_HEADS // 2),
        in_specs=[pl.BlockSpec(memory_space=pltpu.SMEM), blk(QA), blk(KA), blk(VA),
                  pl.BlockSpec((1, LANES), lambda b, p: (0, 0))],
        out_specs=pl.BlockSpec((1, seq, LANES), lambda b, p: (b, 0, p)),
        compiler_params=_cparams(("parallel", "parallel")),
        name="diff_attention",
    )(lam, h3, h3, h3, g)


def _dil_kernel(q_ref, k_ref, v_ref, o_ref, qf, kf, vf, qs, ks, vs, m_s, l_s, a_s, *, seq):
    qf[...] = q_ref[0].astype(F32)
    kf[...] = k_ref[0].astype(F32)
    vf[...] = v_ref[0].astype(F32)
    span = B_SPAN
    lo = lax.broadcasted_iota(jnp.int32, (span, LANES), 1) < B_DIM
    own = ((lax.broadcasted_iota(jnp.int32, (2 * span, LANES), 1) < B_DIM)
           == (lax.broadcasted_iota(jnp.int32, (2 * span, LANES), 0) < span))
    qi = lax.broadcasted_iota(jnp.int32, (2 * span, 2 * span), 0) % span
    ki = lax.broadcasted_iota(jnp.int32, (2 * span, 2 * span), 1)
    band = (ki >= qi) & (ki <= qi + span)
    first = (lax.broadcasted_iota(jnp.int32, (2 * span, span), 1)
             <= lax.broadcasted_iota(jnp.int32, (2 * span, span), 0) % span)

    for g, r in enumerate(B_DILATIONS):
        sub = seq // r
        for rho in range(r):
            rows = pl.ds(rho, sub, stride=r) if r > 1 else pl.ds(0, sub)
            qs[rho * sub:(rho + 1) * sub, :] = qf[rows, :].astype(BF16)
            ks[rho * sub:(rho + 1) * sub, :] = kf[rows, :].astype(BF16)
            vs[rho * sub:(rho + 1) * sub, :] = vf[rows, :].astype(BF16)
        for rho in range(r):
            for n in range(sub // span):
                base = rho * sub + n * span
                qb = qs[base:base + span, :]
                lhs = jnp.where(own, jnp.concatenate([qb, qb], axis=0), jnp.zeros((), BF16))
                start = base - span if n else base
                s = jnp.where(band if n else first, _dot_nt(lhs, ks[start:base + span, :]), NEG_INF)
                m = jnp.max(s, axis=-1, keepdims=True)
                p = jnp.exp2(s - m)
                l = jnp.sum(p, axis=-1, keepdims=True)
                a = jnp.dot(p.astype(BF16), vs[start:base + span, :], preferred_element_type=F32)
                m = jnp.broadcast_to(m, (2 * span, LANES))
                l = jnp.broadcast_to(l, (2 * span, LANES))
                m_g = jnp.where(lo, m[:span], m[span:])
                l_g = jnp.where(lo, l[:span], l[span:])
                a_g = jnp.where(lo, a[:span], a[span:])
                nat = pl.ds(rho + r * span * n, span, stride=r) if r > 1 else pl.ds(n * span, span)
                if g == 0:
                    m_s[nat, :] = m_g
                    l_s[nat, :] = l_g
                    a_s[nat, :] = a_g
                else:
                    m_o = m_s[nat, :]
                    m_n = jnp.maximum(m_o, m_g)
                    w_o = jnp.exp2(m_o - m_n)
                    w_g = jnp.exp2(m_g - m_n)
                    m_s[nat, :] = m_n
                    l_s[nat, :] = w_o * l_s[nat, :] + w_g * l_g
                    a_s[nat, :] = w_o * a_s[nat, :] + w_g * a_g

    o_ref[0] = (a_s[...] / l_s[...]).astype(o_ref.dtype)


def _dilated_attention(h3):
    bsz, seq, _ = h3.shape
    blk = lambda off: pl.BlockSpec((1, seq, LANES), lambda b, p: (b, 0, off // LANES + p))
    return pl.pallas_call(
        functools.partial(_dil_kernel, seq=seq),
        out_shape=jax.ShapeDtypeStruct((bsz, seq, B_HEADS * B_DIM), BF16),
        grid=(bsz, B_HEADS // 2),
        in_specs=[blk(QB), blk(KB), blk(VB)],
        out_specs=pl.BlockSpec((1, seq, LANES), lambda b, p: (b, 0, p)),
        scratch_shapes=([pltpu.VMEM((seq, LANES), F32)] * 3 + [pltpu.VMEM((seq, LANES), BF16)] * 3
                        + [pltpu.VMEM((seq, LANES), F32)] * 3),
        compiler_params=_cparams(("parallel", "parallel")),
        name="dilated_attention",
    )(h3, h3, h3)


def _gla_kernel(q_ref, k_ref, v_ref, gate_ref, rc_ref, wa_ref, ba_ref, gn_ref, rcat_ref, o_ref, st, *, tb):
    t = pl.program_id(2)

    @pl.when(t == 0)
    def _():
        st[...] = jnp.zeros_like(st)

    ch, half, nc = C_CHUNK, C_CHUNK // 2, tb // C_CHUNK
    z = jnp.dot(rc_ref[0], wa_ref[0], preferred_element_type=F32) + ba_ref[0]
    la = (jnp.minimum(z, 0.0) - jnp.log(1.0 + jnp.exp(-jnp.abs(z)))) * (LOG2_E / C_TAU)
    ri = lax.broadcasted_iota(jnp.int32, (LANES, LANES), 0)
    ci = lax.broadcasted_iota(jnp.int32, (LANES, LANES), 1)
    same = (ri // ch) == (ci // ch)
    tri = jnp.where(same & (ci <= ri), 1.0, 0.0).astype(F32)
    blk = jnp.where(same, 1.0, 0.0).astype(F32)
    sums = lambda m: jnp.concatenate(
        [jnp.dot(m, la[r:r + LANES], preferred_element_type=F32, precision=lax.Precision.HIGHEST)
         for r in range(0, tb, LANES)], axis=0)
    b = sums(tri)
    bl = sums(blk)
    qs = q_ref[0].astype(F32) * (C_K ** -0.5)
    kf = k_ref[0].astype(F32)
    vb = v_ref[0]
    qt = (qs * jnp.exp2(b)).astype(BF16)
    kt = kf * jnp.exp2(bl - b)
    gl = jnp.exp2(bl)
    head0 = lax.broadcasted_iota(jnp.int32, (tb, LANES), 1) < C_KP
    kt_h = (jnp.where(head0, kt, 0.0).astype(BF16), jnp.where(head0, 0.0, kt).astype(BF16))

    def chunk_row(x, j, rows):
        xj = x.reshape(nc, ch, LANES)[:, j:j + 1, :]
        return jnp.broadcast_to(xj, (nc, rows, LANES)).reshape(nc * rows, LANES)

    def upper(x):
        return x.reshape(nc, 2, half, LANES)[:, 1].reshape(nc * half, LANES)

    rowi = lax.broadcasted_iota(jnp.int32, (tb, LANES), 0) % ch
    rowu = lax.broadcasted_iota(jnp.int32, (nc * half, LANES), 0) % half + half
    b_u, qs_u = upper(b), upper(qs)
    terms = []
    for j in range(ch):
        if j < half:
            e = jnp.exp2(jnp.where(rowi >= j, b - chunk_row(b, j, ch), NEG_INF))
            term = qs * e * chunk_row(kf, j, ch)
        else:
            e = jnp.exp2(jnp.where(rowu >= j, b_u - chunk_row(b, j, half), NEG_INF))
            tu = (qs_u * e * chunk_row(kf, j, half)).reshape(nc, 1, half, LANES)
            term = jnp.concatenate([jnp.zeros_like(tu), tu], axis=1).reshape(tb, LANES)
        terms.append(term.astype(BF16))
    s_tiled = jnp.dot(jnp.concatenate(terms, axis=1), rcat_ref[...], preferred_element_type=F32)
    rr = lax.broadcasted_iota(jnp.int32, (tb, 2 * LANES), 0)
    cc = lax.broadcasted_iota(jnp.int32, (tb, 2 * LANES), 1)
    m = jnp.where((rr % LANES) // ch == (cc % LANES) // ch, s_tiled, 0.0).astype(BF16)
    o_intra = jnp.concatenate(
        [jnp.concatenate([jnp.dot(m[r:r + LANES, h * LANES:(h + 1) * LANES], vb[r:r + LANES, h * C_VP:(h + 1) * C_VP],
                                  preferred_element_type=F32) for h in range(2)], axis=1)
         for r in range(0, tb, LANES)], axis=0)

    chunks = [slice(c * ch, (c + 1) * ch) for c in range(nc)]
    uts = [jnp.concatenate(
        [lax.dot_general(vb[rows, h * C_VP:(h + 1) * C_VP], kt_h[h][rows], (((0,), (0,)), ((), ())),
                         preferred_element_type=F32) for h in range(2)], axis=0) for rows in chunks]
    s_prev = st[...]
    states = []
    for c in range(nc):
        states.append(s_prev.astype(BF16))
        s_prev = gl[c * ch:c * ch + 1, :] * s_prev + uts[c]
    st[...] = s_prev
    o_inter = [_dot_nt(qt[rows], states[c]) for c, rows in enumerate(chunks)]

    o = o_intra + jnp.concatenate(o_inter, axis=0)
    gate = gate_ref[0].astype(F32)
    outs = []
    for hh in range(2):
        oh = o[:, hh * C_VP:(hh + 1) * C_VP]
        ms = jnp.sum(oh * oh, axis=-1, keepdims=True) * (1.0 / C_V)
        gh = gate[:, hh * C_VP:(hh + 1) * C_VP]
        outs.append(oh * lax.rsqrt(ms + LN_EPS) * gn_ref[...] * (gh * jax.nn.sigmoid(gh)))
    o_ref[0] = jnp.concatenate(outs, axis=1).astype(o_ref.dtype)


def _gla_reduce_matrix():
    r = np.arange(C_CHUNK * LANES)[:, None]
    c = np.arange(2 * LANES)[None, :]
    hit = ((r % LANES) // C_KP == c // LANES) & (r // LANES == (c % LANES) % C_CHUNK)
    return jnp.asarray(hit, dtype=BF16)


def _gla(h3, wa, ba, gn, tb=512):
    bsz, seq, _ = h3.shape
    return pl.pallas_call(
        functools.partial(_gla_kernel, tb=tb),
        out_shape=jax.ShapeDtypeStruct((bsz, seq, C_HEADS * C_VP), BF16),
        grid=(bsz, C_HEADS // 2, seq // tb),
        in_specs=[pl.BlockSpec((1, tb, LANES), lambda b, p, t: (b, t, QC // LANES + p)),
                  pl.BlockSpec((1, tb, LANES), lambda b, p, t: (b, t, KC // LANES + p)),
                  pl.BlockSpec((1, tb, 2 * C_VP), lambda b, p, t: (b, t, VC // (2 * C_VP) + p)),
                  pl.BlockSpec((1, tb, 2 * C_VP), lambda b, p, t: (b, t, GC // (2 * C_VP) + p)),
                  pl.BlockSpec((1, tb, LANES), lambda b, p, t: (b, t, RC // LANES)),
                  pl.BlockSpec((1, LANES, LANES), lambda b, p, t: (p, 0, 0)),
                  pl.BlockSpec((1, 1, LANES), lambda b, p, t: (p, 0, 0)),
                  pl.BlockSpec((1, C_VP), lambda b, p, t: (0, 0)),
                  pl.BlockSpec((C_CHUNK * LANES, 2 * LANES), lambda b, p, t: (0, 0))],
        out_specs=pl.BlockSpec((1, tb, 2 * C_VP), lambda b, p, t: (b, t, p)),
        scratch_shapes=[pltpu.VMEM((2 * C_VP, LANES), F32)],
        compiler_params=_cparams(("parallel", "parallel", "arbitrary")),
        name="gla",
    )(h3, h3, h3, h3, h3, wa, ba, gn, _gla_reduce_matrix())


def _matmul_kernel(a_ref, w_ref, o_ref):
    o_ref[...] = jnp.dot(a_ref[...], w_ref[...], preferred_element_type=F32).astype(o_ref.dtype)


def _matmul(a, w, out_dtype, tm=256, tn=512):
    m, k = a.shape
    n = w.shape[1]
    return pl.pallas_call(
        _matmul_kernel,
        out_shape=jax.ShapeDtypeStruct((m, n), out_dtype),
        grid=(m // tm, n // tn),
        in_specs=[pl.BlockSpec((tm, k), lambda i, j: (i, 0)), pl.BlockSpec((k, tn), lambda i, j: (0, j))],
        out_specs=pl.BlockSpec((tm, tn), lambda i, j: (i, j)),
        compiler_params=_cparams(("parallel", "arbitrary")),
        name="matmul",
    )(a, w)


def _route_top2(x, xh, rwh, rwl):
    xl = (x - xh.astype(F32)).astype(BF16)
    logits = (jnp.dot(xh, rwh, preferred_element_type=F32) + jnp.dot(xh, rwl, preferred_element_type=F32)
              + jnp.dot(xl, rwh, preferred_element_type=F32))
    lane = lax.broadcasted_iota(jnp.int32, logits.shape, 1)
    logits = jnp.where(lane < N_EXPERTS, logits, -jnp.inf)
    v1 = jnp.max(logits, axis=-1, keepdims=True)
    i1 = jnp.min(jnp.where(logits == v1, lane, LANES), axis=-1, keepdims=True)
    rest = jnp.where(lane == i1, -jnp.inf, logits)
    v2 = jnp.max(rest, axis=-1, keepdims=True)
    i2 = jnp.min(jnp.where(rest == v2, lane, LANES), axis=-1, keepdims=True)
    e2 = jnp.exp(v2 - v1)
    w1 = 1.0 / (1.0 + e2)
    w2 = e2 / (1.0 + e2)
    return (jnp.where(lane == 0, w1, 0.0) + jnp.where(lane == 1, w2, 0.0)
            + jnp.where(lane == 2, i1.astype(F32), 0.0) + jnp.where(lane == 3, i2.astype(F32), 0.0))


def _mix_xattn_kernel(oa_ref, ob_ref, oc_ref, wa_ref, wb_ref, wc_ref, x_ref, g1_ref, b1_ref,
                      wq_ref, k_ref, v_ref, wo_ref, g2_ref, b2_ref, *rest, route):
    if route:
        rwh_ref, rwl_ref, y_ref, yb_ref, ys_ref, route_ref = rest
    else:
        y_ref, yb_ref = rest
    acc = jnp.dot(oa_ref[0], wa_ref[...], preferred_element_type=F32)
    acc += jnp.dot(ob_ref[0], wb_ref[...], preferred_element_type=F32)
    acc += jnp.dot(oc_ref[0], wc_ref[...], preferred_element_type=F32)
    x1 = _layer_norm(ALPHA * x_ref[0] + acc, g1_ref[...], b1_ref[...])
    q = jnp.dot(x1.astype(BF16), wq_ref[...], preferred_element_type=F32).astype(BF16)
    outs = []
    for h in range(XA_HEADS):
        cols = slice(h * XA_DIM, (h + 1) * XA_DIM)
        s = _dot_nt(q[:, cols], k_ref[0, :, cols]) * (XA_DIM ** -0.5)
        m = jnp.max(s, axis=-1, keepdims=True)
        p = jnp.exp(s - m)
        l = jnp.sum(p, axis=-1, keepdims=True)
        o = jnp.dot(p.astype(BF16), v_ref[0, :, cols], preferred_element_type=F32)
        outs.append((o / l).astype(BF16))
    xa = jnp.dot(jnp.concatenate(outs, axis=1), wo_ref[...], preferred_element_type=F32)
    y = _layer_norm(ALPHA * x1 + xa, g2_ref[...], b2_ref[...])
    yb = y.astype(BF16)
    y_ref[0] = y
    yb_ref[0] = yb
    if route:
        route_ref[0] = _route_top2(y, yb, rwh_ref[...], rwl_ref[...])
        for s in range(ROW_SLABS):
            _slab(ys_ref, s, y.shape[0])[...] = y[:, s * LANES:(s + 1) * LANES]


def _mix_xattn_ln(oa, ob, oc, wo_parts, x3, ln1, wq, kv3, wo, ln2, router_w=None, tm=512):
    bsz, seq, _ = x3.shape
    mem_len = kv3.shape[1]
    route = router_w is not None
    row = lambda n: pl.BlockSpec((1, tm, n), lambda bi, i: (bi, i, 0))
    full = lambda a: pl.BlockSpec(a.shape, lambda bi, i: (0,) * a.ndim)
    steps = seq // tm
    ins = [oa, ob, oc, *wo_parts, x3, *ln1, wq, kv3, kv3, wo, *ln2]
    in_specs = [row(oa.shape[2]), row(ob.shape[2]), row(oc.shape[2]), *[full(w) for w in wo_parts],
                row(D_MODEL), *[full(v) for v in ln1], full(wq),
                pl.BlockSpec((1, mem_len, D_MODEL), lambda bi, i: (bi, 0, 0)),
                pl.BlockSpec((1, mem_len, D_MODEL), lambda bi, i: (bi, 0, 1)),
                full(wo), *[full(v) for v in ln2]]
    out_shape = [jax.ShapeDtypeStruct((bsz, seq, D_MODEL), F32), jax.ShapeDtypeStruct((bsz, seq, D_MODEL), BF16)]
    out_specs = [row(D_MODEL), row(D_MODEL)]
    if route:
        rwh = router_w.astype(BF16)
        rwl = (router_w - rwh.astype(F32)).astype(BF16)
        ins += [rwh, rwl]
        in_specs += [full(rwh), full(rwl)]
        out_shape += [jax.ShapeDtypeStruct((bsz * seq * ROW_SLABS, LANES), F32),
                      jax.ShapeDtypeStruct((bsz, seq, LANES), F32)]
        out_specs += [pl.BlockSpec((tm * ROW_SLABS, LANES), lambda bi, i: (bi * steps + i, 0)), row(LANES)]
    return pl.pallas_call(
        functools.partial(_mix_xattn_kernel, route=route),
        out_shape=tuple(out_shape),
        grid=(bsz, steps),
        in_specs=in_specs,
        out_specs=tuple(out_specs),
        compiler_params=_cparams(("parallel", "arbitrary")),
        name="mix_xattn_ln",
    )(*ins)


def _swiglu_chunk(xb, w1, w3, w2):
    hg = jnp.dot(xb, w1, preferred_element_type=F32)
    hu = jnp.dot(xb, w3, preferred_element_type=F32)
    a = hg * jax.nn.sigmoid(hg) * hu
    return jnp.dot(a.astype(BF16), w2, preferred_element_type=F32)


def _ffn_kernel(xb_ref, x_ref, w1_ref, w3_ref, w2_ref, g_ref, b_ref, y_ref, yb_ref, acc_ref):
    j = pl.program_id(1)

    @pl.when(j == 0)
    def _():
        acc_ref[...] = jnp.zeros_like(acc_ref)

    acc_ref[...] += _swiglu_chunk(xb_ref[...], w1_ref[...], w3_ref[...], w2_ref[...])

    @pl.when(j == pl.num_programs(1) - 1)
    def _():
        y = _layer_norm(ALPHA * x_ref[...] + acc_ref[...], g_ref[...], b_ref[...])
        y_ref[...] = y
        yb_ref[...] = y.astype(BF16)


def _ffn_ln(xb, x, w13, w2, g, b, tm=512, tf=1408):
    t = x.shape[0]
    nf = w2.shape[0] // tf
    row = pl.BlockSpec((tm, D_MODEL), lambda i, j: (i, 0))
    vec = lambda a: pl.BlockSpec(a.shape, lambda i, j: (0, 0))
    return pl.pallas_call(
        _ffn_kernel,
        out_shape=(jax.ShapeDtypeStruct((t, D_MODEL), F32), jax.ShapeDtypeStruct((t, D_MODEL), BF16)),
        grid=(t // tm, nf),
        in_specs=[row, row,
                  pl.BlockSpec((D_MODEL, tf), lambda i, j: (0, j)),
                  pl.BlockSpec((D_MODEL, tf), lambda i, j: (0, j + nf)),
                  pl.BlockSpec((tf, D_MODEL), lambda i, j: (j, 0)),
                  vec(g), vec(b)],
        out_specs=(row, row),
        scratch_shapes=[pltpu.VMEM((tm, D_MODEL), F32)],
        compiler_params=_cparams(("parallel", "arbitrary")),
        name="ffn_ln",
    )(xb, x, w13, w13, w2, g, b)


def _dispatch_plan(route, tm):
    t = route.shape[0]
    expert = route[:, 2:4].astype(jnp.int32).reshape(-1)
    onehot = (expert[:, None] == jnp.arange(N_EXPERTS, dtype=jnp.int32)[None, :]).astype(jnp.int32)
    csum = jnp.cumsum(onehot, axis=0)
    rank = jnp.sum((csum - 1) * onehot, axis=1)
    counts = csum[-1]
    padded = (counts + tm - 1) // tm * tm
    ends = jnp.cumsum(padded)
    dest = (ends - padded)[expert] + rank
    n_tiles = TOP_K * t // tm + N_EXPERTS
    src = jnp.zeros((n_tiles * tm,), jnp.int32).at[dest].set(jnp.arange(TOP_K * t, dtype=jnp.int32) // TOP_K)
    tile_start = jnp.arange(n_tiles, dtype=jnp.int32) * tm
    tile_expert = jnp.sum((tile_start[:, None] >= ends[None, :]).astype(jnp.int32), axis=1)
    tile_expert = jnp.minimum(tile_expert, N_EXPERTS - 1)
    tile_used = (tile_start < ends[-1]).astype(jnp.int32)
    return src, dest.reshape(t, TOP_K).astype(jnp.int32), tile_expert, tile_used


ROW_SLABS = D_MODEL // LANES


def _slab(ref, s, rows):
    return ref.at[pl.ds(s, rows, stride=ROW_SLABS), :]


def _row_tile(ref, r):
    return ref.at[pl.ds(r * ROW_SLABS, ROW_SLABS), :]


GATHER_PRIORITY = 1


def _moe_ffn_kernel(te_ref, tu_ref, src_ref, nxt_ref, x_hbm, w1_ref, w3_ref, w2_ref, y_ref,
                    xbuf, xs_ref, acc_ref, sem, *, tm):
    i = pl.program_id(0)
    j = pl.program_id(1)
    n_tiles = pl.num_programs(0)
    last = pl.num_programs(1) - 1
    slot = i % 2

    def start_rows(idx_ref, dst_slot):
        def issue(r, carry):
            pltpu.make_async_copy(_row_tile(x_hbm, idx_ref[0, 0, r]), _row_tile(xbuf.at[dst_slot], r),
                                  sem.at[dst_slot]).start(priority=GATHER_PRIORITY)
            return carry
        lax.fori_loop(0, tm, issue, 0, unroll=8)

    @pl.when(j == 0)
    def _():
        @pl.when((i == 0) & (tu_ref[0] > 0))
        def _():
            start_rows(src_ref, 0)

        @pl.when(tu_ref[i] > 0)
        def _():
            pltpu.make_async_copy(x_hbm.at[pl.ds(0, tm * ROW_SLABS), :], xbuf.at[slot], sem.at[slot]).wait()
            for s in range(ROW_SLABS):
                xs_ref[:, s * LANES:(s + 1) * LANES] = _slab(xbuf.at[slot], s, tm)[...].astype(xs_ref.dtype)

        @pl.when((i + 1 < n_tiles) & (tu_ref[jnp.minimum(i + 1, n_tiles - 1)] > 0))
        def _():
            start_rows(nxt_ref, 1 - slot)

        acc_ref[...] = jnp.zeros_like(acc_ref)

    @pl.when(tu_ref[i] > 0)
    def _():
        acc_ref[...] += _swiglu_chunk(xs_ref[...], w1_ref[0], w3_ref[0], w2_ref[0])

    @pl.when(j == last)
    def _():
        for s in range(ROW_SLABS):
            _slab(y_ref, s, tm)[...] = acc_ref[:, s * LANES:(s + 1) * LANES]


def _moe_ffn(tile_expert, tile_used, src, x3, w13, w2, tm, tf):
    n_tiles = src.shape[0]
    nf = w2.shape[1] // tf
    chunk = lambda i, j, tu: jnp.where(tu[i] > 0, j, nf - 1)
    idx = lambda shift: pl.BlockSpec((1, 1, tm), lambda i, j, te, tu: (jnp.minimum(i + shift, n_tiles - 1), 0, 0),
                                     memory_space=pltpu.SMEM)
    return pl.pallas_call(
        functools.partial(_moe_ffn_kernel, tm=tm),
        out_shape=jax.ShapeDtypeStruct((n_tiles * tm * ROW_SLABS, LANES), F32),
        grid_spec=pltpu.PrefetchScalarGridSpec(
            num_scalar_prefetch=2, grid=(n_tiles, nf),
            in_specs=[idx(0), idx(1), pl.BlockSpec(memory_space=pl.ANY),
                      pl.BlockSpec((1, D_MODEL, tf), lambda i, j, te, tu: (te[i], 0, chunk(i, j, tu))),
                      pl.BlockSpec((1, D_MODEL, tf), lambda i, j, te, tu: (te[i], 0, chunk(i, j, tu) + nf)),
                      pl.BlockSpec((1, tf, D_MODEL), lambda i, j, te, tu: (te[i], chunk(i, j, tu), 0))],
            out_specs=pl.BlockSpec((tm * ROW_SLABS, LANES), lambda i, j, te, tu: (i, 0)),
            scratch_shapes=[pltpu.VMEM((2, tm * ROW_SLABS, LANES), F32), pltpu.VMEM((tm, D_MODEL), BF16),
                            pltpu.VMEM((tm, D_MODEL), F32), pltpu.SemaphoreType.DMA((2,))]),
        compiler_params=_cparams(("arbitrary", "arbitrary")),
        name="moe_ffn",
    )(tile_expert, tile_used, src, src, x3, w13, w13, w2)


def _combine_kernel(d0_ref, d1_ref, n0_ref, n1_ref, y_hbm, route_ref, x_ref, g_ref, b_ref, o_ref, ob_ref,
                    buf, sem, *, tm):
    i = pl.program_id(0)
    slot = i % 2

    def start_rows(a_ref, b_ref, dst_slot):
        def issue(r, carry):
            pltpu.make_async_copy(_row_tile(y_hbm, a_ref[0, 0, r]), _row_tile(buf.at[dst_slot, 0], r),
                                  sem.at[dst_slot, 0]).start(priority=0)
            pltpu.make_async_copy(_row_tile(y_hbm, b_ref[0, 0, r]), _row_tile(buf.at[dst_slot, 1], r),
                                  sem.at[dst_slot, 1]).start(priority=1)
            return carry
        lax.fori_loop(0, tm, issue, 0, unroll=8)

    @pl.when(i == 0)
    def _():
        start_rows(d0_ref, d1_ref, 0)

    for k in range(TOP_K):
        pltpu.make_async_copy(y_hbm.at[pl.ds(0, tm * ROW_SLABS), :], buf.at[slot, k], sem.at[slot, k]).wait()

    @pl.when(i + 1 < pl.num_programs(0))
    def _():
        start_rows(n0_ref, n1_ref, 1 - slot)

    route = route_ref[...]
    w0, w1 = route[:, 0:1], route[:, 1:2]
    f = jnp.concatenate([w0 * _slab(buf.at[slot, 0], s, tm)[...] + w1 * _slab(buf.at[slot, 1], s, tm)[...]
                         for s in range(ROW_SLABS)], axis=1)
    y = _layer_norm(ALPHA * x_ref[...] + f, g_ref[...], b_ref[...])
    o_ref[...] = y
    ob_ref[...] = y.astype(BF16)


def _combine_ln(dest, ys, route, x, g, b, tm=512):
    t = x.shape[0]
    steps = t // tm
    row = pl.BlockSpec((tm, D_MODEL), lambda i: (i, 0))
    vec = lambda a: pl.BlockSpec(a.shape, lambda i: (0, 0))
    idx = lambda shift: pl.BlockSpec((1, 1, tm), lambda i: (jnp.minimum(i + shift, steps - 1), 0, 0),
                                     memory_space=pltpu.SMEM)
    d0, d1 = dest[:, 0].reshape(-1, 1, tm), dest[:, 1].reshape(-1, 1, tm)
    return pl.pallas_call(
        functools.partial(_combine_kernel, tm=tm),
        out_shape=(jax.ShapeDtypeStruct((t, D_MODEL), F32), jax.ShapeDtypeStruct((t, D_MODEL), BF16)),
        grid=(steps,),
        in_specs=[idx(0), idx(0), idx(1), idx(1), pl.BlockSpec(memory_space=pl.ANY),
                  pl.BlockSpec((tm, LANES), lambda i: (i, 0)),
                  row, vec(g), vec(b)],
        out_specs=(row, row),
        scratch_shapes=[pltpu.VMEM((2, TOP_K, tm * ROW_SLABS, LANES), F32), pltpu.SemaphoreType.DMA((2, TOP_K))],
        compiler_params=_cparams(("arbitrary",)),
        name="moe_combine_ln",
    )(d0, d1, d0, d1, ys, route, x, g, b)


def _moe_ln(x, x_slab, route, w13, w2, g, b, tm=1024, tf=896):
    src, dest, tile_expert, tile_used = _dispatch_plan(route, tm)
    ys = _moe_ffn(tile_expert, tile_used, src.reshape(-1, 1, tm), x_slab, w13, w2, tm, tf)
    return _combine_ln(dest, ys, route, x, g, b)


def _pad_heads(w, heads, width, padded, axis):
    shape = list(w.shape)
    shape[axis:axis + 1] = [heads, width]
    w = w.reshape(shape)
    pad = [(0, 0)] * w.ndim
    pad[axis + 1] = (0, padded - width)
    w = jnp.pad(w, pad)
    shape[axis:axis + 2] = [heads * padded]
    return w.reshape(shape)


def _layout_w_in(w):
    bounds = np.cumsum([0, 256, 256, 256, 384, 384, 384, 192, 192, 384, 384, 16])
    qa, ka, va, qb, kb, vb, qc, kc, vc, gc, rc = [w[:, bounds[i]:bounds[i + 1]] for i in range(11)]
    rc = jnp.pad(rc, ((0, 0), (0, LANES - C_RANK)))
    qc = _pad_heads(qc, C_HEADS, C_K, C_KP, 1)
    kc = _pad_heads(kc, C_HEADS, C_K, C_KP, 1)
    vc = _pad_heads(vc, C_HEADS, C_V, C_VP, 1)
    gc = _pad_heads(gc, C_HEADS, C_V, C_VP, 1)
    return jnp.concatenate([qa, ka, qb, kb, va, vb, rc, qc, kc, vc, gc], axis=1).astype(BF16)


def _rope_tables(positions, dim):
    rot = dim // ROPE_FRAC
    inv_freq = ROPE_THETA ** (-jnp.arange(0, rot, 2, dtype=F32) / rot)
    ang = positions.astype(F32).reshape(-1, 1) * inv_freq
    cos, sin = jnp.cos(ang), jnp.sin(ang)
    t = ang.shape[0]
    c = jnp.concatenate([cos, cos, jnp.ones((t, dim - rot), F32)], axis=1)
    s = jnp.concatenate([-sin, sin, jnp.zeros((t, dim - rot), F32)], axis=1)
    reps = LANES // dim
    return jnp.tile(c, (1, reps)), jnp.tile(s, (1, reps))


def kernel(x, mem, positions, w_in, lam_q1, lam_k1, lam_q2, lam_k2, diff_norm_g, gla_wa2, gla_ba, gla_norm_g,
           w_out, ln_mix_g, ln_mix_b, xa_wq, xa_wkv, xa_wo, ln_xa_g, ln_xa_b, ffd_w13, ffd_w2, moe_router,
           moe_w13, moe_w2, ln_ffn_g, ln_ffn_b):
    bsz, seq, _ = x.shape
    t = bsz * seq
    ca, sa = _rope_tables(positions, A_QK)
    cb, sb = _rope_tables(positions, B_DIM)
    tabs = (ca, sa, cb, sb)
    memb = mem.reshape(-1, D_MODEL).astype(BF16)
    mem_len = mem.shape[1]
    row = lambda v: v.reshape(1, -1).astype(F32)

    xf = x.reshape(t, D_MODEL)
    xb = xf.astype(BF16)
    for l in range(DEPTH):
        lam_init = 0.8 - 0.6 * math.exp(-0.3 * l)
        lam = (jnp.exp(jnp.sum(lam_q1[l] * lam_k1[l])) - jnp.exp(jnp.sum(lam_q2[l] * lam_k2[l]))
               + lam_init).reshape(1, 1).astype(F32)
        h = _inproj(xb, _layout_w_in(w_in[l]), tabs)
        h3 = h.reshape(bsz, seq, H_COLS)
        oa = _diff_attention(h3, lam, jnp.tile(row(diff_norm_g[l]), (1, 2)), lam_init)
        ob = _dilated_attention(h3)
        wa = _pad_heads(gla_wa2[l], C_HEADS, C_K, C_KP, 1)
        wa = jnp.pad(wa, ((0, LANES - C_RANK), (0, 0))).reshape(LANES, 2, LANES).transpose(1, 0, 2).astype(BF16)
        ba = _pad_heads(gla_ba[l], C_HEADS, C_K, C_KP, 0).reshape(2, 1, LANES).astype(F32)
        gn = jnp.pad(row(gla_norm_g[l]), ((0, 0), (0, C_VP - C_V)))
        oc = _gla(h3, wa, ba, gn)
        wo = w_out[l]
        wo_a = wo[:A_HEADS * A_V].astype(BF16)
        wo_b = wo[A_HEADS * A_V:A_HEADS * A_V + B_HEADS * B_DIM].astype(BF16)
        wo_c = _pad_heads(wo[A_HEADS * A_V + B_HEADS * B_DIM:], C_HEADS, C_V, C_VP, 0).astype(BF16)
        kv = _matmul(memb, xa_wkv[l].astype(BF16), BF16).reshape(bsz, mem_len, 2 * D_MODEL)
        moe = l % 2 == 1
        rw = jnp.pad(moe_router[l // 2], ((0, 0), (0, LANES - N_EXPERTS))) if moe else None
        outs = _mix_xattn_ln(oa, ob, oc, (wo_a, wo_b, wo_c), xf.reshape(bsz, seq, D_MODEL),
                             (row(ln_mix_g[l]), row(ln_mix_b[l])), xa_wq[l].astype(BF16), kv, xa_wo[l].astype(BF16),
                             (row(ln_xa_g[l]), row(ln_xa_b[l])), router_w=rw)
        xf, xb = outs[0].reshape(t, D_MODEL), outs[1].reshape(t, D_MODEL)
        if moe:
            xf, xb = _moe_ln(xf, outs[2], outs[3].reshape(t, LANES), moe_w13[l // 2].astype(BF16),
                             moe_w2[l // 2].astype(BF16), row(ln_ffn_g[l]), row(ln_ffn_b[l]))
        else:
            xf, xb = _ffn_ln(xb, xf, ffd_w13[l // 2].astype(BF16), ffd_w2[l // 2].astype(BF16),
                             row(ln_ffn_g[l]), row(ln_ffn_b[l]))
    return xf.reshape(bsz, seq, D_MODEL)
```

```python
import functools
import math

import numpy as np
import jax
import jax.numpy as jnp
from jax import lax
from jax.experimental import pallas as pl
from jax.experimental.pallas import tpu as pltpu

F32 = jnp.float32
BF16 = jnp.bfloat16

LANES = 128
VMEM_LIMIT_BYTES = 56 * 1024 * 1024

D_MODEL = 1024
DEPTH = 2
ROPE_THETA = 500000.0
ROPE_FRAC = 4
LN_EPS = 1e-5
NEG_INF = -1e30
ALPHA = (2 * DEPTH) ** 0.25
LOG2_E = math.log2(math.e)

A_HEADS, A_QK, A_V = 4, 32, 64
B_HEADS, B_DIM = 6, 64
B_SPAN = 128
B_DILATIONS = (1, 4, 16)
C_HEADS, C_K, C_V = 4, 48, 96
C_KP, C_VP = 64, 128
C_RANK, C_TAU, C_CHUNK = 16, 16.0, 16
XA_HEADS, XA_DIM = 4, 256
N_EXPERTS, TOP_K = 8, 2

QA, KA, QB, KB, VA, VB, RC, QC, KC, VC, GC, H_COLS = (
    0, 256, 512, 896, 1280, 1536, 1920, 2048, 2304, 2560, 3072, 3584)
IN_TN = 256
ROPE_A_TILES = (QB - QA) // IN_TN
ROPE_B_TILES = (VA - QA) // IN_TN


def _cparams(sem):
    return pltpu.CompilerParams(dimension_semantics=sem, vmem_limit_bytes=VMEM_LIMIT_BYTES)


def _layer_norm(y, g, b):
    mu = jnp.mean(y, axis=-1, keepdims=True)
    yc = y - mu
    var = jnp.mean(yc * yc, axis=-1, keepdims=True)
    return yc * lax.rsqrt(var + LN_EPS) * g + b


def _dot_nt(a, b):
    return lax.dot_general(a, b, (((1,), (1,)), ((), ())), preferred_element_type=F32)


def _inproj_kernel(x_ref, w_ref, ca_ref, sa_ref, cb_ref, sb_ref, o_ref):
    x = x_ref[...]
    lane = lax.broadcasted_iota(jnp.int32, (x.shape[0], LANES), 1)

    def rope_cols(lo, k_lo, hi, c_ref, s_ref, half, period):
        acc = jnp.dot(x, w_ref[:, lo:hi], preferred_element_type=F32)
        c, s = c_ref[...], s_ref[...]
        first = lane % period < half
        for k in range((hi - lo) // LANES):
            a = acc[:, k * LANES:(k + 1) * LANES]
            up = pltpu.roll(a, LANES - half, 1)
            dn = pltpu.roll(a, half, 1)
            r = a * c + jnp.where(first, up, dn) * s
            if lo + k * LANES < k_lo:
                r = r * (period ** -0.5 * LOG2_E)
            o_ref[:, lo + k * LANES:lo + (k + 1) * LANES] = r.astype(o_ref.dtype)

    rope_cols(QA, KA, QB, ca_ref, sa_ref, A_QK // ROPE_FRAC // 2, A_QK)
    rope_cols(QB, KB, VA, cb_ref, sb_ref, B_DIM // ROPE_FRAC // 2, B_DIM)
    o_ref[:, VA:] = jnp.dot(x, w_ref[:, VA:], preferred_element_type=F32).astype(o_ref.dtype)


def _inproj(xb, w, tabs, tm=512):
    t = xb.shape[0]
    tab_spec = pl.BlockSpec((tm, LANES), lambda i: (i, 0))
    return pl.pallas_call(
        _inproj_kernel,
        out_shape=jax.ShapeDtypeStruct((t, H_COLS), BF16),
        grid=(t // tm,),
        in_specs=[pl.BlockSpec((tm, D_MODEL), lambda i: (i, 0)),
                  pl.BlockSpec((D_MODEL, H_COLS), lambda i: (0, 0)),
                  tab_spec, tab_spec, tab_spec, tab_spec],
        out_specs=pl.BlockSpec((tm, H_COLS), lambda i: (i, 0)),
        compiler_params=_cparams(("parallel",)),
        name="inproj_rope",
    )(xb, w, *tabs)


def _diff_kernel(lam_ref, q_ref, k_ref, v_ref, g_ref, o_ref, *, seq, tq, out_scale):
    lane = lax.broadcasted_iota(jnp.int32, (tq, LANES), 1)
    row = lax.broadcasted_iota(jnp.int32, (tq, tq), 0)
    col = lax.broadcasted_iota(jnp.int32, (tq, tq), 1)
    causal = col <= row
    lo = lane < A_V
    lam = lam_ref[0, 0]

    for i in range(seq // tq):
        past = i * tq
        q = q_ref[0, past:past + tq, :]
        k_d = k_ref[0, past:past + tq, :]
        v_d = v_ref[0, past:past + tq, :]

        def attend(c):
            qm = jnp.where(lane // A_QK == c, q, jnp.zeros_like(q))
            s_d = jnp.where(causal, _dot_nt(qm, k_d), NEG_INF)
            m = jnp.max(s_d, axis=-1, keepdims=True)
            if past:
                s_p = _dot_nt(qm, k_ref[0, :past, :])
                m = jnp.maximum(m, jnp.max(s_p, axis=-1, keepdims=True))
            p_d = jnp.exp2(s_d - m)
            l = jnp.sum(p_d, axis=-1, keepdims=True)
            acc = jnp.dot(p_d.astype(BF16), v_d, preferred_element_type=F32)
            if past:
                p_p = jnp.exp2(s_p - m)
                l = l + jnp.sum(p_p, axis=-1, keepdims=True)
                acc = acc + jnp.dot(p_p.astype(BF16), v_ref[0, :past, :], preferred_element_type=F32)
            return acc / l

        o_h0 = attend(0) - lam * attend(1)
        o_h1 = attend(2) - lam * attend(3)
        o = jnp.where(lo, o_h0, o_h1)
        sq = o * o
        s_lo = jnp.sum(jnp.where(lo, sq, 0.0), axis=-1, keepdims=True)
        s_all = jnp.sum(sq, axis=-1, keepdims=True)
        ms = jnp.where(lo, s_lo, s_all - s_lo) * (1.0 / A_V)
        o_ref[0, past:past + tq, :] = (o * lax.rsqrt(ms + LN_EPS) * g_ref[...] * out_scale).astype(o_ref.dtype)


def _diff_attention(h3, lam, g, lam_init, tq=512):
    bsz, seq, _ = h3.shape
    kern = functools.partial(_diff_kernel, seq=seq, tq=tq, out_scale=1.0 - lam_init)
    blk = lambda off: pl.BlockSpec((1, seq, LANES), lambda b, p: (b, 0, off // LANES + p))
    return pl.pallas_call(
        kern,
        out_shape=jax.ShapeDtypeStruct((bsz, seq, A_HEADS * A_V), BF16),
        grid=(bsz, A_HEADS // 2),
        in_specs=[pl.BlockSpec(memory_space=pltpu.SMEM), blk(QA), blk(KA), blk(VA),
                  pl.BlockSpec((1, LANES), lambda b, p: (0, 0))],
        out_specs=pl.BlockSpec((1, seq, LANES), lambda b, p: (b, 0, p)),
        compiler_params=_cparams(("parallel", "parallel")),
        name="diff_attention",
    )(lam, h3, h3, h3, g)


def _dil_kernel(q_ref, k_ref, v_ref, o_ref, qf, kf, vf, qs, ks, vs, m_s, l_s, a_s, *, seq):
    qf[...] = q_ref[0].astype(F32)
    kf[...] = k_ref[0].astype(F32)
    vf[...] = v_ref[0].astype(F32)
    span = B_SPAN
    lo = lax.broadcasted_iota(jnp.int32, (span, LANES), 1) < B_DIM
    own = ((lax.broadcasted_iota(jnp.int32, (2 * span, LANES), 1) < B_DIM)
           == (lax.broadcasted_iota(jnp.int32, (2 * span, LANES), 0) < span))
    qi = lax.broadcasted_iota(jnp.int32, (2 * span, 2 * span), 0) % span
    ki = lax.broadcasted_iota(jnp.int32, (2 * span, 2 * span), 1)
    band = (ki >= qi) & (ki <= qi + span)
    first = (lax.broadcasted_iota(jnp.int32, (2 * span, span), 1)
             <= lax.broadcasted_iota(jnp.int32, (2 * span, span), 0) % span)

    for g, r in enumerate(B_DILATIONS):
        sub = seq // r
        for rho in range(r):
            rows = pl.ds(rho, sub, stride=r) if r > 1 else pl.ds(0, sub)
            qs[rho * sub:(rho + 1) * sub, :] = qf[rows, :].astype(BF16)
            ks[rho * sub:(rho + 1) * sub, :] = kf[rows, :].astype(BF16)
            vs[rho * sub:(rho + 1) * sub, :] = vf[rows, :].astype(BF16)
        for rho in range(r):
            for n in range(sub // span):
                base = rho * sub + n * span
                qb = qs[base:base + span, :]
                lhs = jnp.where(own, jnp.concatenate([qb, qb], axis=0), jnp.zeros((), BF16))
                start = base - span if n else base
                s = jnp.where(band if n else first, _dot_nt(lhs, ks[start:base + span, :]), NEG_INF)
                m = jnp.max(s, axis=-1, keepdims=True)
                p = jnp.exp2(s - m)
                l = jnp.sum(p, axis=-1, keepdims=True)
                a = jnp.dot(p.astype(BF16), vs[start:base + span, :], preferred_element_type=F32)
                m = jnp.broadcast_to(m, (2 * span, LANES))
                l = jnp.broadcast_to(l, (2 * span, LANES))
                m_g = jnp.where(lo, m[:span], m[span:])
                l_g = jnp.where(lo, l[:span], l[span:])
                a_g = jnp.where(lo, a[:span], a[span:])
                nat = pl.ds(rho + r * span * n, span, stride=r) if r > 1 else pl.ds(n * span, span)
                if g == 0:
                    m_s[nat, :] = m_g
                    l_s[nat, :] = l_g
                    a_s[nat, :] = a_g
                else:
                    m_o = m_s[nat, :]
                    m_n = jnp.maximum(m_o, m_g)
                    w_o = jnp.exp2(m_o - m_n)
                    w_g = jnp.exp2(m_g - m_n)
                    m_s[nat, :] = m_n
                    l_s[nat, :] = w_o * l_s[nat, :] + w_g * l_g
                    a_s[nat, :] = w_o * a_s[nat, :] + w_g * a_g

    o_ref[0] = (a_s[...] / l_s[...]).astype(o_ref.dtype)


def _dilated_attention(h3):
    bsz, seq, _ = h3.shape
    blk = lambda off: pl.BlockSpec((1, seq, LANES), lambda b, p: (b, 0, off // LANES + p))
    return pl.pallas_call(
        functools.partial(_dil_kernel, seq=seq),
        out_shape=jax.ShapeDtypeStruct((bsz, seq, B_HEADS * B_DIM), BF16),
        grid=(bsz, B_HEADS // 2),
        in_specs=[blk(QB), blk(KB), blk(VB)],
        out_specs=pl.BlockSpec((1, seq, LANES), lambda b, p: (b, 0, p)),
        scratch_shapes=([pltpu.VMEM((seq, LANES), F32)] * 3 + [pltpu.VMEM((seq, LANES), BF16)] * 3
                        + [pltpu.VMEM((seq, LANES), F32)] * 3),
        compiler_params=_cparams(("parallel", "parallel")),
        name="dilated_attention",
    )(h3, h3, h3)


def _gla_kernel(q_ref, k_ref, v_ref, gate_ref, rc_ref, wa_ref, ba_ref, gn_ref, rcat_ref, o_ref, st, *, tb):
    t = pl.program_id(2)

    @pl.when(t == 0)
    def _():
        st[...] = jnp.zeros_like(st)

    ch, half, nc = C_CHUNK, C_CHUNK // 2, tb // C_CHUNK
    z = jnp.dot(rc_ref[0], wa_ref[0], preferred_element_type=F32) + ba_ref[0]
    la = (jnp.minimum(z, 0.0) - jnp.log(1.0 + jnp.exp(-jnp.abs(z)))) * (LOG2_E / C_TAU)
    ri = lax.broadcasted_iota(jnp.int32, (LANES, LANES), 0)
    ci = lax.broadcasted_iota(jnp.int32, (LANES, LANES), 1)
    same = (ri // ch) == (ci // ch)
    tri = jnp.where(same & (ci <= ri), 1.0, 0.0).astype(F32)
    blk = jnp.where(same, 1.0, 0.0).astype(F32)
    sums = lambda m: jnp.concatenate(
        [jnp.dot(m, la[r:r + LANES], preferred_element_type=F32, precision=lax.Precision.HIGHEST)
         for r in range(0, tb, LANES)], axis=0)
    b = sums(tri)
    bl = sums(blk)
    qs = q_ref[0].astype(F32) * (C_K ** -0.5)
    kf = k_ref[0].astype(F32)
    vb = v_ref[0]
    qt = (qs * jnp.exp2(b)).astype(BF16)
    kt = kf * jnp.exp2(bl - b)
    gl = jnp.exp2(bl)
    head0 = lax.broadcasted_iota(jnp.int32, (tb, LANES), 1) < C_KP
    kt_h = (jnp.where(head0, kt, 0.0).astype(BF16), jnp.where(head0, 0.0, kt).astype(BF16))

    def chunk_row(x, j, rows):
        xj = x.reshape(nc, ch, LANES)[:, j:j + 1, :]
        return jnp.broadcast_to(xj, (nc, rows, LANES)).reshape(nc * rows, LANES)

    def upper(x):
        return x.reshape(nc, 2, half, LANES)[:, 1].reshape(nc * half, LANES)

    rowi = lax.broadcasted_iota(jnp.int32, (tb, LANES), 0) % ch
    rowu = lax.broadcasted_iota(jnp.int32, (nc * half, LANES), 0) % half + half
    b_u, qs_u = upper(b), upper(qs)
    terms = []
    for j in range(ch):
        if j < half:
            e = jnp.exp2(jnp.where(rowi >= j, b - chunk_row(b, j, ch), NEG_INF))
            term = qs * e * chunk_row(kf, j, ch)
        else:
            e = jnp.exp2(jnp.where(rowu >= j, b_u - chunk_row(b, j, half), NEG_INF))
            tu = (qs_u * e * chunk_row(kf, j, half)).reshape(nc, 1, half, LANES)
            term = jnp.concatenate([jnp.zeros_like(tu), tu], axis=1).reshape(tb, LANES)
        terms.append(term.astype(BF16))
    s_tiled = jnp.dot(jnp.concatenate(terms, axis=1), rcat_ref[...], preferred_element_type=F32)
    rr = lax.broadcasted_iota(jnp.int32, (tb, 2 * LANES), 0)
    cc = lax.broadcasted_iota(jnp.int32, (tb, 2 * LANES), 1)
    m = jnp.where((rr % LANES) // ch == (cc % LANES) // ch, s_tiled, 0.0).astype(BF16)
    o_intra = jnp.concatenate(
        [jnp.concatenate([jnp.dot(m[r:r + LANES, h * LANES:(h + 1) * LANES], vb[r:r + LANES, h * C_VP:(h + 1) * C_VP],
                                  preferred_element_type=F32) for h in range(2)], axis=1)
         for r in range(0, tb, LANES)], axis=0)

    chunks = [slice(c * ch, (c + 1) * ch) for c in range(nc)]
    uts = [jnp.concatenate(
        [lax.dot_general(vb[rows, h * C_VP:(h + 1) * C_VP], kt_h[h][rows], (((0,), (0,)), ((), ())),
                         preferred_element_type=F32) for h in range(2)], axis=0) for rows in chunks]
    s_prev = st[...]
    states = []
    for c in range(nc):
        states.append(s_prev.astype(BF16))
        s_prev = gl[c * ch:c * ch + 1, :] * s_prev + uts[c]
    st[...] = s_prev
    o_inter = [_dot_nt(qt[rows], states[c]) for c, rows in enumerate(chunks)]

    o = o_intra + jnp.concatenate(o_inter, axis=0)
    gate = gate_ref[0].astype(F32)
    outs = []
    for hh in range(2):
        oh = o[:, hh * C_VP:(hh + 1) * C_VP]
        ms = jnp.sum(oh * oh, axis=-1, keepdims=True) * (1.0 / C_V)
        gh = gate[:, hh * C_VP:(hh + 1) * C_VP]
        outs.append(oh * lax.rsqrt(ms + LN_EPS) * gn_ref[...] * (gh * jax.nn.sigmoid(gh)))
    o_ref[0] = jnp.concatenate(outs, axis=1).astype(o_ref.dtype)


def _gla_reduce_matrix():
    r = np.arange(C_CHUNK * LANES)[:, None]
    c = np.arange(2 * LANES)[None, :]
    hit = ((r % LANES) // C_KP == c // LANES) & (r // LANES == (c % LANES) % C_CHUNK)
    return jnp.asarray(hit, dtype=BF16)


def _gla(h3, wa, ba, gn, tb=512):
    bsz, seq, _ = h3.shape
    return pl.pallas_call(
        functools.partial(_gla_kernel, tb=tb),
        out_shape=jax.ShapeDtypeStruct((bsz, seq, C_HEADS * C_VP), BF16),
        grid=(bsz, C_HEADS // 2, seq // tb),
        in_specs=[pl.BlockSpec((1, tb, LANES), lambda b, p, t: (b, t, QC // LANES + p)),
                  pl.BlockSpec((1, tb, LANES), lambda b, p, t: (b, t, KC // LANES + p)),
                  pl.BlockSpec((1, tb, 2 * C_VP), lambda b, p, t: (b, t, VC // (2 * C_VP) + p)),
                  pl.BlockSpec((1, tb, 2 * C_VP), lambda b, p, t: (b, t, GC // (2 * C_VP) + p)),
                  pl.BlockSpec((1, tb, LANES), lambda b, p, t: (b, t, RC // LANES)),
                  pl.BlockSpec((1, LANES, LANES), lambda b, p, t: (p, 0, 0)),
                  pl.BlockSpec((1, 1, LANES), lambda b, p, t: (p, 0, 0)),
                  pl.BlockSpec((1, C_VP), lambda b, p, t: (0, 0)),
                  pl.BlockSpec((C_CHUNK * LANES, 2 * LANES), lambda b, p, t: (0, 0))],
        out_specs=pl.BlockSpec((1, tb, 2 * C_VP), lambda b, p, t: (b, t, p)),
        scratch_shapes=[pltpu.VMEM((2 * C_VP, LANES), F32)],
        compiler_params=_cparams(("parallel", "parallel", "arbitrary")),
        name="gla",
    )(h3, h3, h3, h3, h3, wa, ba, gn, _gla_reduce_matrix())


def _matmul_kernel(a_ref, w_ref, o_ref):
    o_ref[...] = jnp.dot(a_ref[...], w_ref[...], preferred_element_type=F32).astype(o_ref.dtype)


def _matmul(a, w, out_dtype, tm=256, tn=512):
    m, k = a.shape
    n = w.shape[1]
    return pl.pallas_call(
        _matmul_kernel,
        out_shape=jax.ShapeDtypeStruct((m, n), out_dtype),
        grid=(m // tm, n // tn),
        in_specs=[pl.BlockSpec((tm, k), lambda i, j: (i, 0)), pl.BlockSpec((k, tn), lambda i, j: (0, j))],
        out_specs=pl.BlockSpec((tm, tn), lambda i, j: (i, j)),
        compiler_params=_cparams(("parallel", "arbitrary")),
        name="matmul",
    )(a, w)


def _route_top2(x, xh, rwh, rwl):
    xl = (x - xh.astype(F32)).astype(BF16)
    logits = (jnp.dot(xh, rwh, preferred_element_type=F32) + jnp.dot(xh, rwl, preferred_element_type=F32)
              + jnp.dot(xl, rwh, preferred_element_type=F32))
    lane = lax.broadcasted_iota(jnp.int32, logits.shape, 1)
    logits = jnp.where(lane < N_EXPERTS, logits, -jnp.inf)
    v1 = jnp.max(logits, axis=-1, keepdims=True)
    i1 = jnp.min(jnp.where(logits == v1, lane, LANES), axis=-1, keepdims=True)
    rest = jnp.where(lane == i1, -jnp.inf, logits)
    v2 = jnp.max(rest, axis=-1, keepdims=True)
    i2 = jnp.min(jnp.where(rest == v2, lane, LANES), axis=-1, keepdims=True)
    e2 = jnp.exp(v2 - v1)
    w1 = 1.0 / (1.0 + e2)
    w2 = e2 / (1.0 + e2)
    return (jnp.where(lane == 0, w1, 0.0) + jnp.where(lane == 1, w2, 0.0)
            + jnp.where(lane == 2, i1.astype(F32), 0.0) + jnp.where(lane == 3, i2.astype(F32), 0.0))


def _mix_xattn_kernel(oa_ref, ob_ref, oc_ref, wa_ref, wb_ref, wc_ref, x_ref, g1_ref, b1_ref,
                      wq_ref, k_ref, v_ref, wo_ref, g2_ref, b2_ref, *rest, route):
    if route:
        rwh_ref, rwl_ref, y_ref, yb_ref, ys_ref, route_ref = rest
    else:
        y_ref, yb_ref = rest
    acc = jnp.dot(oa_ref[0], wa_ref[...], preferred_element_type=F32)
    acc += jnp.dot(ob_ref[0], wb_ref[...], preferred_element_type=F32)
    acc += jnp.dot(oc_ref[0], wc_ref[...], preferred_element_type=F32)
    x1 = _layer_norm(ALPHA * x_ref[0] + acc, g1_ref[...], b1_ref[...])
    q = jnp.dot(x1.astype(BF16), wq_ref[...], preferred_element_type=F32).astype(BF16)
    outs = []
    for h in range(XA_HEADS):
        cols = slice(h * XA_DIM, (h + 1) * XA_DIM)
        s = _dot_nt(q[:, cols], k_ref[0, :, cols]) * (XA_DIM ** -0.5)
        m = jnp.max(s, axis=-1, keepdims=True)
        p = jnp.exp(s - m)
        l = jnp.sum(p, axis=-1, keepdims=True)
        o = jnp.dot(p.astype(BF16), v_ref[0, :, cols], preferred_element_type=F32)
        outs.append((o / l).astype(BF16))
    xa = jnp.dot(jnp.concatenate(outs, axis=1), wo_ref[...], preferred_element_type=F32)
    y = _layer_norm(ALPHA * x1 + xa, g2_ref[...], b2_ref[...])
    yb = y.astype(BF16)
    y_ref[0] = y
    yb_ref[0] = yb
    if route:
        route_ref[0] = _route_top2(y, yb, rwh_ref[...], rwl_ref[...])
        for s in range(ROW_SLABS):
            _slab(ys_ref, s, y.shape[0])[...] = y[:, s * LANES:(s + 1) * LANES]


def _mix_xattn_ln(oa, ob, oc, wo_parts, x3, ln1, wq, kv3, wo, ln2, router_w=None, tm=512):
    bsz, seq, _ = x3.shape
    mem_len = kv3.shape[1]
    route = router_w is not None
    row = lambda n: pl.BlockSpec((1, tm, n), lambda bi, i: (bi, i, 0))
    full = lambda a: pl.BlockSpec(a.shape, lambda bi, i: (0,) * a.ndim)
    steps = seq // tm
    ins = [oa, ob, oc, *wo_parts, x3, *ln1, wq, kv3, kv3, wo, *ln2]
    in_specs = [row(oa.shape[2]), row(ob.shape[2]), row(oc.shape[2]), *[full(w) for w in wo_parts],
                row(D_MODEL), *[full(v) for v in ln1], full(wq),
                pl.BlockSpec((1, mem_len, D_MODEL), lambda bi, i: (bi, 0, 0)),
                pl.BlockSpec((1, mem_len, D_MODEL), lambda bi, i: (bi, 0, 1)),
                full(wo), *[full(v) for v in ln2]]
    out_shape = [jax.ShapeDtypeStruct((bsz, seq, D_MODEL), F32), jax.ShapeDtypeStruct((bsz, seq, D_MODEL), BF16)]
    out_specs = [row(D_MODEL), row(D_MODEL)]
    if route:
        rwh = router_w.astype(BF16)
        rwl = (router_w - rwh.astype(F32)).astype(BF16)
        ins += [rwh, rwl]
        in_specs += [full(rwh), full(rwl)]
        out_shape += [jax.ShapeDtypeStruct((bsz * seq * ROW_SLABS, LANES), F32),
                      jax.ShapeDtypeStruct((bsz, seq, LANES), F32)]
        out_specs += [pl.BlockSpec((tm * ROW_SLABS, LANES), lambda bi, i: (bi * steps + i, 0)), row(LANES)]
    return pl.pallas_call(
        functools.partial(_mix_xattn_kernel, route=route),
        out_shape=tuple(out_shape),
        grid=(bsz, steps),
        in_specs=in_specs,
        out_specs=tuple(out_specs),
        compiler_params=_cparams(("parallel", "arbitrary")),
        name="mix_xattn_ln",
    )(*ins)


def _swiglu_chunk(xb, w1, w3, w2):
    hg = jnp.dot(xb, w1, preferred_element_type=F32)
    hu = jnp.dot(xb, w3, preferred_element_type=F32)
    a = hg * jax.nn.sigmoid(hg) * hu
    return jnp.dot(a.astype(BF16), w2, preferred_element_type=F32)


def _ffn_kernel(xb_ref, x_ref, w1_ref, w3_ref, w2_ref, g_ref, b_ref, y_ref, yb_ref, acc_ref):
    j = pl.program_id(1)

    @pl.when(j == 0)
    def _():
        acc_ref[...] = jnp.zeros_like(acc_ref)

    acc_ref[...] += _swiglu_chunk(xb_ref[...], w1_ref[...], w3_ref[...], w2_ref[...])

    @pl.when(j == pl.num_programs(1) - 1)
    def _():
        y = _layer_norm(ALPHA * x_ref[...] + acc_ref[...], g_ref[...], b_ref[...])
        y_ref[...] = y
        yb_ref[...] = y.astype(BF16)


def _ffn_ln(xb, x, w13, w2, g, b, tm=512, tf=1408):
    t = x.shape[0]
    nf = w2.shape[0] // tf
    row = pl.BlockSpec((tm, D_MODEL), lambda i, j: (i, 0))
    vec = lambda a: pl.BlockSpec(a.shape, lambda i, j: (0, 0))
    return pl.pallas_call(
        _ffn_kernel,
        out_shape=(jax.ShapeDtypeStruct((t, D_MODEL), F32), jax.ShapeDtypeStruct((t, D_MODEL), BF16)),
        grid=(t // tm, nf),
        in_specs=[row, row,
                  pl.BlockSpec((D_MODEL, tf), lambda i, j: (0, j)),
                  pl.BlockSpec((D_MODEL, tf), lambda i, j: (0, j + nf)),
                  pl.BlockSpec((tf, D_MODEL), lambda i, j: (j, 0)),
                  vec(g), vec(b)],
        out_specs=(row, row),
        scratch_shapes=[pltpu.VMEM((tm, D_MODEL), F32)],
        compiler_params=_cparams(("parallel", "arbitrary")),
        name="ffn_ln",
    )(xb, x, w13, w13, w2, g, b)


def _dispatch_plan(route, tm):
    t = route.shape[0]
    expert = route[:, 2:4].astype(jnp.int32).reshape(-1)
    onehot = (expert[:, None] == jnp.arange(N_EXPERTS, dtype=jnp.int32)[None, :]).astype(jnp.int32)
    csum = jnp.cumsum(onehot, axis=0)
    rank = jnp.sum((csum - 1) * onehot, axis=1)
    counts = csum[-1]
    padded = (counts + tm - 1) // tm * tm
    ends = jnp.cumsum(padded)
    dest = (ends - padded)[expert] + rank
    n_tiles = TOP_K * t // tm + N_EXPERTS
    src = jnp.zeros((n_tiles * tm,), jnp.int32).at[dest].set(
        jnp.arange(TOP_K * t, dtype=jnp.int32) // TOP_K, unique_indices=True, mode="promise_in_bounds")
    tile_start = jnp.arange(n_tiles, dtype=jnp.int32) * tm
    tile_expert = jnp.sum((tile_start[:, None] >= ends[None, :]).astype(jnp.int32), axis=1)
    tile_expert = jnp.minimum(tile_expert, N_EXPERTS - 1)
    tile_used = (tile_start < ends[-1]).astype(jnp.int32)
    return src, dest.reshape(t, TOP_K).astype(jnp.int32), tile_expert, tile_used


ROW_SLABS = D_MODEL // LANES


def _slab(ref, s, rows):
    return ref.at[pl.ds(s, rows, stride=ROW_SLABS), :]


def _row_tile(ref, r):
    return ref.at[pl.ds(r * ROW_SLABS, ROW_SLABS), :]


GATHER_PRIORITY = 1


def _moe_ffn_kernel(te_ref, tu_ref, src_ref, nxt_ref, x_hbm, w1_ref, w3_ref, w2_ref, y_ref,
                    xbuf, xs_ref, acc_ref, sem, *, tm):
    i = pl.program_id(0)
    j = pl.program_id(1)
    n_tiles = pl.num_programs(0)
    last = pl.num_programs(1) - 1
    slot = i % 2

    def start_rows(idx_ref, dst_slot):
        def issue(r, carry):
            pltpu.make_async_copy(_row_tile(x_hbm, idx_ref[0, 0, r]), _row_tile(xbuf.at[dst_slot], r),
                                  sem.at[dst_slot]).start(priority=GATHER_PRIORITY)
            return carry
        lax.fori_loop(0, tm, issue, 0, unroll=8)

    @pl.when(j == 0)
    def _():
        @pl.when((i == 0) & (tu_ref[0] > 0))
        def _():
            start_rows(src_ref, 0)

        @pl.when(tu_ref[i] > 0)
        def _():
            pltpu.make_async_copy(x_hbm.at[pl.ds(0, tm * ROW_SLABS), :], xbuf.at[slot], sem.at[slot]).wait()
            for s in range(ROW_SLABS):
                xs_ref[:, s * LANES:(s + 1) * LANES] = _slab(xbuf.at[slot], s, tm)[...].astype(xs_ref.dtype)

        @pl.when((i + 1 < n_tiles) & (tu_ref[jnp.minimum(i + 1, n_tiles - 1)] > 0))
        def _():
            start_rows(nxt_ref, 1 - slot)

        acc_ref[...] = jnp.zeros_like(acc_ref)

    @pl.when(tu_ref[i] > 0)
    def _():
        acc_ref[...] += _swiglu_chunk(xs_ref[...], w1_ref[0], w3_ref[0], w2_ref[0])

    @pl.when(j == last)
    def _():
        for s in range(ROW_SLABS):
            _slab(y_ref, s, tm)[...] = acc_ref[:, s * LANES:(s + 1) * LANES]


def _moe_ffn(tile_expert, tile_used, src, x3, w13, w2, tm, tf):
    n_tiles = src.shape[0]
    nf = w2.shape[1] // tf
    chunk = lambda i, j, tu: jnp.where(tu[i] > 0, j, nf - 1)
    idx = lambda shift: pl.BlockSpec((1, 1, tm), lambda i, j, te, tu: (jnp.minimum(i + shift, n_tiles - 1), 0, 0),
                                     memory_space=pltpu.SMEM)
    return pl.pallas_call(
        functools.partial(_moe_ffn_kernel, tm=tm),
        out_shape=jax.ShapeDtypeStruct((n_tiles * tm * ROW_SLABS, LANES), F32),
        grid_spec=pltpu.PrefetchScalarGridSpec(
            num_scalar_prefetch=2, grid=(n_tiles, nf),
            in_specs=[idx(0), idx(1), pl.BlockSpec(memory_space=pl.ANY),
                      pl.BlockSpec((1, D_MODEL, tf), lambda i, j, te, tu: (te[i], 0, chunk(i, j, tu))),
                      pl.BlockSpec((1, D_MODEL, tf), lambda i, j, te, tu: (te[i], 0, chunk(i, j, tu) + nf)),
                      pl.BlockSpec((1, tf, D_MODEL), lambda i, j, te, tu: (te[i], chunk(i, j, tu), 0))],
            out_specs=pl.BlockSpec((tm * ROW_SLABS, LANES), lambda i, j, te, tu: (i, 0)),
            scratch_shapes=[pltpu.VMEM((2, tm * ROW_SLABS, LANES), F32), pltpu.VMEM((tm, D_MODEL), BF16),
                            pltpu.VMEM((tm, D_MODEL), F32), pltpu.SemaphoreType.DMA((2,))]),
        compiler_params=_cparams(("arbitrary", "arbitrary")),
        name="moe_ffn",
    )(tile_expert, tile_used, src, src, x3, w13, w13, w2)


def _combine_kernel(d0_ref, d1_ref, n0_ref, n1_ref, y_hbm, route_ref, x_ref, g_ref, b_ref, o_ref, ob_ref,
                    buf, sem, *, tm):
    i = pl.program_id(0)
    slot = i % 2

    def start_rows(a_ref, b_ref, dst_slot):
        def issue(r, carry):
            pltpu.make_async_copy(_row_tile(y_hbm, a_ref[0, 0, r]), _row_tile(buf.at[dst_slot, 0], r),
                                  sem.at[dst_slot, 0]).start(priority=0)
            pltpu.make_async_copy(_row_tile(y_hbm, b_ref[0, 0, r]), _row_tile(buf.at[dst_slot, 1], r),
                                  sem.at[dst_slot, 1]).start(priority=1)
            return carry
        lax.fori_loop(0, tm, issue, 0, unroll=8)

    @pl.when(i == 0)
    def _():
        start_rows(d0_ref, d1_ref, 0)

    for k in range(TOP_K):
        pltpu.make_async_copy(y_hbm.at[pl.ds(0, tm * ROW_SLABS), :], buf.at[slot, k], sem.at[slot, k]).wait()

    @pl.when(i + 1 < pl.num_programs(0))
    def _():
        start_rows(n0_ref, n1_ref, 1 - slot)

    route = route_ref[...]
    w0, w1 = route[:, 0:1], route[:, 1:2]
    f = jnp.concatenate([w0 * _slab(buf.at[slot, 0], s, tm)[...] + w1 * _slab(buf.at[slot, 1], s, tm)[...]
                         for s in range(ROW_SLABS)], axis=1)
    y = _layer_norm(ALPHA * x_ref[...] + f, g_ref[...], b_ref[...])
    o_ref[...] = y
    ob_ref[...] = y.astype(BF16)


def _combine_ln(dest, ys, route, x, g, b, tm=512):
    t = x.shape[0]
    steps = t // tm
    row = pl.BlockSpec((tm, D_MODEL), lambda i: (i, 0))
    vec = lambda a: pl.BlockSpec(a.shape, lambda i: (0, 0))
    idx = lambda shift: pl.BlockSpec((1, 1, tm), lambda i: (jnp.minimum(i + shift, steps - 1), 0, 0),
                                     memory_space=pltpu.SMEM)
    d0, d1 = dest[:, 0].reshape(-1, 1, tm), dest[:, 1].reshape(-1, 1, tm)
    return pl.pallas_call(
        functools.partial(_combine_kernel, tm=tm),
        out_shape=(jax.ShapeDtypeStruct((t, D_MODEL), F32), jax.ShapeDtypeStruct((t, D_MODEL), BF16)),
        grid=(steps,),
        in_specs=[idx(0), idx(0), idx(1), idx(1), pl.BlockSpec(memory_space=pl.ANY),
                  pl.BlockSpec((tm, LANES), lambda i: (i, 0)),
                  row, vec(g), vec(b)],
        out_specs=(row, row),
        scratch_shapes=[pltpu.VMEM((2, TOP_K, tm * ROW_SLABS, LANES), F32), pltpu.SemaphoreType.DMA((2, TOP_K))],
        compiler_params=_cparams(("arbitrary",)),
        name="moe_combine_ln",
    )(d0, d1, d0, d1, ys, route, x, g, b)


def _moe_ln(x, x_slab, route, w13, w2, g, b, tm=512, tf=1792):
    src, dest, tile_expert, tile_used = _dispatch_plan(route, tm)
    ys = _moe_ffn(tile_expert, tile_used, src.reshape(-1, 1, tm), x_slab, w13, w2, tm, tf)
    return _combine_ln(dest, ys, route, x, g, b)


def _pad_heads(w, heads, width, padded, axis):
    shape = list(w.shape)
    shape[axis:axis + 1] = [heads, width]
    w = w.reshape(shape)
    pad = [(0, 0)] * w.ndim
    pad[axis + 1] = (0, padded - width)
    w = jnp.pad(w, pad)
    shape[axis:axis + 2] = [heads * padded]
    return w.reshape(shape)


def _layout_w_in(w):
    bounds = np.cumsum([0, 256, 256, 256, 384, 384, 384, 192, 192, 384, 384, 16])
    qa, ka, va, qb, kb, vb, qc, kc, vc, gc, rc = [w[:, bounds[i]:bounds[i + 1]] for i in range(11)]
    rc = jnp.pad(rc, ((0, 0), (0, LANES - C_RANK)))
    qc = _pad_heads(qc, C_HEADS, C_K, C_KP, 1)
    kc = _pad_heads(kc, C_HEADS, C_K, C_KP, 1)
    vc = _pad_heads(vc, C_HEADS, C_V, C_VP, 1)
    gc = _pad_heads(gc, C_HEADS, C_V, C_VP, 1)
    return jnp.concatenate([qa, ka, qb, kb, va, vb, rc, qc, kc, vc, gc], axis=1).astype(BF16)


def _rope_tables(positions, dim):
    rot = dim // ROPE_FRAC
    inv_freq = ROPE_THETA ** (-jnp.arange(0, rot, 2, dtype=F32) / rot)
    ang = positions.astype(F32).reshape(-1, 1) * inv_freq
    cos, sin = jnp.cos(ang), jnp.sin(ang)
    t = ang.shape[0]
    c = jnp.concatenate([cos, cos, jnp.ones((t, dim - rot), F32)], axis=1)
    s = jnp.concatenate([-sin, sin, jnp.zeros((t, dim - rot), F32)], axis=1)
    reps = LANES // dim
    return jnp.tile(c, (1, reps)), jnp.tile(s, (1, reps))


def kernel(x, mem, positions, w_in, lam_q1, lam_k1, lam_q2, lam_k2, diff_norm_g, gla_wa2, gla_ba, gla_norm_g,
           w_out, ln_mix_g, ln_mix_b, xa_wq, xa_wkv, xa_wo, ln_xa_g, ln_xa_b, ffd_w13, ffd_w2, moe_router,
           moe_w13, moe_w2, ln_ffn_g, ln_ffn_b):
    bsz, seq, _ = x.shape
    t = bsz * seq
    ca, sa = _rope_tables(positions, A_QK)
    cb, sb = _rope_tables(positions, B_DIM)
    tabs = (ca, sa, cb, sb)
    memb = mem.reshape(-1, D_MODEL).astype(BF16)
    mem_len = mem.shape[1]
    row = lambda v: v.reshape(1, -1).astype(F32)

    xf = x.reshape(t, D_MODEL)
    xb = xf.astype(BF16)
    for l in range(DEPTH):
        lam_init = 0.8 - 0.6 * math.exp(-0.3 * l)
        lam = (jnp.exp(jnp.sum(lam_q1[l] * lam_k1[l])) - jnp.exp(jnp.sum(lam_q2[l] * lam_k2[l]))
               + lam_init).reshape(1, 1).astype(F32)
        h = _inproj(xb, _layout_w_in(w_in[l]), tabs)
        h3 = h.reshape(bsz, seq, H_COLS)
        oa = _diff_attention(h3, lam, jnp.tile(row(diff_norm_g[l]), (1, 2)), lam_init)
        ob = _dilated_attention(h3)
        wa = _pad_heads(gla_wa2[l], C_HEADS, C_K, C_KP, 1)
        wa = jnp.pad(wa, ((0, LANES - C_RANK), (0, 0))).reshape(LANES, 2, LANES).transpose(1, 0, 2).astype(BF16)
        ba = _pad_heads(gla_ba[l], C_HEADS, C_K, C_KP, 0).reshape(2, 1, LANES).astype(F32)
        gn = jnp.pad(row(gla_norm_g[l]), ((0, 0), (0, C_VP - C_V)))
        oc = _gla(h3, wa, ba, gn)
        wo = w_out[l]
        wo_a = wo[:A_HEADS * A_V].astype(BF16)
        wo_b = wo[A_HEADS * A_V:A_HEADS * A_V + B_HEADS * B_DIM].astype(BF16)
        wo_c = _pad_heads(wo[A_HEADS * A_V + B_HEADS * B_DIM:], C_HEADS, C_V, C_VP, 0).astype(BF16)
        kv = _matmul(memb, xa_wkv[l].astype(BF16), BF16).reshape(bsz, mem_len, 2 * D_MODEL)
        moe = l % 2 == 1
        rw = jnp.pad(moe_router[l // 2], ((0, 0), (0, LANES - N_EXPERTS))) if moe else None
        outs = _mix_xattn_ln(oa, ob, oc, (wo_a, wo_b, wo_c), xf.reshape(bsz, seq, D_MODEL),
                             (row(ln_mix_g[l]), row(ln_mix_b[l])), xa_wq[l].astype(BF16), kv, xa_wo[l].astype(BF16),
                             (row(ln_xa_g[l]), row(ln_xa_b[l])), router_w=rw)
        xf, xb = outs[0].reshape(t, D_MODEL), outs[1].reshape(t, D_MODEL)
        if moe:
            xf, xb = _moe_ln(xf, outs[2], outs[3].reshape(t, LANES), moe_w13[l // 2].astype(BF16),
                             moe_w2[l // 2].astype(BF16), row(ln_ffn_g[l]), row(ln_ffn_b[l]))
        else:
            xf, xb = _ffn_ln(xb, xf, ffd_w13[l // 2].astype(BF16), ffd_w2[l // 2].astype(BF16),
                             row(ln_ffn_g[l]), row(ln_ffn_b[l]))
    return xf.reshape(bsz, seq, D_MODEL)
```

```python
import functools
import math

import numpy as np
import jax
import jax.numpy as jnp
from jax import lax
from jax.experimental import pallas as pl
from jax.experimental.pallas import tpu as pltpu

F32 = jnp.float32
BF16 = jnp.bfloat16

LANES = 128
VMEM_LIMIT_BYTES = 56 * 1024 * 1024

D_MODEL = 1024
DEPTH = 2
ROPE_THETA = 500000.0
ROPE_FRAC = 4
LN_EPS = 1e-5
NEG_INF = -1e30
ALPHA = (2 * DEPTH) ** 0.25
LOG2_E = math.log2(math.e)

A_HEADS, A_QK, A_V = 4, 32, 64
B_HEADS, B_DIM = 6, 64
B_SPAN = 128
B_DILATIONS = (1, 4, 16)
C_HEADS, C_K, C_V = 4, 48, 96
C_KP, C_VP = 64, 128
C_RANK, C_TAU, C_CHUNK = 16, 16.0, 16
XA_HEADS, XA_DIM = 4, 256
N_EXPERTS, TOP_K = 8, 2

QA, KA, QB, KB, VA, VB, RC, QC, KC, VC, GC, H_COLS = (
    0, 256, 512, 896, 1280, 1536, 1920, 2048, 2304, 2560, 3072, 3584)
IN_TN = 256
ROPE_A_TILES = (QB - QA) // IN_TN
ROPE_B_TILES = (VA - QA) // IN_TN


def _cparams(sem):
    return pltpu.CompilerParams(dimension_semantics=sem, vmem_limit_bytes=VMEM_LIMIT_BYTES)


def _layer_norm(y, g, b):
    mu = jnp.mean(y, axis=-1, keepdims=True)
    yc = y - mu
    var = jnp.mean(yc * yc, axis=-1, keepdims=True)
    return yc * lax.rsqrt(var + LN_EPS) * g + b


def _dot_nt(a, b):
    return lax.dot_general(a, b, (((1,), (1,)), ((), ())), preferred_element_type=F32)


def _inproj_kernel(x_ref, w_ref, ca_ref, sa_ref, cb_ref, sb_ref, o_ref):
    x = x_ref[...]
    lane = lax.broadcasted_iota(jnp.int32, (x.shape[0], LANES), 1)

    def rope_cols(lo, k_lo, hi, c_ref, s_ref, half, period):
        acc = jnp.dot(x, w_ref[:, lo:hi], preferred_element_type=F32)
        c, s = c_ref[...], s_ref[...]
        first = lane % period < half
        for k in range((hi - lo) // LANES):
            a = acc[:, k * LANES:(k + 1) * LANES]
            up = pltpu.roll(a, LANES - half, 1)
            dn = pltpu.roll(a, half, 1)
            r = a * c + jnp.where(first, up, dn) * s
            if lo + k * LANES < k_lo:
                r = r * (period ** -0.5 * LOG2_E)
            o_ref[:, lo + k * LANES:lo + (k + 1) * LANES] = r.astype(o_ref.dtype)

    rope_cols(QA, KA, QB, ca_ref, sa_ref, A_QK // ROPE_FRAC // 2, A_QK)
    rope_cols(QB, KB, VA, cb_ref, sb_ref, B_DIM // ROPE_FRAC // 2, B_DIM)
    o_ref[:, VA:] = jnp.dot(x, w_ref[:, VA:], preferred_element_type=F32).astype(o_ref.dtype)


def _inproj(xb, w, tabs, tm=512):
    t = xb.shape[0]
    tab_spec = pl.BlockSpec((tm, LANES), lambda i: (i, 0))
    return pl.pallas_call(
        _inproj_kernel,
        out_shape=jax.ShapeDtypeStruct((t, H_COLS), BF16),
        grid=(t // tm,),
        in_specs=[pl.BlockSpec((tm, D_MODEL), lambda i: (i, 0)),
                  pl.BlockSpec((D_MODEL, H_COLS), lambda i: (0, 0)),
                  tab_spec, tab_spec, tab_spec, tab_spec],
        out_specs=pl.BlockSpec((tm, H_COLS), lambda i: (i, 0)),
        compiler_params=_cparams(("parallel",)),
        name="inproj_rope",
    )(xb, w, *tabs)


def _diff_kernel(lam_ref, q_ref, k_ref, v_ref, g_ref, o_ref, *, seq, tq, out_scale):
    lane = lax.broadcasted_iota(jnp.int32, (tq, LANES), 1)
    row = lax.broadcasted_iota(jnp.int32, (tq, tq), 0)
    col = lax.broadcasted_iota(jnp.int32, (tq, tq), 1)
    causal = col <= row
    lo = lane < A_V
    lam = lam_ref[0, 0]

    for i in range(seq // tq):
        past = i * tq
        q = q_ref[0, past:past + tq, :]
        k_d = k_ref[0, past:past + tq, :]
        v_d = v_ref[0, past:past + tq, :]

        def attend(c):
            qm = jnp.where(lane // A_QK == c, q, jnp.zeros_like(q))
            s_d = jnp.where(causal, _dot_nt(qm, k_d), NEG_INF)
            m = jnp.max(s_d, axis=-1, keepdims=True)
            if past:
                s_p = _dot_nt(qm, k_ref[0, :past, :])
                m = jnp.maximum(m, jnp.max(s_p, axis=-1, keepdims=True))
            p_d = jnp.exp2(s_d - m)
            l = jnp.sum(p_d, axis=-1, keepdims=True)
            acc = jnp.dot(p_d.astype(BF16), v_d, preferred_element_type=F32)
            if past:
                p_p = jnp.exp2(s_p - m)
                l = l + jnp.sum(p_p, axis=-1, keepdims=True)
                acc = acc + jnp.dot(p_p.astype(BF16), v_ref[0, :past, :], preferred_element_type=F32)
            return acc / l

        o_h0 = attend(0) - lam * attend(1)
        o_h1 = attend(2) - lam * attend(3)
        o = jnp.where(lo, o_h0, o_h1)
        sq = o * o
        s_lo = jnp.sum(jnp.where(lo, sq, 0.0), axis=-1, keepdims=True)
        s_all = jnp.sum(sq, axis=-1, keepdims=True)
        ms = jnp.where(lo, s_lo, s_all - s_lo) * (1.0 / A_V)
        o_ref[0, past:past + tq, :] = (o * lax.rsqrt(ms + LN_EPS) * g_ref[...] * out_scale).astype(o_ref.dtype)


def _diff_attention(h3, lam, g, lam_init, tq=512):
    bsz, seq, _ = h3.shape
    kern = functools.partial(_diff_kernel, seq=seq, tq=tq, out_scale=1.0 - lam_init)
    blk = lambda off: pl.BlockSpec((1, seq, LANES), lambda b, p: (b, 0, off // LANES + p))
    return pl.pallas_call(
        kern,
        out_shape=jax.ShapeDtypeStruct((bsz, seq, A_HEADS * A_V), BF16),
        grid=(bsz, A_HEADS // 2),
        in_specs=[pl.BlockSpec(memory_space=pltpu.SMEM), blk(QA), blk(KA), blk(VA),
                  pl.BlockSpec((1, LANES), lambda b, p: (0, 0))],
        out_specs=pl.BlockSpec((1, seq, LANES), lambda b, p: (b, 0, p)),
        compiler_params=_cparams(("parallel", "parallel")),
        name="diff_attention",
    )(lam, h3, h3, h3, g)


def _dil_kernel(q_ref, k_ref, v_ref, o_ref, qf, kf, vf, qs, ks, vs, m_s, l_s, a_s, *, seq):
    qf[...] = q_ref[0].astype(F32)
    kf[...] = k_ref[0].astype(F32)
    vf[...] = v_ref[0].astype(F32)
    span = B_SPAN
    lo = lax.broadcasted_iota(jnp.int32, (span, LANES), 1) < B_DIM
    own = ((lax.broadcasted_iota(jnp.int32, (2 * span, LANES), 1) < B_DIM)
           == (lax.broadcasted_iota(jnp.int32, (2 * span, LANES), 0) < span))
    qi = lax.broadcasted_iota(jnp.int32, (2 * span, 2 * span), 0) % span
    ki = lax.broadcasted_iota(jnp.int32, (2 * span, 2 * span), 1)
    band = (ki >= qi) & (ki <= qi + span)
    first = (lax.broadcasted_iota(jnp.int32, (2 * span, span), 1)
             <= lax.broadcasted_iota(jnp.int32, (2 * span, span), 0) % span)

    for g, r in enumerate(B_DILATIONS):
        sub = seq // r
        for rho in range(r):
            rows = pl.ds(rho, sub, stride=r) if r > 1 else pl.ds(0, sub)
            qs[rho * sub:(rho + 1) * sub, :] = qf[rows, :].astype(BF16)
            ks[rho * sub:(rho + 1) * sub, :] = kf[rows, :].astype(BF16)
            vs[rho * sub:(rho + 1) * sub, :] = vf[rows, :].astype(BF16)
        for rho in range(r):
            for n in range(sub // span):
                base = rho * sub + n * span
                qb = qs[base:base + span, :]
                lhs = jnp.where(own, jnp.concatenate([qb, qb], axis=0), jnp.zeros((), BF16))
                start = base - span if n else base
                s = jnp.where(band if n else first, _dot_nt(lhs, ks[start:base + span, :]), NEG_INF)
                m = jnp.max(s, axis=-1, keepdims=True)
                p = jnp.exp2(s - m)
                l = jnp.sum(p, axis=-1, keepdims=True)
                a = jnp.dot(p.astype(BF16), vs[start:base + span, :], preferred_element_type=F32)
                m = jnp.broadcast_to(m, (2 * span, LANES))
                l = jnp.broadcast_to(l, (2 * span, LANES))
                m_g = jnp.where(lo, m[:span], m[span:])
                l_g = jnp.where(lo, l[:span], l[span:])
                a_g = jnp.where(lo, a[:span], a[span:])
                nat = pl.ds(rho + r * span * n, span, stride=r) if r > 1 else pl.ds(n * span, span)
                if g == 0:
                    m_s[nat, :] = m_g
                    l_s[nat, :] = l_g
                    a_s[nat, :] = a_g
                else:
                    m_o = m_s[nat, :]
                    m_n = jnp.maximum(m_o, m_g)
                    w_o = jnp.exp2(m_o - m_n)
                    w_g = jnp.exp2(m_g - m_n)
                    m_s[nat, :] = m_n
                    l_s[nat, :] = w_o * l_s[nat, :] + w_g * l_g
                    a_s[nat, :] = w_o * a_s[nat, :] + w_g * a_g

    o_ref[0] = (a_s[...] / l_s[...]).astype(o_ref.dtype)


def _dilated_attention(h3):
    bsz, seq, _ = h3.shape
    blk = lambda off: pl.BlockSpec((1, seq, LANES), lambda b, p: (b, 0, off // LANES + p))
    return pl.pallas_call(
        functools.partial(_dil_kernel, seq=seq),
        out_shape=jax.ShapeDtypeStruct((bsz, seq, B_HEADS * B_DIM), BF16),
        grid=(bsz, B_HEADS // 2),
        in_specs=[blk(QB), blk(KB), blk(VB)],
        out_specs=pl.BlockSpec((1, seq, LANES), lambda b, p: (b, 0, p)),
        scratch_shapes=([pltpu.VMEM((seq, LANES), F32)] * 3 + [pltpu.VMEM((seq, LANES), BF16)] * 3
                        + [pltpu.VMEM((seq, LANES), F32)] * 3),
        compiler_params=_cparams(("parallel", "parallel")),
        name="dilated_attention",
    )(h3, h3, h3)


def _gla_kernel(q_ref, k_ref, v_ref, gate_ref, rc_ref, wa_ref, ba_ref, gn_ref, rcat_ref, o_ref, st, *, tb):
    t = pl.program_id(2)

    @pl.when(t == 0)
    def _():
        st[...] = jnp.zeros_like(st)

    ch, half, nc = C_CHUNK, C_CHUNK // 2, tb // C_CHUNK
    z = jnp.dot(rc_ref[0], wa_ref[0], preferred_element_type=F32) + ba_ref[0]
    la = (jnp.minimum(z, 0.0) - jnp.log(1.0 + jnp.exp(-jnp.abs(z)))) * (LOG2_E / C_TAU)
    ri = lax.broadcasted_iota(jnp.int32, (LANES, LANES), 0)
    ci = lax.broadcasted_iota(jnp.int32, (LANES, LANES), 1)
    same = (ri // ch) == (ci // ch)
    tri = jnp.where(same & (ci <= ri), 1.0, 0.0).astype(F32)
    blk = jnp.where(same, 1.0, 0.0).astype(F32)
    sums = lambda m: jnp.concatenate(
        [jnp.dot(m, la[r:r + LANES], preferred_element_type=F32, precision=lax.Precision.HIGHEST)
         for r in range(0, tb, LANES)], axis=0)
    b = sums(tri)
    bl = sums(blk)
    qs = q_ref[0].astype(F32) * (C_K ** -0.5)
    kf = k_ref[0].astype(F32)
    vb = v_ref[0]
    qt = (qs * jnp.exp2(b)).astype(BF16)
    kt = kf * jnp.exp2(bl - b)
    gl = jnp.exp2(bl)
    head0 = lax.broadcasted_iota(jnp.int32, (tb, LANES), 1) < C_KP
    kt_h = (jnp.where(head0, kt, 0.0).astype(BF16), jnp.where(head0, 0.0, kt).astype(BF16))

    def chunk_row(x, j, rows):
        xj = x.reshape(nc, ch, LANES)[:, j:j + 1, :]
        return jnp.broadcast_to(xj, (nc, rows, LANES)).reshape(nc * rows, LANES)

    def upper(x):
        return x.reshape(nc, 2, half, LANES)[:, 1].reshape(nc * half, LANES)

    rowi = lax.broadcasted_iota(jnp.int32, (tb, LANES), 0) % ch
    rowu = lax.broadcasted_iota(jnp.int32, (nc * half, LANES), 0) % half + half
    b_u, qs_u = upper(b), upper(qs)
    terms = []
    for j in range(ch):
        if j < half:
            e = jnp.exp2(jnp.where(rowi >= j, b - chunk_row(b, j, ch), NEG_INF))
            term = qs * e * chunk_row(kf, j, ch)
        else:
            e = jnp.exp2(jnp.where(rowu >= j, b_u - chunk_row(b, j, half), NEG_INF))
            tu = (qs_u * e * chunk_row(kf, j, half)).reshape(nc, 1, half, LANES)
            term = jnp.concatenate([jnp.zeros_like(tu), tu], axis=1).reshape(tb, LANES)
        terms.append(term.astype(BF16))
    s_tiled = jnp.dot(jnp.concatenate(terms, axis=1), rcat_ref[...], preferred_element_type=F32)
    rr = lax.broadcasted_iota(jnp.int32, (tb, 2 * LANES), 0)
    cc = lax.broadcasted_iota(jnp.int32, (tb, 2 * LANES), 1)
    m = jnp.where((rr % LANES) // ch == (cc % LANES) // ch, s_tiled, 0.0).astype(BF16)
    o_intra = jnp.concatenate(
        [jnp.concatenate([jnp.dot(m[r:r + LANES, h * LANES:(h + 1) * LANES], vb[r:r + LANES, h * C_VP:(h + 1) * C_VP],
                                  preferred_element_type=F32) for h in range(2)], axis=1)
         for r in range(0, tb, LANES)], axis=0)

    chunks = [slice(c * ch, (c + 1) * ch) for c in range(nc)]
    uts = [jnp.concatenate(
        [lax.dot_general(vb[rows, h * C_VP:(h + 1) * C_VP], kt_h[h][rows], (((0,), (0,)), ((), ())),
                         preferred_element_type=F32) for h in range(2)], axis=0) for rows in chunks]
    s_prev = st[...]
    states = []
    for c in range(nc):
        states.append(s_prev.astype(BF16))
        s_prev = gl[c * ch:c * ch + 1, :] * s_prev + uts[c]
    st[...] = s_prev
    o_inter = [_dot_nt(qt[rows], states[c]) for c, rows in enumerate(chunks)]

    o = o_intra + jnp.concatenate(o_inter, axis=0)
    gate = gate_ref[0].astype(F32)
    outs = []
    for hh in range(2):
        oh = o[:, hh * C_VP:(hh + 1) * C_VP]
        ms = jnp.sum(oh * oh, axis=-1, keepdims=True) * (1.0 / C_V)
        gh = gate[:, hh * C_VP:(hh + 1) * C_VP]
        outs.append(oh * lax.rsqrt(ms + LN_EPS) * gn_ref[...] * (gh * jax.nn.sigmoid(gh)))
    o_ref[0] = jnp.concatenate(outs, axis=1).astype(o_ref.dtype)


def _gla_reduce_matrix():
    r = np.arange(C_CHUNK * LANES)[:, None]
    c = np.arange(2 * LANES)[None, :]
    hit = ((r % LANES) // C_KP == c // LANES) & (r // LANES == (c % LANES) % C_CHUNK)
    return jnp.asarray(hit, dtype=BF16)


def _gla(h3, wa, ba, gn, tb=1024):
    bsz, seq, _ = h3.shape
    return pl.pallas_call(
        functools.partial(_gla_kernel, tb=tb),
        out_shape=jax.ShapeDtypeStruct((bsz, seq, C_HEADS * C_VP), BF16),
        grid=(bsz, C_HEADS // 2, seq // tb),
        in_specs=[pl.BlockSpec((1, tb, LANES), lambda b, p, t: (b, t, QC // LANES + p)),
                  pl.BlockSpec((1, tb, LANES), lambda b, p, t: (b, t, KC // LANES + p)),
                  pl.BlockSpec((1, tb, 2 * C_VP), lambda b, p, t: (b, t, VC // (2 * C_VP) + p)),
                  pl.BlockSpec((1, tb, 2 * C_VP), lambda b, p, t: (b, t, GC // (2 * C_VP) + p)),
                  pl.BlockSpec((1, tb, LANES), lambda b, p, t: (b, t, RC // LANES)),
                  pl.BlockSpec((1, LANES, LANES), lambda b, p, t: (p, 0, 0)),
                  pl.BlockSpec((1, 1, LANES), lambda b, p, t: (p, 0, 0)),
                  pl.BlockSpec((1, C_VP), lambda b, p, t: (0, 0)),
                  pl.BlockSpec((C_CHUNK * LANES, 2 * LANES), lambda b, p, t: (0, 0))],
        out_specs=pl.BlockSpec((1, tb, 2 * C_VP), lambda b, p, t: (b, t, p)),
        scratch_shapes=[pltpu.VMEM((2 * C_VP, LANES), F32)],
        compiler_params=_cparams(("parallel", "parallel", "arbitrary")),
        name="gla",
    )(h3, h3, h3, h3, h3, wa, ba, gn, _gla_reduce_matrix())


def _matmul_kernel(a_ref, w_ref, o_ref):
    o_ref[...] = jnp.dot(a_ref[...], w_ref[...], preferred_element_type=F32).astype(o_ref.dtype)


def _matmul(a, w, out_dtype, tm=256, tn=512):
    m, k = a.shape
    n = w.shape[1]
    return pl.pallas_call(
        _matmul_kernel,
        out_shape=jax.ShapeDtypeStruct((m, n), out_dtype),
        grid=(m // tm, n // tn),
        in_specs=[pl.BlockSpec((tm, k), lambda i, j: (i, 0)), pl.BlockSpec((k, tn), lambda i, j: (0, j))],
        out_specs=pl.BlockSpec((tm, tn), lambda i, j: (i, j)),
        compiler_params=_cparams(("parallel", "arbitrary")),
        name="matmul",
    )(a, w)


def _route_top2(x, xh, rwh, rwl):
    xl = (x - xh.astype(F32)).astype(BF16)
    logits = (jnp.dot(xh, rwh, preferred_element_type=F32) + jnp.dot(xh, rwl, preferred_element_type=F32)
              + jnp.dot(xl, rwh, preferred_element_type=F32))
    lane = lax.broadcasted_iota(jnp.int32, logits.shape, 1)
    logits = jnp.where(lane < N_EXPERTS, logits, -jnp.inf)
    v1 = jnp.max(logits, axis=-1, keepdims=True)
    i1 = jnp.min(jnp.where(logits == v1, lane, LANES), axis=-1, keepdims=True)
    rest = jnp.where(lane == i1, -jnp.inf, logits)
    v2 = jnp.max(rest, axis=-1, keepdims=True)
    i2 = jnp.min(jnp.where(rest == v2, lane, LANES), axis=-1, keepdims=True)
    e2 = jnp.exp(v2 - v1)
    w1 = 1.0 / (1.0 + e2)
    w2 = e2 / (1.0 + e2)
    return (jnp.where(lane == 0, w1, 0.0) + jnp.where(lane == 1, w2, 0.0)
            + jnp.where(lane == 2, i1.astype(F32), 0.0) + jnp.where(lane == 3, i2.astype(F32), 0.0))


def _mix_xattn_kernel(oa_ref, ob_ref, oc_ref, wa_ref, wb_ref, wc_ref, x_ref, g1_ref, b1_ref,
                      wq_ref, k_ref, v_ref, wo_ref, g2_ref, b2_ref, *rest, route):
    if route:
        rwh_ref, rwl_ref, y_ref, yb_ref, ys_ref, route_ref = rest
    else:
        y_ref, yb_ref = rest
    acc = jnp.dot(oa_ref[0], wa_ref[...], preferred_element_type=F32)
    acc += jnp.dot(ob_ref[0], wb_ref[...], preferred_element_type=F32)
    acc += jnp.dot(oc_ref[0], wc_ref[...], preferred_element_type=F32)
    x1 = _layer_norm(ALPHA * x_ref[0] + acc, g1_ref[...], b1_ref[...])
    q = jnp.dot(x1.astype(BF16), wq_ref[...], preferred_element_type=F32).astype(BF16)
    outs = []
    for h in range(XA_HEADS):
        cols = slice(h * XA_DIM, (h + 1) * XA_DIM)
        s = _dot_nt(q[:, cols], k_ref[0, :, cols]) * (XA_DIM ** -0.5)
        m = jnp.max(s, axis=-1, keepdims=True)
        p = jnp.exp(s - m)
        l = jnp.sum(p, axis=-1, keepdims=True)
        o = jnp.dot(p.astype(BF16), v_ref[0, :, cols], preferred_element_type=F32)
        outs.append((o / l).astype(BF16))
    xa = jnp.dot(jnp.concatenate(outs, axis=1), wo_ref[...], preferred_element_type=F32)
    y = _layer_norm(ALPHA * x1 + xa, g2_ref[...], b2_ref[...])
    yb = y.astype(BF16)
    y_ref[0] = y
    yb_ref[0] = yb
    if route:
        route_ref[0] = _route_top2(y, yb, rwh_ref[...], rwl_ref[...])
        for s in range(ROW_SLABS):
            _slab(ys_ref, s, y.shape[0])[...] = y[:, s * LANES:(s + 1) * LANES]


def _mix_xattn_ln(oa, ob, oc, wo_parts, x3, ln1, wq, kv3, wo, ln2, router_w=None, tm=512):
    bsz, seq, _ = x3.shape
    mem_len = kv3.shape[1]
    route = router_w is not None
    row = lambda n: pl.BlockSpec((1, tm, n), lambda bi, i: (bi, i, 0))
    full = lambda a: pl.BlockSpec(a.shape, lambda bi, i: (0,) * a.ndim)
    steps = seq // tm
    ins = [oa, ob, oc, *wo_parts, x3, *ln1, wq, kv3, kv3, wo, *ln2]
    in_specs = [row(oa.shape[2]), row(ob.shape[2]), row(oc.shape[2]), *[full(w) for w in wo_parts],
                row(D_MODEL), *[full(v) for v in ln1], full(wq),
                pl.BlockSpec((1, mem_len, D_MODEL), lambda bi, i: (bi, 0, 0)),
                pl.BlockSpec((1, mem_len, D_MODEL), lambda bi, i: (bi, 0, 1)),
                full(wo), *[full(v) for v in ln2]]
    out_shape = [jax.ShapeDtypeStruct((bsz, seq, D_MODEL), F32), jax.ShapeDtypeStruct((bsz, seq, D_MODEL), BF16)]
    out_specs = [row(D_MODEL), row(D_MODEL)]
    if route:
        rwh = router_w.astype(BF16)
        rwl = (router_w - rwh.astype(F32)).astype(BF16)
        ins += [rwh, rwl]
        in_specs += [full(rwh), full(rwl)]
        out_shape += [jax.ShapeDtypeStruct((bsz * seq * ROW_SLABS, LANES), F32),
                      jax.ShapeDtypeStruct((bsz, seq, LANES), F32)]
        out_specs += [pl.BlockSpec((tm * ROW_SLABS, LANES), lambda bi, i: (bi * steps + i, 0)), row(LANES)]
    return pl.pallas_call(
        functools.partial(_mix_xattn_kernel, route=route),
        out_shape=tuple(out_shape),
        grid=(bsz, steps),
        in_specs=in_specs,
        out_specs=tuple(out_specs),
        compiler_params=_cparams(("parallel", "arbitrary")),
        name="mix_xattn_ln",
    )(*ins)


def _swiglu_chunk(xb, w1, w3, w2):
    hg = jnp.dot(xb, w1, preferred_element_type=F32)
    hu = jnp.dot(xb, w3, preferred_element_type=F32)
    a = hg * jax.nn.sigmoid(hg) * hu
    return jnp.dot(a.astype(BF16), w2, preferred_element_type=F32)


def _ffn_kernel(xb_ref, x_ref, w1_ref, w3_ref, w2_ref, g_ref, b_ref, y_ref, yb_ref, acc_ref):
    j = pl.program_id(1)

    @pl.when(j == 0)
    def _():
        acc_ref[...] = jnp.zeros_like(acc_ref)

    acc_ref[...] += _swiglu_chunk(xb_ref[...], w1_ref[...], w3_ref[...], w2_ref[...])

    @pl.when(j == pl.num_programs(1) - 1)
    def _():
        y = _layer_norm(ALPHA * x_ref[...] + acc_ref[...], g_ref[...], b_ref[...])
        y_ref[...] = y
        yb_ref[...] = y.astype(BF16)


def _ffn_ln(xb, x, w13, w2, g, b, tm=512, tf=2816):
    t = x.shape[0]
    nf = w2.shape[0] // tf
    row = pl.BlockSpec((tm, D_MODEL), lambda i, j: (i, 0))
    vec = lambda a: pl.BlockSpec(a.shape, lambda i, j: (0, 0))
    return pl.pallas_call(
        _ffn_kernel,
        out_shape=(jax.ShapeDtypeStruct((t, D_MODEL), F32), jax.ShapeDtypeStruct((t, D_MODEL), BF16)),
        grid=(t // tm, nf),
        in_specs=[row, row,
                  pl.BlockSpec((D_MODEL, tf), lambda i, j: (0, j), pipeline_mode=pl.Buffered(1)),
                  pl.BlockSpec((D_MODEL, tf), lambda i, j: (0, j + nf), pipeline_mode=pl.Buffered(1)),
                  pl.BlockSpec((tf, D_MODEL), lambda i, j: (j, 0), pipeline_mode=pl.Buffered(1)),
                  vec(g), vec(b)],
        out_specs=(row, row),
        scratch_shapes=[pltpu.VMEM((tm, D_MODEL), F32)],
        compiler_params=_cparams(("parallel", "arbitrary")),
        name="ffn_ln",
    )(xb, x, w13, w13, w2, g, b)


def _dispatch_plan(route, tm):
    t = route.shape[0]
    expert = route[:, 2:4].astype(jnp.int32).reshape(-1)
    onehot = (expert[:, None] == jnp.arange(N_EXPERTS, dtype=jnp.int32)[None, :]).astype(jnp.int32)
    csum = jnp.cumsum(onehot, axis=0)
    rank = jnp.sum((csum - 1) * onehot, axis=1)
    counts = csum[-1]
    padded = (counts + tm - 1) // tm * tm
    ends = jnp.cumsum(padded)
    dest = (ends - padded)[expert] + rank
    n_tiles = TOP_K * t // tm + N_EXPERTS
    src = jnp.zeros((n_tiles * tm,), jnp.int32).at[dest].set(
        jnp.arange(TOP_K * t, dtype=jnp.int32) // TOP_K, unique_indices=True, mode="promise_in_bounds")
    tile_start = jnp.arange(n_tiles, dtype=jnp.int32) * tm
    tile_expert = jnp.sum((tile_start[:, None] >= ends[None, :]).astype(jnp.int32), axis=1)
    tile_expert = jnp.minimum(tile_expert, N_EXPERTS - 1)
    tile_used = (tile_start < ends[-1]).astype(jnp.int32)
    return src, dest.reshape(t, TOP_K).astype(jnp.int32), tile_expert, tile_used


ROW_SLABS = D_MODEL // LANES


def _slab(ref, s, rows):
    return ref.at[pl.ds(s, rows, stride=ROW_SLABS), :]


def _row_tile(ref, r):
    return ref.at[pl.ds(r * ROW_SLABS, ROW_SLABS), :]


GATHER_PRIORITY = 1


def _moe_ffn_kernel(te_ref, tu_ref, src_ref, nxt_ref, x_hbm, w1_ref, w3_ref, w2_ref, y_ref,
                    xbuf, xs_ref, acc_ref, sem, *, tm):
    i = pl.program_id(0)
    j = pl.program_id(1)
    n_tiles = pl.num_programs(0)
    last = pl.num_programs(1) - 1
    slot = i % 2

    def start_rows(idx_ref, dst_slot):
        def issue(r, carry):
            pltpu.make_async_copy(_row_tile(x_hbm, idx_ref[0, 0, r]), _row_tile(xbuf.at[dst_slot], r),
                                  sem.at[dst_slot]).start(priority=GATHER_PRIORITY)
            return carry
        lax.fori_loop(0, tm, issue, 0, unroll=8)

    @pl.when(j == 0)
    def _():
        @pl.when((i == 0) & (tu_ref[0] > 0))
        def _():
            start_rows(src_ref, 0)

        @pl.when(tu_ref[i] > 0)
        def _():
            pltpu.make_async_copy(x_hbm.at[pl.ds(0, tm * ROW_SLABS), :], xbuf.at[slot], sem.at[slot]).wait()
            for s in range(ROW_SLABS):
                xs_ref[:, s * LANES:(s + 1) * LANES] = _slab(xbuf.at[slot], s, tm)[...].astype(xs_ref.dtype)

        @pl.when((i + 1 < n_tiles) & (tu_ref[jnp.minimum(i + 1, n_tiles - 1)] > 0))
        def _():
            start_rows(nxt_ref, 1 - slot)

        acc_ref[...] = jnp.zeros_like(acc_ref)

    @pl.when(tu_ref[i] > 0)
    def _():
        acc_ref[...] += _swiglu_chunk(xs_ref[...], w1_ref[0], w3_ref[0], w2_ref[0])

    @pl.when(j == last)
    def _():
        for s in range(ROW_SLABS):
            _slab(y_ref, s, tm)[...] = acc_ref[:, s * LANES:(s + 1) * LANES]


def _moe_ffn(tile_expert, tile_used, src, x3, w13, w2, tm, tf):
    n_tiles = src.shape[0]
    nf = w2.shape[1] // tf
    chunk = lambda i, j, tu: jnp.where(tu[i] > 0, j, nf - 1)
    idx = lambda shift: pl.BlockSpec((1, 1, tm), lambda i, j, te, tu: (jnp.minimum(i + shift, n_tiles - 1), 0, 0),
                                     memory_space=pltpu.SMEM)
    return pl.pallas_call(
        functools.partial(_moe_ffn_kernel, tm=tm),
        out_shape=jax.ShapeDtypeStruct((n_tiles * tm * ROW_SLABS, LANES), F32),
        grid_spec=pltpu.PrefetchScalarGridSpec(
            num_scalar_prefetch=2, grid=(n_tiles, nf),
            in_specs=[idx(0), idx(1), pl.BlockSpec(memory_space=pl.ANY),
                      pl.BlockSpec((1, D_MODEL, tf), lambda i, j, te, tu: (te[i], 0, chunk(i, j, tu))),
                      pl.BlockSpec((1, D_MODEL, tf), lambda i, j, te, tu: (te[i], 0, chunk(i, j, tu) + nf)),
                      pl.BlockSpec((1, tf, D_MODEL), lambda i, j, te, tu: (te[i], chunk(i, j, tu), 0))],
            out_specs=pl.BlockSpec((tm * ROW_SLABS, LANES), lambda i, j, te, tu: (i, 0)),
            scratch_shapes=[pltpu.VMEM((2, tm * ROW_SLABS, LANES), F32), pltpu.VMEM((tm, D_MODEL), BF16),
                            pltpu.VMEM((tm, D_MODEL), F32), pltpu.SemaphoreType.DMA((2,))]),
        compiler_params=_cparams(("arbitrary", "arbitrary")),
        name="moe_ffn",
    )(tile_expert, tile_used, src, src, x3, w13, w13, w2)


def _combine_kernel(d0_ref, d1_ref, n0_ref, n1_ref, y_hbm, route_ref, x_ref, g_ref, b_ref, o_ref, ob_ref,
                    buf, sem, *, tm):
    i = pl.program_id(0)
    slot = i % 2

    def start_rows(a_ref, b_ref, dst_slot):
        def issue(r, carry):
            pltpu.make_async_copy(_row_tile(y_hbm, a_ref[0, 0, r]), _row_tile(buf.at[dst_slot, 0], r),
                                  sem.at[dst_slot, 0]).start(priority=0)
            pltpu.make_async_copy(_row_tile(y_hbm, b_ref[0, 0, r]), _row_tile(buf.at[dst_slot, 1], r),
                                  sem.at[dst_slot, 1]).start(priority=1)
            return carry
        lax.fori_loop(0, tm, issue, 0, unroll=8)

    @pl.when(i == 0)
    def _():
        start_rows(d0_ref, d1_ref, 0)

    for k in range(TOP_K):
        pltpu.make_async_copy(y_hbm.at[pl.ds(0, tm * ROW_SLABS), :], buf.at[slot, k], sem.at[slot, k]).wait()

    @pl.when(i + 1 < pl.num_programs(0))
    def _():
        start_rows(n0_ref, n1_ref, 1 - slot)

    route = route_ref[...]
    w0, w1 = route[:, 0:1], route[:, 1:2]
    f = jnp.concatenate([w0 * _slab(buf.at[slot, 0], s, tm)[...] + w1 * _slab(buf.at[slot, 1], s, tm)[...]
                         for s in range(ROW_SLABS)], axis=1)
    y = _layer_norm(ALPHA * x_ref[...] + f, g_ref[...], b_ref[...])
    o_ref[...] = y
    ob_ref[...] = y.astype(BF16)


def _combine_ln(dest, ys, route, x, g, b, tm=512):
    t = x.shape[0]
    steps = t // tm
    row = pl.BlockSpec((tm, D_MODEL), lambda i: (i, 0))
    vec = lambda a: pl.BlockSpec(a.shape, lambda i: (0, 0))
    idx = lambda shift: pl.BlockSpec((1, 1, tm), lambda i: (jnp.minimum(i + shift, steps - 1), 0, 0),
                                     memory_space=pltpu.SMEM)
    d0, d1 = dest[:, 0].reshape(-1, 1, tm), dest[:, 1].reshape(-1, 1, tm)
    return pl.pallas_call(
        functools.partial(_combine_kernel, tm=tm),
        out_shape=(jax.ShapeDtypeStruct((t, D_MODEL), F32), jax.ShapeDtypeStruct((t, D_MODEL), BF16)),
        grid=(steps,),
        in_specs=[idx(0), idx(0), idx(1), idx(1), pl.BlockSpec(memory_space=pl.ANY),
                  pl.BlockSpec((tm, LANES), lambda i: (i, 0)),
                  row, vec(g), vec(b)],
        out_specs=(row, row),
        scratch_shapes=[pltpu.VMEM((2, TOP_K, tm * ROW_SLABS, LANES), F32), pltpu.SemaphoreType.DMA((2, TOP_K))],
        compiler_params=_cparams(("arbitrary",)),
        name="moe_combine_ln",
    )(d0, d1, d0, d1, ys, route, x, g, b)


def _moe_ln(x, x_slab, route, w13, w2, g, b, tm=512, tf=1792):
    src, dest, tile_expert, tile_used = _dispatch_plan(route, tm)
    ys = _moe_ffn(tile_expert, tile_used, src.reshape(-1, 1, tm), x_slab, w13, w2, tm, tf)
    return _combine_ln(dest, ys, route, x, g, b)


def _pad_heads(w, heads, width, padded, axis):
    shape = list(w.shape)
    shape[axis:axis + 1] = [heads, width]
    w = w.reshape(shape)
    pad = [(0, 0)] * w.ndim
    pad[axis + 1] = (0, padded - width)
    w = jnp.pad(w, pad)
    shape[axis:axis + 2] = [heads * padded]
    return w.reshape(shape)


def _layout_w_in(w):
    bounds = np.cumsum([0, 256, 256, 256, 384, 384, 384, 192, 192, 384, 384, 16])
    qa, ka, va, qb, kb, vb, qc, kc, vc, gc, rc = [w[:, bounds[i]:bounds[i + 1]] for i in range(11)]
    rc = jnp.pad(rc, ((0, 0), (0, LANES - C_RANK)))
    qc = _pad_heads(qc, C_HEADS, C_K, C_KP, 1)
    kc = _pad_heads(kc, C_HEADS, C_K, C_KP, 1)
    vc = _pad_heads(vc, C_HEADS, C_V, C_VP, 1)
    gc = _pad_heads(gc, C_HEADS, C_V, C_VP, 1)
    return jnp.concatenate([qa, ka, qb, kb, va, vb, rc, qc, kc, vc, gc], axis=1).astype(BF16)


def _rope_tables(positions, dim):
    rot = dim // ROPE_FRAC
    inv_freq = ROPE_THETA ** (-jnp.arange(0, rot, 2, dtype=F32) / rot)
    ang = positions.astype(F32).reshape(-1, 1) * inv_freq
    cos, sin = jnp.cos(ang), jnp.sin(ang)
    t = ang.shape[0]
    c = jnp.concatenate([cos, cos, jnp.ones((t, dim - rot), F32)], axis=1)
    s = jnp.concatenate([-sin, sin, jnp.zeros((t, dim - rot), F32)], axis=1)
    reps = LANES // dim
    return jnp.tile(c, (1, reps)), jnp.tile(s, (1, reps))


def kernel(x, mem, positions, w_in, lam_q1, lam_k1, lam_q2, lam_k2, diff_norm_g, gla_wa2, gla_ba, gla_norm_g,
           w_out, ln_mix_g, ln_mix_b, xa_wq, xa_wkv, xa_wo, ln_xa_g, ln_xa_b, ffd_w13, ffd_w2, moe_router,
           moe_w13, moe_w2, ln_ffn_g, ln_ffn_b):
    bsz, seq, _ = x.shape
    t = bsz * seq
    ca, sa = _rope_tables(positions, A_QK)
    cb, sb = _rope_tables(positions, B_DIM)
    tabs = (ca, sa, cb, sb)
    memb = mem.reshape(-1, D_MODEL).astype(BF16)
    mem_len = mem.shape[1]
    row = lambda v: v.reshape(1, -1).astype(F32)

    xf = x.reshape(t, D_MODEL)
    xb = xf.astype(BF16)
    for l in range(DEPTH):
        lam_init = 0.8 - 0.6 * math.exp(-0.3 * l)
        lam = (jnp.exp(jnp.sum(lam_q1[l] * lam_k1[l])) - jnp.exp(jnp.sum(lam_q2[l] * lam_k2[l]))
               + lam_init).reshape(1, 1).astype(F32)
        h = _inproj(xb, _layout_w_in(w_in[l]), tabs)
        h3 = h.reshape(bsz, seq, H_COLS)
        oa = _diff_attention(h3, lam, jnp.tile(row(diff_norm_g[l]), (1, 2)), lam_init)
        ob = _dilated_attention(h3)
        wa = _pad_heads(gla_wa2[l], C_HEADS, C_K, C_KP, 1)
        wa = jnp.pad(wa, ((0, LANES - C_RANK), (0, 0))).reshape(LANES, 2, LANES).transpose(1, 0, 2).astype(BF16)
        ba = _pad_heads(gla_ba[l], C_HEADS, C_K, C_KP, 0).reshape(2, 1, LANES).astype(F32)
        gn = jnp.pad(row(gla_norm_g[l]), ((0, 0), (0, C_VP - C_V)))
        oc = _gla(h3, wa, ba, gn)
        wo = w_out[l]
        wo_a = wo[:A_HEADS * A_V].astype(BF16)
        wo_b = wo[A_HEADS * A_V:A_HEADS * A_V + B_HEADS * B_DIM].astype(BF16)
        wo_c = _pad_heads(wo[A_HEADS * A_V + B_HEADS * B_DIM:], C_HEADS, C_V, C_VP, 0).astype(BF16)
        kv = _matmul(memb, xa_wkv[l].astype(BF16), BF16).reshape(bsz, mem_len, 2 * D_MODEL)
        moe = l % 2 == 1
        rw = jnp.pad(moe_router[l // 2], ((0, 0), (0, LANES - N_EXPERTS))) if moe else None
        outs = _mix_xattn_ln(oa, ob, oc, (wo_a, wo_b, wo_c), xf.reshape(bsz, seq, D_MODEL),
                             (row(ln_mix_g[l]), row(ln_mix_b[l])), xa_wq[l].astype(BF16), kv, xa_wo[l].astype(BF16),
                             (row(ln_xa_g[l]), row(ln_xa_b[l])), router_w=rw)
        xf, xb = outs[0].reshape(t, D_MODEL), outs[1].reshape(t, D_MODEL)
        if moe:
            xf, xb = _moe_ln(xf, outs[2], outs[3].reshape(t, LANES), moe_w13[l // 2].astype(BF16),
                             moe_w2[l // 2].astype(BF16), row(ln_ffn_g[l]), row(ln_ffn_b[l]))
        else:
            xf, xb = _ffn_ln(xb, xf, ffd_w13[l // 2].astype(BF16), ffd_w2[l // 2].astype(BF16),
                             row(ln_ffn_g[l]), row(ln_ffn_b[l]))
    return xf.reshape(bsz, seq, D_MODEL)
```

```python
import functools
import math

import numpy as np
import jax
import jax.numpy as jnp
from jax import lax
from jax.experimental import pallas as pl
from jax.experimental.pallas import tpu as pltpu

F32 = jnp.float32
BF16 = jnp.bfloat16

LANES = 128
VMEM_LIMIT_BYTES = 56 * 1024 * 1024

D_MODEL = 1024
DEPTH = 2
ROPE_THETA = 500000.0
ROPE_FRAC = 4
LN_EPS = 1e-5
NEG_INF = -1e30
ALPHA = (2 * DEPTH) ** 0.25
LOG2_E = math.log2(math.e)

A_HEADS, A_QK, A_V = 4, 32, 64
B_HEADS, B_DIM = 6, 64
B_SPAN = 128
B_DILATIONS = (1, 4, 16)
C_HEADS, C_K, C_V = 4, 48, 96
C_KP, C_VP = 64, 128
C_RANK, C_TAU, C_CHUNK = 16, 16.0, 16
XA_HEADS, XA_DIM = 4, 256
N_EXPERTS, TOP_K = 8, 2

QA, KA, QB, KB, VA, VB, RC, QC, KC, VC, GC, H_COLS = (
    0, 256, 512, 896, 1280, 1536, 1920, 2048, 2304, 2560, 3072, 3584)
IN_TN = 256
ROPE_A_TILES = (QB - QA) // IN_TN
ROPE_B_TILES = (VA - QA) // IN_TN


def _cparams(sem):
    return pltpu.CompilerParams(dimension_semantics=sem, vmem_limit_bytes=VMEM_LIMIT_BYTES)


def _layer_norm(y, g, b):
    mu = jnp.mean(y, axis=-1, keepdims=True)
    yc = y - mu
    var = jnp.mean(yc * yc, axis=-1, keepdims=True)
    return yc * lax.rsqrt(var + LN_EPS) * g + b


def _dot_nt(a, b):
    return lax.dot_general(a, b, (((1,), (1,)), ((), ())), preferred_element_type=F32)


def _inproj_kernel(x_ref, w_ref, ca_ref, sa_ref, cb_ref, sb_ref, o_ref):
    x = x_ref[...]
    lane = lax.broadcasted_iota(jnp.int32, (x.shape[0], LANES), 1)

    def rope_cols(lo, k_lo, hi, c_ref, s_ref, half, period):
        acc = jnp.dot(x, w_ref[:, lo:hi], preferred_element_type=F32)
        c, s = c_ref[...], s_ref[...]
        first = lane % period < half
        for k in range((hi - lo) // LANES):
            a = acc[:, k * LANES:(k + 1) * LANES]
            up = pltpu.roll(a, LANES - half, 1)
            dn = pltpu.roll(a, half, 1)
            r = a * c + jnp.where(first, up, dn) * s
            if lo + k * LANES < k_lo:
                r = r * (period ** -0.5 * LOG2_E)
            o_ref[:, lo + k * LANES:lo + (k + 1) * LANES] = r.astype(o_ref.dtype)

    rope_cols(QA, KA, QB, ca_ref, sa_ref, A_QK // ROPE_FRAC // 2, A_QK)
    rope_cols(QB, KB, VA, cb_ref, sb_ref, B_DIM // ROPE_FRAC // 2, B_DIM)
    o_ref[:, VA:] = jnp.dot(x, w_ref[:, VA:], preferred_element_type=F32).astype(o_ref.dtype)


def _inproj(xb, w, tabs, tm=512):
    t = xb.shape[0]
    tab_spec = pl.BlockSpec((tm, LANES), lambda i: (i, 0))
    return pl.pallas_call(
        _inproj_kernel,
        out_shape=jax.ShapeDtypeStruct((t, H_COLS), BF16),
        grid=(t // tm,),
        in_specs=[pl.BlockSpec((tm, D_MODEL), lambda i: (i, 0)),
                  pl.BlockSpec((D_MODEL, H_COLS), lambda i: (0, 0)),
                  tab_spec, tab_spec, tab_spec, tab_spec],
        out_specs=pl.BlockSpec((tm, H_COLS), lambda i: (i, 0)),
        compiler_params=_cparams(("parallel",)),
        name="inproj_rope",
    )(xb, w, *tabs)


def _diff_kernel(lam_ref, q_ref, k_ref, v_ref, g_ref, o_ref, *, seq, tq, out_scale):
    lane = lax.broadcasted_iota(jnp.int32, (tq, LANES), 1)
    row = lax.broadcasted_iota(jnp.int32, (tq, tq), 0)
    col = lax.broadcasted_iota(jnp.int32, (tq, tq), 1)
    causal = col <= row
    lo = lane < A_V
    lam = lam_ref[0, 0]

    for i in range(seq // tq):
        past = i * tq
        q = q_ref[0, past:past + tq, :]
        k_d = k_ref[0, past:past + tq, :]
        v_d = v_ref[0, past:past + tq, :]

        def attend(c):
            qm = jnp.where(lane // A_QK == c, q, jnp.zeros_like(q))
            s_d = jnp.where(causal, _dot_nt(qm, k_d), NEG_INF)
            m = jnp.max(s_d, axis=-1, keepdims=True)
            if past:
                s_p = _dot_nt(qm, k_ref[0, :past, :])
                m = jnp.maximum(m, jnp.max(s_p, axis=-1, keepdims=True))
            p_d = jnp.exp2(s_d - m)
            l = jnp.sum(p_d, axis=-1, keepdims=True)
            acc = jnp.dot(p_d.astype(BF16), v_d, preferred_element_type=F32)
            if past:
                p_p = jnp.exp2(s_p - m)
                l = l + jnp.sum(p_p, axis=-1, keepdims=True)
                acc = acc + jnp.dot(p_p.astype(BF16), v_ref[0, :past, :], preferred_element_type=F32)
            return acc / l

        o_h0 = attend(0) - lam * attend(1)
        o_h1 = attend(2) - lam * attend(3)
        o = jnp.where(lo, o_h0, o_h1)
        sq = o * o
        s_lo = jnp.sum(jnp.where(lo, sq, 0.0), axis=-1, keepdims=True)
        s_all = jnp.sum(sq, axis=-1, keepdims=True)
        ms = jnp.where(lo, s_lo, s_all - s_lo) * (1.0 / A_V)
        o_ref[0, past:past + tq, :] = (o * lax.rsqrt(ms + LN_EPS) * g_ref[...] * out_scale).astype(o_ref.dtype)


def _diff_attention(h3, lam, g, lam_init, tq=512):
    bsz, seq, _ = h3.shape
    kern = functools.partial(_diff_kernel, seq=seq, tq=tq, out_scale=1.0 - lam_init)
    blk = lambda off: pl.BlockSpec((1, seq, LANES), lambda b, p: (b, 0, off // LANES + p))
    return pl.pallas_call(
        kern,
        out_shape=jax.ShapeDtypeStruct((bsz, seq, A_HEADS * A_V), BF16),
        grid=(bsz, A_HEADS // 2),
        in_specs=[pl.BlockSpec(memory_space=pltpu.SMEM), blk(QA), blk(KA), blk(VA),
                  pl.BlockSpec((1, LANES), lambda b, p: (0, 0))],
        out_specs=pl.BlockSpec((1, seq, LANES), lambda b, p: (b, 0, p)),
        compiler_params=_cparams(("parallel", "parallel")),
        name="diff_attention",
    )(lam, h3, h3, h3, g)


def _dil_kernel(q_ref, k_ref, v_ref, o_ref, qf, kf, vf, qs, ks, vs, m_s, l_s, a_s, *, seq):
    qf[...] = q_ref[0].astype(F32)
    kf[...] = k_ref[0].astype(F32)
    vf[...] = v_ref[0].astype(F32)
    span = B_SPAN
    lo = lax.broadcasted_iota(jnp.int32, (span, LANES), 1) < B_DIM
    own = ((lax.broadcasted_iota(jnp.int32, (2 * span, LANES), 1) < B_DIM)
           == (lax.broadcasted_iota(jnp.int32, (2 * span, LANES), 0) < span))
    qi = lax.broadcasted_iota(jnp.int32, (2 * span, 2 * span), 0) % span
    ki = lax.broadcasted_iota(jnp.int32, (2 * span, 2 * span), 1)
    band = (ki >= qi) & (ki <= qi + span)
    first = (lax.broadcasted_iota(jnp.int32, (2 * span, span), 1)
             <= lax.broadcasted_iota(jnp.int32, (2 * span, span), 0) % span)

    for g, r in enumerate(B_DILATIONS):
        sub = seq // r
        for rho in range(r):
            rows = pl.ds(rho, sub, stride=r) if r > 1 else pl.ds(0, sub)
            qs[rho * sub:(rho + 1) * sub, :] = qf[rows, :].astype(BF16)
            ks[rho * sub:(rho + 1) * sub, :] = kf[rows, :].astype(BF16)
            vs[rho * sub:(rho + 1) * sub, :] = vf[rows, :].astype(BF16)
        for rho in range(r):
            for n in range(sub // span):
                base = rho * sub + n * span
                qb = qs[base:base + span, :]
                lhs = jnp.where(own, jnp.concatenate([qb, qb], axis=0), jnp.zeros((), BF16))
                start = base - span if n else base
                s = jnp.where(band if n else first, _dot_nt(lhs, ks[start:base + span, :]), NEG_INF)
                m = jnp.max(s, axis=-1, keepdims=True)
                p = jnp.exp2(s - m)
                l = jnp.sum(p, axis=-1, keepdims=True)
                a = jnp.dot(p.astype(BF16), vs[start:base + span, :], preferred_element_type=F32)
                m = jnp.broadcast_to(m, (2 * span, LANES))
                l = jnp.broadcast_to(l, (2 * span, LANES))
                m_g = jnp.where(lo, m[:span], m[span:])
                l_g = jnp.where(lo, l[:span], l[span:])
                a_g = jnp.where(lo, a[:span], a[span:])
                nat = pl.ds(rho + r * span * n, span, stride=r) if r > 1 else pl.ds(n * span, span)
                if g == 0:
                    m_s[nat, :] = m_g
                    l_s[nat, :] = l_g
                    a_s[nat, :] = a_g
                else:
                    m_o = m_s[nat, :]
                    m_n = jnp.maximum(m_o, m_g)
                    w_o = jnp.exp2(m_o - m_n)
                    w_g = jnp.exp2(m_g - m_n)
                    m_s[nat, :] = m_n
                    l_s[nat, :] = w_o * l_s[nat, :] + w_g * l_g
                    a_s[nat, :] = w_o * a_s[nat, :] + w_g * a_g

    o_ref[0] = (a_s[...] / l_s[...]).astype(o_ref.dtype)


def _dilated_attention(h3):
    bsz, seq, _ = h3.shape
    blk = lambda off: pl.BlockSpec((1, seq, LANES), lambda b, p: (b, 0, off // LANES + p))
    return pl.pallas_call(
        functools.partial(_dil_kernel, seq=seq),
        out_shape=jax.ShapeDtypeStruct((bsz, seq, B_HEADS * B_DIM), BF16),
        grid=(bsz, B_HEADS // 2),
        in_specs=[blk(QB), blk(KB), blk(VB)],
        out_specs=pl.BlockSpec((1, seq, LANES), lambda b, p: (b, 0, p)),
        scratch_shapes=([pltpu.VMEM((seq, LANES), F32)] * 3 + [pltpu.VMEM((seq, LANES), BF16)] * 3
                        + [pltpu.VMEM((seq, LANES), F32)] * 3),
        compiler_params=_cparams(("parallel", "parallel")),
        name="dilated_attention",
    )(h3, h3, h3)


def _gla_kernel(q_ref, k_ref, v_ref, gate_ref, rc_ref, wa_ref, ba_ref, gn_ref, rcat_ref, o_ref, st, *, tb):
    t = pl.program_id(2)

    @pl.when(t == 0)
    def _():
        st[...] = jnp.zeros_like(st)

    ch, half, nc = C_CHUNK, C_CHUNK // 2, tb // C_CHUNK
    z = jnp.dot(rc_ref[0], wa_ref[0], preferred_element_type=F32) + ba_ref[0]
    la = (jnp.minimum(z, 0.0) - jnp.log(1.0 + jnp.exp(-jnp.abs(z)))) * (LOG2_E / C_TAU)
    ri = lax.broadcasted_iota(jnp.int32, (LANES, LANES), 0)
    ci = lax.broadcasted_iota(jnp.int32, (LANES, LANES), 1)
    same = (ri // ch) == (ci // ch)
    tri = jnp.where(same & (ci <= ri), 1.0, 0.0).astype(F32)
    blk = jnp.where(same, 1.0, 0.0).astype(F32)
    sums = lambda m: jnp.concatenate(
        [jnp.dot(m, la[r:r + LANES], preferred_element_type=F32, precision=lax.Precision.HIGHEST)
         for r in range(0, tb, LANES)], axis=0)
    b = sums(tri)
    bl = sums(blk)
    qs = q_ref[0].astype(F32) * (C_K ** -0.5)
    kf = k_ref[0].astype(F32)
    vb = v_ref[0]
    qt = (qs * jnp.exp2(b)).astype(BF16)
    kt = kf * jnp.exp2(bl - b)
    gl = jnp.exp2(bl)
    head0 = lax.broadcasted_iota(jnp.int32, (tb, LANES), 1) < C_KP
    kt_h = (jnp.where(head0, kt, 0.0).astype(BF16), jnp.where(head0, 0.0, kt).astype(BF16))

    def chunk_row(x, j, rows):
        xj = x.reshape(nc, ch, LANES)[:, j:j + 1, :]
        return jnp.broadcast_to(xj, (nc, rows, LANES)).reshape(nc * rows, LANES)

    def upper(x):
        return x.reshape(nc, 2, half, LANES)[:, 1].reshape(nc * half, LANES)

    rowi = lax.broadcasted_iota(jnp.int32, (tb, LANES), 0) % ch
    rowu = lax.broadcasted_iota(jnp.int32, (nc * half, LANES), 0) % half + half
    b_u, qs_u = upper(b), upper(qs)
    terms = []
    for j in range(ch):
        if j < half:
            e = jnp.exp2(jnp.where(rowi >= j, b - chunk_row(b, j, ch), NEG_INF))
            term = qs * e * chunk_row(kf, j, ch)
        else:
            e = jnp.exp2(jnp.where(rowu >= j, b_u - chunk_row(b, j, half), NEG_INF))
            tu = (qs_u * e * chunk_row(kf, j, half)).reshape(nc, 1, half, LANES)
            term = jnp.concatenate([jnp.zeros_like(tu), tu], axis=1).reshape(tb, LANES)
        terms.append(term.astype(BF16))
    s_tiled = jnp.dot(jnp.concatenate(terms, axis=1), rcat_ref[...], preferred_element_type=F32)
    rr = lax.broadcasted_iota(jnp.int32, (tb, 2 * LANES), 0)
    cc = lax.broadcasted_iota(jnp.int32, (tb, 2 * LANES), 1)
    m = jnp.where((rr % LANES) // ch == (cc % LANES) // ch, s_tiled, 0.0).astype(BF16)
    o_intra = jnp.concatenate(
        [jnp.concatenate([jnp.dot(m[r:r + LANES, h * LANES:(h + 1) * LANES], vb[r:r + LANES, h * C_VP:(h + 1) * C_VP],
                                  preferred_element_type=F32) for h in range(2)], axis=1)
         for r in range(0, tb, LANES)], axis=0)

    chunks = [slice(c * ch, (c + 1) * ch) for c in range(nc)]
    uts = [jnp.concatenate(
        [lax.dot_general(vb[rows, h * C_VP:(h + 1) * C_VP], kt_h[h][rows], (((0,), (0,)), ((), ())),
                         preferred_element_type=F32) for h in range(2)], axis=0) for rows in chunks]
    s_prev = st[...]
    states = []
    for c in range(nc):
        states.append(s_prev.astype(BF16))
        s_prev = gl[c * ch:c * ch + 1, :] * s_prev + uts[c]
    st[...] = s_prev
    o_inter = [_dot_nt(qt[rows], states[c]) for c, rows in enumerate(chunks)]

    o = o_intra + jnp.concatenate(o_inter, axis=0)
    gate = gate_ref[0].astype(F32)
    outs = []
    for hh in range(2):
        oh = o[:, hh * C_VP:(hh + 1) * C_VP]
        ms = jnp.sum(oh * oh, axis=-1, keepdims=True) * (1.0 / C_V)
        gh = gate[:, hh * C_VP:(hh + 1) * C_VP]
        outs.append(oh * lax.rsqrt(ms + LN_EPS) * gn_ref[...] * (gh * jax.nn.sigmoid(gh)))
    o_ref[0] = jnp.concatenate(outs, axis=1).astype(o_ref.dtype)


def _gla_reduce_matrix():
    r = np.arange(C_CHUNK * LANES)[:, None]
    c = np.arange(2 * LANES)[None, :]
    hit = ((r % LANES) // C_KP == c // LANES) & (r // LANES == (c % LANES) % C_CHUNK)
    return jnp.asarray(hit, dtype=BF16)


def _gla(h3, wa, ba, gn, tb=1024):
    bsz, seq, _ = h3.shape
    return pl.pallas_call(
        functools.partial(_gla_kernel, tb=tb),
        out_shape=jax.ShapeDtypeStruct((bsz, seq, C_HEADS * C_VP), BF16),
        grid=(bsz, C_HEADS // 2, seq // tb),
        in_specs=[pl.BlockSpec((1, tb, LANES), lambda b, p, t: (b, t, QC // LANES + p)),
                  pl.BlockSpec((1, tb, LANES), lambda b, p, t: (b, t, KC // LANES + p)),
                  pl.BlockSpec((1, tb, 2 * C_VP), lambda b, p, t: (b, t, VC // (2 * C_VP) + p)),
                  pl.BlockSpec((1, tb, 2 * C_VP), lambda b, p, t: (b, t, GC // (2 * C_VP) + p)),
                  pl.BlockSpec((1, tb, LANES), lambda b, p, t: (b, t, RC // LANES)),
                  pl.BlockSpec((1, LANES, LANES), lambda b, p, t: (p, 0, 0)),
                  pl.BlockSpec((1, 1, LANES), lambda b, p, t: (p, 0, 0)),
                  pl.BlockSpec((1, C_VP), lambda b, p, t: (0, 0)),
                  pl.BlockSpec((C_CHUNK * LANES, 2 * LANES), lambda b, p, t: (0, 0))],
        out_specs=pl.BlockSpec((1, tb, 2 * C_VP), lambda b, p, t: (b, t, p)),
        scratch_shapes=[pltpu.VMEM((2 * C_VP, LANES), F32)],
        compiler_params=_cparams(("parallel", "parallel", "arbitrary")),
        name="gla",
    )(h3, h3, h3, h3, h3, wa, ba, gn, _gla_reduce_matrix())


def _matmul_kernel(a_ref, w_ref, o_ref):
    o_ref[...] = jnp.dot(a_ref[...], w_ref[...], preferred_element_type=F32).astype(o_ref.dtype)


def _matmul(a, w, out_dtype, tm=512):
    m, k = a.shape
    n = w.shape[1]
    tm = min(tm, m)
    return pl.pallas_call(
        _matmul_kernel,
        out_shape=jax.ShapeDtypeStruct((m, n), out_dtype),
        grid=(m // tm,),
        in_specs=[pl.BlockSpec((tm, k), lambda i: (i, 0)), pl.BlockSpec((k, n), lambda i: (0, 0))],
        out_specs=pl.BlockSpec((tm, n), lambda i: (i, 0)),
        compiler_params=_cparams(("parallel",)),
        name="matmul",
    )(a, w)


def _route_top2(x, xh, rwh, rwl):
    xl = (x - xh.astype(F32)).astype(BF16)
    logits = (jnp.dot(xh, rwh, preferred_element_type=F32) + jnp.dot(xh, rwl, preferred_element_type=F32)
              + jnp.dot(xl, rwh, preferred_element_type=F32))
    lane = lax.broadcasted_iota(jnp.int32, logits.shape, 1)
    logits = jnp.where(lane < N_EXPERTS, logits, -jnp.inf)
    v1 = jnp.max(logits, axis=-1, keepdims=True)
    i1 = jnp.min(jnp.where(logits == v1, lane, LANES), axis=-1, keepdims=True)
    rest = jnp.where(lane == i1, -jnp.inf, logits)
    v2 = jnp.max(rest, axis=-1, keepdims=True)
    i2 = jnp.min(jnp.where(rest == v2, lane, LANES), axis=-1, keepdims=True)
    e2 = jnp.exp(v2 - v1)
    w1 = 1.0 / (1.0 + e2)
    w2 = e2 / (1.0 + e2)
    return (jnp.where(lane == 0, w1, 0.0) + jnp.where(lane == 1, w2, 0.0)
            + jnp.where(lane == 2, i1.astype(F32), 0.0) + jnp.where(lane == 3, i2.astype(F32), 0.0))


def _mix_xattn_kernel(oa_ref, ob_ref, oc_ref, wa_ref, wb_ref, wc_ref, x_ref, g1_ref, b1_ref,
                      wq_ref, k_ref, v_ref, wo_ref, g2_ref, b2_ref, *rest, route):
    if route:
        rwh_ref, rwl_ref, y_ref, yb_ref, ys_ref, route_ref = rest
    else:
        y_ref, yb_ref = rest
    acc = jnp.dot(oa_ref[0], wa_ref[...], preferred_element_type=F32)
    acc += jnp.dot(ob_ref[0], wb_ref[...], preferred_element_type=F32)
    acc += jnp.dot(oc_ref[0], wc_ref[...], preferred_element_type=F32)
    x1 = _layer_norm(ALPHA * x_ref[0] + acc, g1_ref[...], b1_ref[...])
    q = jnp.dot(x1.astype(BF16), wq_ref[...], preferred_element_type=F32).astype(BF16)
    outs = []
    for h in range(XA_HEADS):
        cols = slice(h * XA_DIM, (h + 1) * XA_DIM)
        s = _dot_nt(q[:, cols], k_ref[0, :, cols]) * (XA_DIM ** -0.5)
        m = jnp.max(s, axis=-1, keepdims=True)
        p = jnp.exp(s - m)
        l = jnp.sum(p, axis=-1, keepdims=True)
        o = jnp.dot(p.astype(BF16), v_ref[0, :, cols], preferred_element_type=F32)
        outs.append((o / l).astype(BF16))
    xa = jnp.dot(jnp.concatenate(outs, axis=1), wo_ref[...], preferred_element_type=F32)
    y = _layer_norm(ALPHA * x1 + xa, g2_ref[...], b2_ref[...])
    yb = y.astype(BF16)
    y_ref[0] = y
    yb_ref[0] = yb
    if route:
        route_ref[0] = _route_top2(y, yb, rwh_ref[...], rwl_ref[...])
        for s in range(ROW_SLABS):
            _slab(ys_ref, s, y.shape[0])[...] = y[:, s * LANES:(s + 1) * LANES]


def _mix_xattn_ln(oa, ob, oc, wo_parts, x3, ln1, wq, kv3, wo, ln2, router_w=None, tm=512):
    bsz, seq, _ = x3.shape
    mem_len = kv3.shape[1]
    route = router_w is not None
    row = lambda n: pl.BlockSpec((1, tm, n), lambda bi, i: (bi, i, 0))
    full = lambda a: pl.BlockSpec(a.shape, lambda bi, i: (0,) * a.ndim)
    steps = seq // tm
    ins = [oa, ob, oc, *wo_parts, x3, *ln1, wq, kv3, kv3, wo, *ln2]
    in_specs = [row(oa.shape[2]), row(ob.shape[2]), row(oc.shape[2]), *[full(w) for w in wo_parts],
                row(D_MODEL), *[full(v) for v in ln1], full(wq),
                pl.BlockSpec((1, mem_len, D_MODEL), lambda bi, i: (bi, 0, 0)),
                pl.BlockSpec((1, mem_len, D_MODEL), lambda bi, i: (bi, 0, 1)),
                full(wo), *[full(v) for v in ln2]]
    out_shape = [jax.ShapeDtypeStruct((bsz, seq, D_MODEL), F32), jax.ShapeDtypeStruct((bsz, seq, D_MODEL), BF16)]
    out_specs = [row(D_MODEL), row(D_MODEL)]
    if route:
        rwh = router_w.astype(BF16)
        rwl = (router_w - rwh.astype(F32)).astype(BF16)
        ins += [rwh, rwl]
        in_specs += [full(rwh), full(rwl)]
        out_shape += [jax.ShapeDtypeStruct((bsz * seq * ROW_SLABS, LANES), F32),
                      jax.ShapeDtypeStruct((bsz, seq, LANES), F32)]
        out_specs += [pl.BlockSpec((tm * ROW_SLABS, LANES), lambda bi, i: (bi * steps + i, 0)), row(LANES)]
    return pl.pallas_call(
        functools.partial(_mix_xattn_kernel, route=route),
        out_shape=tuple(out_shape),
        grid=(bsz, steps),
        in_specs=in_specs,
        out_specs=tuple(out_specs),
        compiler_params=_cparams(("parallel", "arbitrary")),
        name="mix_xattn_ln",
    )(*ins)


def _swiglu_chunk(xb, w1, w3, w2):
    hg = jnp.dot(xb, w1, preferred_element_type=F32)
    hu = jnp.dot(xb, w3, preferred_element_type=F32)
    a = hg * jax.nn.sigmoid(hg) * hu
    return jnp.dot(a.astype(BF16), w2, preferred_element_type=F32)


def _ffn_kernel(xb_ref, x_ref, w1_ref, w3_ref, w2_ref, g_ref, b_ref, y_ref, yb_ref):
    f = _swiglu_chunk(xb_ref[...], w1_ref[...], w3_ref[...], w2_ref[...])
    y = _layer_norm(ALPHA * x_ref[...] + f, g_ref[...], b_ref[...])
    y_ref[...] = y
    yb_ref[...] = y.astype(BF16)


def _ffn_ln(xb, x, w13, w2, g, b, tm=512):
    t = x.shape[0]
    f = w2.shape[0]
    row = pl.BlockSpec((tm, D_MODEL), lambda i: (i, 0))
    vec = lambda a: pl.BlockSpec(a.shape, lambda i: (0, 0))
    weight = lambda shape, col: pl.BlockSpec(shape, lambda i: (0, col), pipeline_mode=pl.Buffered(1))
    return pl.pallas_call(
        _ffn_kernel,
        out_shape=(jax.ShapeDtypeStruct((t, D_MODEL), F32), jax.ShapeDtypeStruct((t, D_MODEL), BF16)),
        grid=(t // tm,),
        in_specs=[row, row, weight((D_MODEL, f), 0), weight((D_MODEL, f), 1), weight((f, D_MODEL), 0),
                  vec(g), vec(b)],
        out_specs=(row, row),
        compiler_params=_cparams(("parallel",)),
        name="ffn_ln",
    )(xb, x, w13, w13, w2, g, b)


def _dispatch_plan(route, tm):
    t = route.shape[0]
    expert = route[:, 2:4].astype(jnp.int32).reshape(-1)
    onehot = (expert[:, None] == jnp.arange(N_EXPERTS, dtype=jnp.int32)[None, :]).astype(jnp.int32)
    csum = jnp.cumsum(onehot, axis=0)
    rank = jnp.sum((csum - 1) * onehot, axis=1)
    counts = csum[-1]
    padded = (counts + tm - 1) // tm * tm
    ends = jnp.cumsum(padded)
    dest = (ends - padded)[expert] + rank
    n_tiles = TOP_K * t // tm + N_EXPERTS
    src = jnp.zeros((n_tiles * tm,), jnp.int32).at[dest].set(
        jnp.arange(TOP_K * t, dtype=jnp.int32) // TOP_K, unique_indices=True, mode="promise_in_bounds")
    tile_start = jnp.arange(n_tiles, dtype=jnp.int32) * tm
    tile_expert = jnp.sum((tile_start[:, None] >= ends[None, :]).astype(jnp.int32), axis=1)
    tile_expert = jnp.minimum(tile_expert, N_EXPERTS - 1)
    tile_used = (tile_start < ends[-1]).astype(jnp.int32)
    return src, dest.reshape(t, TOP_K).astype(jnp.int32), tile_expert, tile_used


ROW_SLABS = D_MODEL // LANES


def _slab(ref, s, rows):
    return ref.at[pl.ds(s, rows, stride=ROW_SLABS), :]


def _row_tile(ref, r):
    return ref.at[pl.ds(r * ROW_SLABS, ROW_SLABS), :]


GATHER_PRIORITY = 1


def _moe_ffn_kernel(te_ref, tu_ref, src_ref, nxt_ref, x_hbm, w1_ref, w3_ref, w2_ref, y_ref,
                    xbuf, xs_ref, acc_ref, sem, *, tm):
    i = pl.program_id(0)
    j = pl.program_id(1)
    n_tiles = pl.num_programs(0)
    last = pl.num_programs(1) - 1
    slot = i % 2

    def start_rows(idx_ref, dst_slot):
        def issue(r, carry):
            pltpu.make_async_copy(_row_tile(x_hbm, idx_ref[0, 0, r]), _row_tile(xbuf.at[dst_slot], r),
                                  sem.at[dst_slot]).start(priority=GATHER_PRIORITY)
            return carry
        lax.fori_loop(0, tm, issue, 0, unroll=8)

    @pl.when(j == 0)
    def _():
        @pl.when((i == 0) & (tu_ref[0] > 0))
        def _():
            start_rows(src_ref, 0)

        @pl.when(tu_ref[i] > 0)
        def _():
            pltpu.make_async_copy(x_hbm.at[pl.ds(0, tm * ROW_SLABS), :], xbuf.at[slot], sem.at[slot]).wait()
            for s in range(ROW_SLABS):
                xs_ref[:, s * LANES:(s + 1) * LANES] = _slab(xbuf.at[slot], s, tm)[...].astype(xs_ref.dtype)

        @pl.when((i + 1 < n_tiles) & (tu_ref[jnp.minimum(i + 1, n_tiles - 1)] > 0))
        def _():
            start_rows(nxt_ref, 1 - slot)

        acc_ref[...] = jnp.zeros_like(acc_ref)

    @pl.when(tu_ref[i] > 0)
    def _():
        acc_ref[...] += _swiglu_chunk(xs_ref[...], w1_ref[0], w3_ref[0], w2_ref[0])

    @pl.when(j == last)
    def _():
        for s in range(ROW_SLABS):
            _slab(y_ref, s, tm)[...] = acc_ref[:, s * LANES:(s + 1) * LANES]


def _moe_ffn(tile_expert, tile_used, src, x3, w13, w2, tm, tf):
    n_tiles = src.shape[0]
    nf = w2.shape[1] // tf
    chunk = lambda i, j, tu: jnp.where(tu[i] > 0, j, nf - 1)
    idx = lambda shift: pl.BlockSpec((1, 1, tm), lambda i, j, te, tu: (jnp.minimum(i + shift, n_tiles - 1), 0, 0),
                                     memory_space=pltpu.SMEM)
    return pl.pallas_call(
        functools.partial(_moe_ffn_kernel, tm=tm),
        out_shape=jax.ShapeDtypeStruct((n_tiles * tm * ROW_SLABS, LANES), F32),
        grid_spec=pltpu.PrefetchScalarGridSpec(
            num_scalar_prefetch=2, grid=(n_tiles, nf),
            in_specs=[idx(0), idx(1), pl.BlockSpec(memory_space=pl.ANY),
                      pl.BlockSpec((1, D_MODEL, tf), lambda i, j, te, tu: (te[i], 0, chunk(i, j, tu))),
                      pl.BlockSpec((1, D_MODEL, tf), lambda i, j, te, tu: (te[i], 0, chunk(i, j, tu) + nf)),
                      pl.BlockSpec((1, tf, D_MODEL), lambda i, j, te, tu: (te[i], chunk(i, j, tu), 0))],
            out_specs=pl.BlockSpec((tm * ROW_SLABS, LANES), lambda i, j, te, tu: (i, 0)),
            scratch_shapes=[pltpu.VMEM((2, tm * ROW_SLABS, LANES), F32), pltpu.VMEM((tm, D_MODEL), BF16),
                            pltpu.VMEM((tm, D_MODEL), F32), pltpu.SemaphoreType.DMA((2,))]),
        compiler_params=_cparams(("arbitrary", "arbitrary")),
        name="moe_ffn",
    )(tile_expert, tile_used, src, src, x3, w13, w13, w2)


def _combine_kernel(d0_ref, d1_ref, n0_ref, n1_ref, y_hbm, route_ref, x_ref, g_ref, b_ref, o_ref, ob_ref,
                    buf, sem, *, tm):
    i = pl.program_id(0)
    slot = i % 2

    def start_rows(a_ref, b_ref, dst_slot):
        def issue(r, carry):
            pltpu.make_async_copy(_row_tile(y_hbm, a_ref[0, 0, r]), _row_tile(buf.at[dst_slot, 0], r),
                                  sem.at[dst_slot, 0]).start(priority=0)
            pltpu.make_async_copy(_row_tile(y_hbm, b_ref[0, 0, r]), _row_tile(buf.at[dst_slot, 1], r),
                                  sem.at[dst_slot, 1]).start(priority=1)
            return carry
        lax.fori_loop(0, tm, issue, 0, unroll=8)

    @pl.when(i == 0)
    def _():
        start_rows(d0_ref, d1_ref, 0)

    for k in range(TOP_K):
        pltpu.make_async_copy(y_hbm.at[pl.ds(0, tm * ROW_SLABS), :], buf.at[slot, k], sem.at[slot, k]).wait()

    @pl.when(i + 1 < pl.num_programs(0))
    def _():
        start_rows(n0_ref, n1_ref, 1 - slot)

    route = route_ref[...]
    w0, w1 = route[:, 0:1], route[:, 1:2]
    f = jnp.concatenate([w0 * _slab(buf.at[slot, 0], s, tm)[...] + w1 * _slab(buf.at[slot, 1], s, tm)[...]
                         for s in range(ROW_SLABS)], axis=1)
    y = _layer_norm(ALPHA * x_ref[...] + f, g_ref[...], b_ref[...])
    o_ref[...] = y
    ob_ref[...] = y.astype(BF16)


def _combine_ln(dest, ys, route, x, g, b, tm=512):
    t = x.shape[0]
    steps = t // tm
    row = pl.BlockSpec((tm, D_MODEL), lambda i: (i, 0))
    vec = lambda a: pl.BlockSpec(a.shape, lambda i: (0, 0))
    idx = lambda shift: pl.BlockSpec((1, 1, tm), lambda i: (jnp.minimum(i + shift, steps - 1), 0, 0),
                                     memory_space=pltpu.SMEM)
    d0, d1 = dest[:, 0].reshape(-1, 1, tm), dest[:, 1].reshape(-1, 1, tm)
    return pl.pallas_call(
        functools.partial(_combine_kernel, tm=tm),
        out_shape=(jax.ShapeDtypeStruct((t, D_MODEL), F32), jax.ShapeDtypeStruct((t, D_MODEL), BF16)),
        grid=(steps,),
        in_specs=[idx(0), idx(0), idx(1), idx(1), pl.BlockSpec(memory_space=pl.ANY),
                  pl.BlockSpec((tm, LANES), lambda i: (i, 0)),
                  row, vec(g), vec(b)],
        out_specs=(row, row),
        scratch_shapes=[pltpu.VMEM((2, TOP_K, tm * ROW_SLABS, LANES), F32), pltpu.SemaphoreType.DMA((2, TOP_K))],
        compiler_params=_cparams(("arbitrary",)),
        name="moe_combine_ln",
    )(d0, d1, d0, d1, ys, route, x, g, b)


def _moe_ln(x, x_slab, route, w13, w2, g, b, tm=512, tf=1792):
    src, dest, tile_expert, tile_used = _dispatch_plan(route, tm)
    ys = _moe_ffn(tile_expert, tile_used, src.reshape(-1, 1, tm), x_slab, w13, w2, tm, tf)
    return _combine_ln(dest, ys, route, x, g, b)


def _pad_heads(w, heads, width, padded, axis):
    shape = list(w.shape)
    shape[axis:axis + 1] = [heads, width]
    w = w.reshape(shape)
    pad = [(0, 0)] * w.ndim
    pad[axis + 1] = (0, padded - width)
    w = jnp.pad(w, pad)
    shape[axis:axis + 2] = [heads * padded]
    return w.reshape(shape)


def _layout_w_in(w):
    bounds = np.cumsum([0, 256, 256, 256, 384, 384, 384, 192, 192, 384, 384, 16])
    qa, ka, va, qb, kb, vb, qc, kc, vc, gc, rc = [w[:, bounds[i]:bounds[i + 1]] for i in range(11)]
    rc = jnp.pad(rc, ((0, 0), (0, LANES - C_RANK)))
    qc = _pad_heads(qc, C_HEADS, C_K, C_KP, 1)
    kc = _pad_heads(kc, C_HEADS, C_K, C_KP, 1)
    vc = _pad_heads(vc, C_HEADS, C_V, C_VP, 1)
    gc = _pad_heads(gc, C_HEADS, C_V, C_VP, 1)
    return jnp.concatenate([qa, ka, qb, kb, va, vb, rc, qc, kc, vc, gc], axis=1).astype(BF16)


def _rope_tables(positions, dim):
    rot = dim // ROPE_FRAC
    inv_freq = ROPE_THETA ** (-jnp.arange(0, rot, 2, dtype=F32) / rot)
    ang = positions.astype(F32).reshape(-1, 1) * inv_freq
    cos, sin = jnp.cos(ang), jnp.sin(ang)
    t = ang.shape[0]
    c = jnp.concatenate([cos, cos, jnp.ones((t, dim - rot), F32)], axis=1)
    s = jnp.concatenate([-sin, sin, jnp.zeros((t, dim - rot), F32)], axis=1)
    reps = LANES // dim
    return jnp.tile(c, (1, reps)), jnp.tile(s, (1, reps))


def kernel(x, mem, positions, w_in, lam_q1, lam_k1, lam_q2, lam_k2, diff_norm_g, gla_wa2, gla_ba, gla_norm_g,
           w_out, ln_mix_g, ln_mix_b, xa_wq, xa_wkv, xa_wo, ln_xa_g, ln_xa_b, ffd_w13, ffd_w2, moe_router,
           moe_w13, moe_w2, ln_ffn_g, ln_ffn_b):
    bsz, seq, _ = x.shape
    t = bsz * seq
    ca, sa = _rope_tables(positions, A_QK)
    cb, sb = _rope_tables(positions, B_DIM)
    tabs = (ca, sa, cb, sb)
    memb = mem.reshape(-1, D_MODEL).astype(BF16)
    mem_len = mem.shape[1]
    row = lambda v: v.reshape(1, -1).astype(F32)

    xf = x.reshape(t, D_MODEL)
    xb = xf.astype(BF16)
    for l in range(DEPTH):
        lam_init = 0.8 - 0.6 * math.exp(-0.3 * l)
        lam = (jnp.exp(jnp.sum(lam_q1[l] * lam_k1[l])) - jnp.exp(jnp.sum(lam_q2[l] * lam_k2[l]))
               + lam_init).reshape(1, 1).astype(F32)
        h = _inproj(xb, _layout_w_in(w_in[l]), tabs)
        h3 = h.reshape(bsz, seq, H_COLS)
        oa = _diff_attention(h3, lam, jnp.tile(row(diff_norm_g[l]), (1, 2)), lam_init)
        ob = _dilated_attention(h3)
        wa = _pad_heads(gla_wa2[l], C_HEADS, C_K, C_KP, 1)
        wa = jnp.pad(wa, ((0, LANES - C_RANK), (0, 0))).reshape(LANES, 2, LANES).transpose(1, 0, 2).astype(BF16)
        ba = _pad_heads(gla_ba[l], C_HEADS, C_K, C_KP, 0).reshape(2, 1, LANES).astype(F32)
        gn = jnp.pad(row(gla_norm_g[l]), ((0, 0), (0, C_VP - C_V)))
        oc = _gla(h3, wa, ba, gn)
        wo = w_out[l]
        wo_a = wo[:A_HEADS * A_V].astype(BF16)
        wo_b = wo[A_HEADS * A_V:A_HEADS * A_V + B_HEADS * B_DIM].astype(BF16)
        wo_c = _pad_heads(wo[A_HEADS * A_V + B_HEADS * B_DIM:], C_HEADS, C_V, C_VP, 0).astype(BF16)
        kv = _matmul(memb, xa_wkv[l].astype(BF16), BF16).reshape(bsz, mem_len, 2 * D_MODEL)
        moe = l % 2 == 1
        rw = jnp.pad(moe_router[l // 2], ((0, 0), (0, LANES - N_EXPERTS))) if moe else None
        outs = _mix_xattn_ln(oa, ob, oc, (wo_a, wo_b, wo_c), xf.reshape(bsz, seq, D_MODEL),
                             (row(ln_mix_g[l]), row(ln_mix_b[l])), xa_wq[l].astype(BF16), kv, xa_wo[l].astype(BF16),
                             (row(ln_xa_g[l]), row(ln_xa_b[l])), router_w=rw)
        xf, xb = outs[0].reshape(t, D_MODEL), outs[1].reshape(t, D_MODEL)
        if moe:
            xf, xb = _moe_ln(xf, outs[2], outs[3].reshape(t, LANES), moe_w13[l // 2].astype(BF16),
                             moe_w2[l // 2].astype(BF16), row(ln_ffn_g[l]), row(ln_ffn_b[l]))
        else:
            xf, xb = _ffn_ln(xb, xf, ffd_w13[l // 2].astype(BF16), ffd_w2[l // 2].astype(BF16),
                             row(ln_ffn_g[l]), row(ln_ffn_b[l]))
    return xf.reshape(bsz, seq, D_MODEL)
```

```python
import functools
import math

import numpy as np
import jax
import jax.numpy as jnp
from jax import lax
from jax.experimental import pallas as pl
from jax.experimental.pallas import tpu as pltpu

F32 = jnp.float32
BF16 = jnp.bfloat16

LANES = 128
VMEM_LIMIT_BYTES = 56 * 1024 * 1024

D_MODEL = 1024
DEPTH = 2
ROPE_THETA = 500000.0
ROPE_FRAC = 4
LN_EPS = 1e-5
NEG_INF = -1e30
ALPHA = (2 * DEPTH) ** 0.25
LOG2_E = math.log2(math.e)

A_HEADS, A_QK, A_V = 4, 32, 64
B_HEADS, B_DIM = 6, 64
B_SPAN = 128
B_DILATIONS = (1, 4, 16)
C_HEADS, C_K, C_V = 4, 48, 96
C_KP, C_VP = 64, 128
C_RANK, C_TAU, C_CHUNK = 16, 16.0, 16
XA_HEADS, XA_DIM = 4, 256
N_EXPERTS, TOP_K = 8, 2

QA, KA, QB, KB, VA, VB, RC, QC, KC, VC, GC, H_COLS = (
    0, 256, 512, 896, 1280, 1536, 1920, 2048, 2304, 2560, 3072, 3584)


def _cparams(sem):
    return pltpu.CompilerParams(dimension_semantics=sem, vmem_limit_bytes=VMEM_LIMIT_BYTES)


def _layer_norm(y, g, b):
    mu = jnp.mean(y, axis=-1, keepdims=True)
    yc = y - mu
    var = jnp.mean(yc * yc, axis=-1, keepdims=True)
    return yc * lax.rsqrt(var + LN_EPS) * g + b


def _dot_nt(a, b):
    return lax.dot_general(a, b, (((1,), (1,)), ((), ())), preferred_element_type=F32)


def _inproj_kernel(x_ref, w_ref, ca_ref, sa_ref, cb_ref, sb_ref, o_ref):
    x = x_ref[...]
    lane = lax.broadcasted_iota(jnp.int32, (x.shape[0], LANES), 1)

    def rope_cols(lo, k_lo, hi, c_ref, s_ref, half, period):
        acc = jnp.dot(x, w_ref[:, lo:hi], preferred_element_type=F32)
        c, s = c_ref[...], s_ref[...]
        first = lane % period < half
        for k in range((hi - lo) // LANES):
            a = acc[:, k * LANES:(k + 1) * LANES]
            up = pltpu.roll(a, LANES - half, 1)
            dn = pltpu.roll(a, half, 1)
            r = a * c + jnp.where(first, up, dn) * s
            if lo + k * LANES < k_lo:
                r = r * (period ** -0.5 * LOG2_E)
            o_ref[:, lo + k * LANES:lo + (k + 1) * LANES] = r.astype(o_ref.dtype)

    rope_cols(QA, KA, QB, ca_ref, sa_ref, A_QK // ROPE_FRAC // 2, A_QK)
    rope_cols(QB, KB, VA, cb_ref, sb_ref, B_DIM // ROPE_FRAC // 2, B_DIM)
    o_ref[:, VA:] = jnp.dot(x, w_ref[:, VA:], preferred_element_type=F32).astype(o_ref.dtype)


def _inproj(xb, w, tabs, tm=512):
    t = xb.shape[0]
    tab_spec = pl.BlockSpec((tm, LANES), lambda i: (i, 0))
    return pl.pallas_call(
        _inproj_kernel,
        out_shape=jax.ShapeDtypeStruct((t, H_COLS), BF16),
        grid=(t // tm,),
        in_specs=[pl.BlockSpec((tm, D_MODEL), lambda i: (i, 0)),
                  pl.BlockSpec((D_MODEL, H_COLS), lambda i: (0, 0)),
                  tab_spec, tab_spec, tab_spec, tab_spec],
        out_specs=pl.BlockSpec((tm, H_COLS), lambda i: (i, 0)),
        compiler_params=_cparams(("parallel",)),
        name="inproj_rope",
    )(xb, w, *tabs)


def _diff_kernel(lam_ref, q_ref, k_ref, v_ref, g_ref, o_ref, *, seq, tq, out_scale):
    lane = lax.broadcasted_iota(jnp.int32, (tq, LANES), 1)
    row = lax.broadcasted_iota(jnp.int32, (tq, tq), 0)
    col = lax.broadcasted_iota(jnp.int32, (tq, tq), 1)
    causal = col <= row
    lo = lane < A_V
    lam = lam_ref[0, 0]

    for i in range(seq // tq):
        past = i * tq
        q = q_ref[0, past:past + tq, :]
        k_d = k_ref[0, past:past + tq, :]
        v_d = v_ref[0, past:past + tq, :]

        def attend(c):
            qm = jnp.where(lane // A_QK == c, q, jnp.zeros_like(q))
            s_d = jnp.where(causal, _dot_nt(qm, k_d), NEG_INF)
            m = jnp.max(s_d, axis=-1, keepdims=True)
            if past:
                s_p = _dot_nt(qm, k_ref[0, :past, :])
                m = jnp.maximum(m, jnp.max(s_p, axis=-1, keepdims=True))
            p_d = jnp.exp2(s_d - m)
            l = jnp.sum(p_d, axis=-1, keepdims=True)
            acc = jnp.dot(p_d.astype(BF16), v_d, preferred_element_type=F32)
            if past:
                p_p = jnp.exp2(s_p - m)
                l = l + jnp.sum(p_p, axis=-1, keepdims=True)
                acc = acc + jnp.dot(p_p.astype(BF16), v_ref[0, :past, :], preferred_element_type=F32)
            return acc / l

        o_h0 = attend(0) - lam * attend(1)
        o_h1 = attend(2) - lam * attend(3)
        o = jnp.where(lo, o_h0, o_h1)
        sq = o * o
        s_lo = jnp.sum(jnp.where(lo, sq, 0.0), axis=-1, keepdims=True)
        s_all = jnp.sum(sq, axis=-1, keepdims=True)
        ms = jnp.where(lo, s_lo, s_all - s_lo) * (1.0 / A_V)
        o_ref[0, past:past + tq, :] = (o * lax.rsqrt(ms + LN_EPS) * g_ref[...] * out_scale).astype(o_ref.dtype)


def _diff_attention(h3, lam, g, lam_init, tq=512):
    bsz, seq, _ = h3.shape
    kern = functools.partial(_diff_kernel, seq=seq, tq=tq, out_scale=1.0 - lam_init)
    blk = lambda off: pl.BlockSpec((1, seq, LANES), lambda b, p: (b, 0, off // LANES + p))
    return pl.pallas_call(
        kern,
        out_shape=jax.ShapeDtypeStruct((bsz, seq, A_HEADS * A_V), BF16),
        grid=(bsz, A_HEADS // 2),
        in_specs=[pl.BlockSpec(memory_space=pltpu.SMEM), blk(QA), blk(KA), blk(VA),
                  pl.BlockSpec((1, LANES), lambda b, p: (0, 0))],
        out_specs=pl.BlockSpec((1, seq, LANES), lambda b, p: (b, 0, p)),
        compiler_params=_cparams(("parallel", "parallel")),
        name="diff_attention",
    )(lam, h3, h3, h3, g)


def _dil_kernel(q_ref, k_ref, v_ref, o_ref, qf, kf, vf, qs, ks, vs, m_s, l_s, a_s, *, seq):
    qf[...] = q_ref[0].astype(F32)
    kf[...] = k_ref[0].astype(F32)
    vf[...] = v_ref[0].astype(F32)
    span = B_SPAN
    lo = lax.broadcasted_iota(jnp.int32, (span, LANES), 1) < B_DIM
    own = ((lax.broadcasted_iota(jnp.int32, (2 * span, LANES), 1) < B_DIM)
           == (lax.broadcasted_iota(jnp.int32, (2 * span, LANES), 0) < span))
    qi = lax.broadcasted_iota(jnp.int32, (2 * span, 2 * span), 0) % span
    ki = lax.broadcasted_iota(jnp.int32, (2 * span, 2 * span), 1)
    band = (ki >= qi) & (ki <= qi + span)
    first = (lax.broadcasted_iota(jnp.int32, (2 * span, span), 1)
             <= lax.broadcasted_iota(jnp.int32, (2 * span, span), 0) % span)

    for g, r in enumerate(B_DILATIONS):
        sub = seq // r
        for rho in range(r):
            rows = pl.ds(rho, sub, stride=r) if r > 1 else pl.ds(0, sub)
            qs[rho * sub:(rho + 1) * sub, :] = qf[rows, :].astype(BF16)
            ks[rho * sub:(rho + 1) * sub, :] = kf[rows, :].astype(BF16)
            vs[rho * sub:(rho + 1) * sub, :] = vf[rows, :].astype(BF16)
        for rho in range(r):
            for n in range(sub // span):
                base = rho * sub + n * span
                qb = qs[base:base + span, :]
                lhs = jnp.where(own, jnp.concatenate([qb, qb], axis=0), jnp.zeros((), BF16))
                start = base - span if n else base
                s = jnp.where(band if n else first, _dot_nt(lhs, ks[start:base + span, :]), NEG_INF)
                m = jnp.max(s, axis=-1, keepdims=True)
                p = jnp.exp2(s - m)
                l = jnp.sum(p, axis=-1, keepdims=True)
                a = jnp.dot(p.astype(BF16), vs[start:base + span, :], preferred_element_type=F32)
                m = jnp.broadcast_to(m, (2 * span, LANES))
                l = jnp.broadcast_to(l, (2 * span, LANES))
                m_g = jnp.where(lo, m[:span], m[span:])
                l_g = jnp.where(lo, l[:span], l[span:])
                a_g = jnp.where(lo, a[:span], a[span:])
                nat = pl.ds(rho + r * span * n, span, stride=r) if r > 1 else pl.ds(n * span, span)
                if g == 0:
                    m_s[nat, :] = m_g
                    l_s[nat, :] = l_g
                    a_s[nat, :] = a_g
                else:
                    m_o = m_s[nat, :]
                    m_n = jnp.maximum(m_o, m_g)
                    w_o = jnp.exp2(m_o - m_n)
                    w_g = jnp.exp2(m_g - m_n)
                    m_s[nat, :] = m_n
                    l_s[nat, :] = w_o * l_s[nat, :] + w_g * l_g
                    a_s[nat, :] = w_o * a_s[nat, :] + w_g * a_g

    o_ref[0] = (a_s[...] / l_s[...]).astype(o_ref.dtype)


def _dilated_attention(h3):
    bsz, seq, _ = h3.shape
    blk = lambda off: pl.BlockSpec((1, seq, LANES), lambda b, p: (b, 0, off // LANES + p))
    return pl.pallas_call(
        functools.partial(_dil_kernel, seq=seq),
        out_shape=jax.ShapeDtypeStruct((bsz, seq, B_HEADS * B_DIM), BF16),
        grid=(bsz, B_HEADS // 2),
        in_specs=[blk(QB), blk(KB), blk(VB)],
        out_specs=pl.BlockSpec((1, seq, LANES), lambda b, p: (b, 0, p)),
        scratch_shapes=([pltpu.VMEM((seq, LANES), F32)] * 3 + [pltpu.VMEM((seq, LANES), BF16)] * 3
                        + [pltpu.VMEM((seq, LANES), F32)] * 3),
        compiler_params=_cparams(("parallel", "parallel")),
        name="dilated_attention",
    )(h3, h3, h3)


def _gla_kernel(q_ref, k_ref, v_ref, gate_ref, rc_ref, wa_ref, ba_ref, gn_ref, rcat_ref, o_ref, st, *, tb):
    t = pl.program_id(2)

    @pl.when(t == 0)
    def _():
        st[...] = jnp.zeros_like(st)

    ch, half, nc = C_CHUNK, C_CHUNK // 2, tb // C_CHUNK
    z = jnp.dot(rc_ref[0], wa_ref[0], preferred_element_type=F32) + ba_ref[0]
    la = (jnp.minimum(z, 0.0) - jnp.log(1.0 + jnp.exp(-jnp.abs(z)))) * (LOG2_E / C_TAU)
    ri = lax.broadcasted_iota(jnp.int32, (LANES, LANES), 0)
    ci = lax.broadcasted_iota(jnp.int32, (LANES, LANES), 1)
    same = (ri // ch) == (ci // ch)
    tri = jnp.where(same & (ci <= ri), 1.0, 0.0).astype(F32)
    blk = jnp.where(same, 1.0, 0.0).astype(F32)
    sums = lambda m: jnp.concatenate(
        [jnp.dot(m, la[r:r + LANES], preferred_element_type=F32, precision=lax.Precision.HIGHEST)
         for r in range(0, tb, LANES)], axis=0)
    b = sums(tri)
    bl = sums(blk)
    qs = q_ref[0].astype(F32) * (C_K ** -0.5)
    kf = k_ref[0].astype(F32)
    vb = v_ref[0]
    qt = (qs * jnp.exp2(b)).astype(BF16)
    kt = kf * jnp.exp2(bl - b)
    gl = jnp.exp2(bl)
    head0 = lax.broadcasted_iota(jnp.int32, (tb, LANES), 1) < C_KP
    kt_h = (jnp.where(head0, kt, 0.0).astype(BF16), jnp.where(head0, 0.0, kt).astype(BF16))

    def chunk_row(x, j, rows):
        xj = x.reshape(nc, ch, LANES)[:, j:j + 1, :]
        return jnp.broadcast_to(xj, (nc, rows, LANES)).reshape(nc * rows, LANES)

    def upper(x):
        return x.reshape(nc, 2, half, LANES)[:, 1].reshape(nc * half, LANES)

    rowi = lax.broadcasted_iota(jnp.int32, (tb, LANES), 0) % ch
    rowu = lax.broadcasted_iota(jnp.int32, (nc * half, LANES), 0) % half + half
    b_u, qs_u = upper(b), upper(qs)
    terms = []
    for j in range(ch):
        if j < half:
            e = jnp.exp2(jnp.where(rowi >= j, b - chunk_row(b, j, ch), NEG_INF))
            term = qs * e * chunk_row(kf, j, ch)
        else:
            e = jnp.exp2(jnp.where(rowu >= j, b_u - chunk_row(b, j, half), NEG_INF))
            tu = (qs_u * e * chunk_row(kf, j, half)).reshape(nc, 1, half, LANES)
            term = jnp.concatenate([jnp.zeros_like(tu), tu], axis=1).reshape(tb, LANES)
        terms.append(term.astype(BF16))
    s_tiled = jnp.dot(jnp.concatenate(terms, axis=1), rcat_ref[...], preferred_element_type=F32)
    rr = lax.broadcasted_iota(jnp.int32, (tb, 2 * LANES), 0)
    cc = lax.broadcasted_iota(jnp.int32, (tb, 2 * LANES), 1)
    m = jnp.where((rr % LANES) // ch == (cc % LANES) // ch, s_tiled, 0.0).astype(BF16)
    o_intra = jnp.concatenate(
        [jnp.concatenate([jnp.dot(m[r:r + LANES, h * LANES:(h + 1) * LANES], vb[r:r + LANES, h * C_VP:(h + 1) * C_VP],
                                  preferred_element_type=F32) for h in range(2)], axis=1)
         for r in range(0, tb, LANES)], axis=0)

    chunks = [slice(c * ch, (c + 1) * ch) for c in range(nc)]
    uts = [jnp.concatenate(
        [lax.dot_general(vb[rows, h * C_VP:(h + 1) * C_VP], kt_h[h][rows], (((0,), (0,)), ((), ())),
                         preferred_element_type=F32) for h in range(2)], axis=0) for rows in chunks]
    s_prev = st[...]
    states = []
    for c in range(nc):
        states.append(s_prev.astype(BF16))
        s_prev = gl[c * ch:c * ch + 1, :] * s_prev + uts[c]
    st[...] = s_prev
    o_inter = [_dot_nt(qt[rows], states[c]) for c, rows in enumerate(chunks)]

    o = o_intra + jnp.concatenate(o_inter, axis=0)
    gate = gate_ref[0].astype(F32)
    outs = []
    for hh in range(2):
        oh = o[:, hh * C_VP:(hh + 1) * C_VP]
        ms = jnp.sum(oh * oh, axis=-1, keepdims=True) * (1.0 / C_V)
        gh = gate[:, hh * C_VP:(hh + 1) * C_VP]
        outs.append(oh * lax.rsqrt(ms + LN_EPS) * gn_ref[...] * (gh * jax.nn.sigmoid(gh)))
    o_ref[0] = jnp.concatenate(outs, axis=1).astype(o_ref.dtype)


def _gla_reduce_matrix():
    r = np.arange(C_CHUNK * LANES)[:, None]
    c = np.arange(2 * LANES)[None, :]
    hit = ((r % LANES) // C_KP == c // LANES) & (r // LANES == (c % LANES) % C_CHUNK)
    return jnp.asarray(hit, dtype=BF16)


def _gla(h3, wa, ba, gn, tb=1024):
    bsz, seq, _ = h3.shape
    return pl.pallas_call(
        functools.partial(_gla_kernel, tb=tb),
        out_shape=jax.ShapeDtypeStruct((bsz, seq, C_HEADS * C_VP), BF16),
        grid=(bsz, C_HEADS // 2, seq // tb),
        in_specs=[pl.BlockSpec((1, tb, LANES), lambda b, p, t: (b, t, QC // LANES + p)),
                  pl.BlockSpec((1, tb, LANES), lambda b, p, t: (b, t, KC // LANES + p)),
                  pl.BlockSpec((1, tb, 2 * C_VP), lambda b, p, t: (b, t, VC // (2 * C_VP) + p)),
                  pl.BlockSpec((1, tb, 2 * C_VP), lambda b, p, t: (b, t, GC // (2 * C_VP) + p)),
                  pl.BlockSpec((1, tb, LANES), lambda b, p, t: (b, t, RC // LANES)),
                  pl.BlockSpec((1, LANES, LANES), lambda b, p, t: (p, 0, 0)),
                  pl.BlockSpec((1, 1, LANES), lambda b, p, t: (p, 0, 0)),
                  pl.BlockSpec((1, C_VP), lambda b, p, t: (0, 0)),
                  pl.BlockSpec((C_CHUNK * LANES, 2 * LANES), lambda b, p, t: (0, 0))],
        out_specs=pl.BlockSpec((1, tb, 2 * C_VP), lambda b, p, t: (b, t, p)),
        scratch_shapes=[pltpu.VMEM((2 * C_VP, LANES), F32)],
        compiler_params=_cparams(("parallel", "parallel", "arbitrary")),
        name="gla",
    )(h3, h3, h3, h3, h3, wa, ba, gn, _gla_reduce_matrix())


def _matmul_kernel(a_ref, w_ref, o_ref):
    o_ref[...] = jnp.dot(a_ref[...], w_ref[...], preferred_element_type=F32).astype(o_ref.dtype)


def _matmul(a, w, out_dtype, tm=512):
    m, k = a.shape
    n = w.shape[1]
    tm = min(tm, m)
    return pl.pallas_call(
        _matmul_kernel,
        out_shape=jax.ShapeDtypeStruct((m, n), out_dtype),
        grid=(m // tm,),
        in_specs=[pl.BlockSpec((tm, k), lambda i: (i, 0)), pl.BlockSpec((k, n), lambda i: (0, 0))],
        out_specs=pl.BlockSpec((tm, n), lambda i: (i, 0)),
        compiler_params=_cparams(("parallel",)),
        name="matmul",
    )(a, w)


def _route_top2(x, xh, rwh, rwl):
    xl = (x - xh.astype(F32)).astype(BF16)
    logits = (jnp.dot(xh, rwh, preferred_element_type=F32) + jnp.dot(xh, rwl, preferred_element_type=F32)
              + jnp.dot(xl, rwh, preferred_element_type=F32))
    lane = lax.broadcasted_iota(jnp.int32, logits.shape, 1)
    logits = jnp.where(lane < N_EXPERTS, logits, -jnp.inf)
    v1 = jnp.max(logits, axis=-1, keepdims=True)
    i1 = jnp.min(jnp.where(logits == v1, lane, LANES), axis=-1, keepdims=True)
    rest = jnp.where(lane == i1, -jnp.inf, logits)
    v2 = jnp.max(rest, axis=-1, keepdims=True)
    i2 = jnp.min(jnp.where(rest == v2, lane, LANES), axis=-1, keepdims=True)
    e2 = jnp.exp(v2 - v1)
    w1 = 1.0 / (1.0 + e2)
    w2 = e2 / (1.0 + e2)
    return (jnp.where(lane == 0, w1, 0.0) + jnp.where(lane == 1, w2, 0.0)
            + jnp.where(lane == 2, i1.astype(F32), 0.0) + jnp.where(lane == 3, i2.astype(F32), 0.0))


def _mix_xattn_kernel(oa_ref, ob_ref, oc_ref, wa_ref, wb_ref, wc_ref, x_ref, g1_ref, b1_ref,
                      wq_ref, k_ref, v_ref, wo_ref, g2_ref, b2_ref, *rest, route):
    if route:
        rwh_ref, rwl_ref, y_ref, yb_ref, ys_ref, route_ref = rest
    else:
        y_ref, yb_ref = rest
    acc = jnp.dot(oa_ref[0], wa_ref[...], preferred_element_type=F32)
    acc += jnp.dot(ob_ref[0], wb_ref[...], preferred_element_type=F32)
    acc += jnp.dot(oc_ref[0], wc_ref[...], preferred_element_type=F32)
    x1 = _layer_norm(ALPHA * x_ref[0] + acc, g1_ref[...], b1_ref[...])
    q = jnp.dot(x1.astype(BF16), wq_ref[...], preferred_element_type=F32).astype(BF16)
    outs = []
    for h in range(XA_HEADS):
        cols = slice(h * XA_DIM, (h + 1) * XA_DIM)
        s = _dot_nt(q[:, cols], k_ref[0, :, cols]) * (XA_DIM ** -0.5)
        m = jnp.max(s, axis=-1, keepdims=True)
        p = jnp.exp(s - m)
        l = jnp.sum(p, axis=-1, keepdims=True)
        o = jnp.dot(p.astype(BF16), v_ref[0, :, cols], preferred_element_type=F32)
        outs.append((o / l).astype(BF16))
    xa = jnp.dot(jnp.concatenate(outs, axis=1), wo_ref[...], preferred_element_type=F32)
    y = _layer_norm(ALPHA * x1 + xa, g2_ref[...], b2_ref[...])
    yb = y.astype(BF16)
    y_ref[0] = y
    yb_ref[0] = yb
    if route:
        route_ref[0] = _route_top2(y, yb, rwh_ref[...], rwl_ref[...])
        for s in range(ROW_SLABS):
            _slab(ys_ref, s, y.shape[0])[...] = y[:, s * LANES:(s + 1) * LANES]


def _mix_xattn_ln(oa, ob, oc, wo_parts, x3, ln1, wq, kv3, wo, ln2, router_w=None, tm=512):
    bsz, seq, _ = x3.shape
    mem_len = kv3.shape[1]
    route = router_w is not None
    row = lambda n: pl.BlockSpec((1, tm, n), lambda bi, i: (bi, i, 0))
    full = lambda a: pl.BlockSpec(a.shape, lambda bi, i: (0,) * a.ndim)
    steps = seq // tm
    ins = [oa, ob, oc, *wo_parts, x3, *ln1, wq, kv3, kv3, wo, *ln2]
    in_specs = [row(oa.shape[2]), row(ob.shape[2]), row(oc.shape[2]), *[full(w) for w in wo_parts],
                row(D_MODEL), *[full(v) for v in ln1], full(wq),
                pl.BlockSpec((1, mem_len, D_MODEL), lambda bi, i: (bi, 0, 0)),
                pl.BlockSpec((1, mem_len, D_MODEL), lambda bi, i: (bi, 0, 1)),
                full(wo), *[full(v) for v in ln2]]
    out_shape = [jax.ShapeDtypeStruct((bsz, seq, D_MODEL), F32), jax.ShapeDtypeStruct((bsz, seq, D_MODEL), BF16)]
    out_specs = [row(D_MODEL), row(D_MODEL)]
    if route:
        rwh = router_w.astype(BF16)
        rwl = (router_w - rwh.astype(F32)).astype(BF16)
        ins += [rwh, rwl]
        in_specs += [full(rwh), full(rwl)]
        out_shape += [jax.ShapeDtypeStruct((bsz * seq * ROW_SLABS, LANES), F32),
                      jax.ShapeDtypeStruct((bsz, seq, LANES), F32)]
        out_specs += [pl.BlockSpec((tm * ROW_SLABS, LANES), lambda bi, i: (bi * steps + i, 0)), row(LANES)]
    return pl.pallas_call(
        functools.partial(_mix_xattn_kernel, route=route),
        out_shape=tuple(out_shape),
        grid=(bsz, steps),
        in_specs=in_specs,
        out_specs=tuple(out_specs),
        compiler_params=_cparams(("parallel", "arbitrary")),
        name="mix_xattn_ln",
    )(*ins)


def _swiglu_chunk(xb, w1, w3, w2):
    hg = jnp.dot(xb, w1, preferred_element_type=F32)
    hu = jnp.dot(xb, w3, preferred_element_type=F32)
    a = hg * jax.nn.sigmoid(hg) * hu
    return jnp.dot(a.astype(BF16), w2, preferred_element_type=F32)


def _ffn_kernel(xb_ref, x_ref, w1_ref, w3_ref, w2_ref, g_ref, b_ref, y_ref, yb_ref):
    f = _swiglu_chunk(xb_ref[...], w1_ref[...], w3_ref[...], w2_ref[...])
    y = _layer_norm(ALPHA * x_ref[...] + f, g_ref[...], b_ref[...])
    y_ref[...] = y
    yb_ref[...] = y.astype(BF16)


def _ffn_ln(xb, x, w13, w2, g, b, tm=512):
    t = x.shape[0]
    f = w2.shape[0]
    row = pl.BlockSpec((tm, D_MODEL), lambda i: (i, 0))
    vec = lambda a: pl.BlockSpec(a.shape, lambda i: (0, 0))
    weight = lambda shape, col: pl.BlockSpec(shape, lambda i: (0, col), pipeline_mode=pl.Buffered(1))
    return pl.pallas_call(
        _ffn_kernel,
        out_shape=(jax.ShapeDtypeStruct((t, D_MODEL), F32), jax.ShapeDtypeStruct((t, D_MODEL), BF16)),
        grid=(t // tm,),
        in_specs=[row, row, weight((D_MODEL, f), 0), weight((D_MODEL, f), 1), weight((f, D_MODEL), 0),
                  vec(g), vec(b)],
        out_specs=(row, row),
        compiler_params=_cparams(("parallel",)),
        name="ffn_ln",
    )(xb, x, w13, w13, w2, g, b)


def _dispatch_plan(route, tm):
    t = route.shape[0]
    expert = route[:, 2:4].astype(jnp.int32).reshape(-1)
    onehot = (expert[:, None] == jnp.arange(N_EXPERTS, dtype=jnp.int32)[None, :]).astype(jnp.int32)
    csum = jnp.cumsum(onehot, axis=0)
    rank = jnp.sum((csum - 1) * onehot, axis=1)
    counts = csum[-1]
    padded = (counts + tm - 1) // tm * tm
    ends = jnp.cumsum(padded)
    dest = (ends - padded)[expert] + rank
    n_tiles = TOP_K * t // tm + N_EXPERTS
    src = jnp.zeros((n_tiles * tm,), jnp.int32).at[dest].set(
        jnp.arange(TOP_K * t, dtype=jnp.int32) // TOP_K, unique_indices=True, mode="promise_in_bounds")
    tile_start = jnp.arange(n_tiles, dtype=jnp.int32) * tm
    tile_expert = jnp.sum((tile_start[:, None] >= ends[None, :]).astype(jnp.int32), axis=1)
    tile_expert = jnp.minimum(tile_expert, N_EXPERTS - 1)
    tile_used = (tile_start < ends[-1]).astype(jnp.int32)
    return src, dest.reshape(t, TOP_K).astype(jnp.int32), tile_expert, tile_used


ROW_SLABS = D_MODEL // LANES


def _slab(ref, s, rows):
    return ref.at[pl.ds(s, rows, stride=ROW_SLABS), :]


def _row_tile(ref, r):
    return ref.at[pl.ds(r * ROW_SLABS, ROW_SLABS), :]


GATHER_PRIORITY = 1


def _moe_ffn_kernel(te_ref, tu_ref, src_ref, nxt_ref, x_hbm, w1_ref, w3_ref, w2_ref, y_ref,
                    xbuf, xs_ref, acc_ref, sem, *, tm):
    i = pl.program_id(0)
    j = pl.program_id(1)
    n_tiles = pl.num_programs(0)
    last = pl.num_programs(1) - 1
    slot = i % 2

    def start_rows(idx_ref, dst_slot):
        def issue(r, carry):
            pltpu.make_async_copy(_row_tile(x_hbm, idx_ref[0, 0, r]), _row_tile(xbuf.at[dst_slot], r),
                                  sem.at[dst_slot]).start(priority=GATHER_PRIORITY)
            return carry
        lax.fori_loop(0, tm, issue, 0, unroll=8)

    @pl.when(j == 0)
    def _():
        @pl.when((i == 0) & (tu_ref[0] > 0))
        def _():
            start_rows(src_ref, 0)

        @pl.when(tu_ref[i] > 0)
        def _():
            pltpu.make_async_copy(x_hbm.at[pl.ds(0, tm * ROW_SLABS), :], xbuf.at[slot], sem.at[slot]).wait()
            for s in range(ROW_SLABS):
                xs_ref[:, s * LANES:(s + 1) * LANES] = _slab(xbuf.at[slot], s, tm)[...].astype(xs_ref.dtype)

        @pl.when((i + 1 < n_tiles) & (tu_ref[jnp.minimum(i + 1, n_tiles - 1)] > 0))
        def _():
            start_rows(nxt_ref, 1 - slot)

        acc_ref[...] = jnp.zeros_like(acc_ref)

    @pl.when(tu_ref[i] > 0)
    def _():
        acc_ref[...] += _swiglu_chunk(xs_ref[...], w1_ref[0], w3_ref[0], w2_ref[0])

    @pl.when(j == last)
    def _():
        for s in range(ROW_SLABS):
            _slab(y_ref, s, tm)[...] = acc_ref[:, s * LANES:(s + 1) * LANES]


def _moe_ffn(tile_expert, tile_used, src, x3, w13, w2, tm, tf):
    n_tiles = src.shape[0]
    nf = w2.shape[1] // tf
    chunk = lambda i, j, tu: jnp.where(tu[i] > 0, j, nf - 1)
    idx = lambda shift: pl.BlockSpec((1, 1, tm), lambda i, j, te, tu: (jnp.minimum(i + shift, n_tiles - 1), 0, 0),
                                     memory_space=pltpu.SMEM)
    return pl.pallas_call(
        functools.partial(_moe_ffn_kernel, tm=tm),
        out_shape=jax.ShapeDtypeStruct((n_tiles * tm * ROW_SLABS, LANES), F32),
        grid_spec=pltpu.PrefetchScalarGridSpec(
            num_scalar_prefetch=2, grid=(n_tiles, nf),
            in_specs=[idx(0), idx(1), pl.BlockSpec(memory_space=pl.ANY),
                      pl.BlockSpec((1, D_MODEL, tf), lambda i, j, te, tu: (te[i], 0, chunk(i, j, tu))),
                      pl.BlockSpec((1, D_MODEL, tf), lambda i, j, te, tu: (te[i], 0, chunk(i, j, tu) + nf)),
                      pl.BlockSpec((1, tf, D_MODEL), lambda i, j, te, tu: (te[i], chunk(i, j, tu), 0))],
            out_specs=pl.BlockSpec((tm * ROW_SLABS, LANES), lambda i, j, te, tu: (i, 0)),
            scratch_shapes=[pltpu.VMEM((2, tm * ROW_SLABS, LANES), F32), pltpu.VMEM((tm, D_MODEL), BF16),
                            pltpu.VMEM((tm, D_MODEL), F32), pltpu.SemaphoreType.DMA((2,))]),
        compiler_params=_cparams(("arbitrary", "arbitrary")),
        name="moe_ffn",
    )(tile_expert, tile_used, src, src, x3, w13, w13, w2)


def _combine_kernel(d0_ref, d1_ref, n0_ref, n1_ref, y_hbm, route_ref, x_ref, g_ref, b_ref, o_ref, ob_ref,
                    buf, sem, *, tm):
    i = pl.program_id(0)
    slot = i % 2

    def start_rows(a_ref, b_ref, dst_slot):
        def issue(r, carry):
            pltpu.make_async_copy(_row_tile(y_hbm, a_ref[0, 0, r]), _row_tile(buf.at[dst_slot, 0], r),
                                  sem.at[dst_slot, 0]).start(priority=0)
            pltpu.make_async_copy(_row_tile(y_hbm, b_ref[0, 0, r]), _row_tile(buf.at[dst_slot, 1], r),
                                  sem.at[dst_slot, 1]).start(priority=1)
            return carry
        lax.fori_loop(0, tm, issue, 0, unroll=8)

    @pl.when(i == 0)
    def _():
        start_rows(d0_ref, d1_ref, 0)

    for k in range(TOP_K):
        pltpu.make_async_copy(y_hbm.at[pl.ds(0, tm * ROW_SLABS), :], buf.at[slot, k], sem.at[slot, k]).wait()

    @pl.when(i + 1 < pl.num_programs(0))
    def _():
        start_rows(n0_ref, n1_ref, 1 - slot)

    route = route_ref[...]
    w0, w1 = route[:, 0:1], route[:, 1:2]
    f = jnp.concatenate([w0 * _slab(buf.at[slot, 0], s, tm)[...] + w1 * _slab(buf.at[slot, 1], s, tm)[...]
                         for s in range(ROW_SLABS)], axis=1)
    y = _layer_norm(ALPHA * x_ref[...] + f, g_ref[...], b_ref[...])
    o_ref[...] = y
    ob_ref[...] = y.astype(BF16)


def _combine_ln(dest, ys, route, x, g, b, tm=512):
    t = x.shape[0]
    steps = t // tm
    row = pl.BlockSpec((tm, D_MODEL), lambda i: (i, 0))
    vec = lambda a: pl.BlockSpec(a.shape, lambda i: (0, 0))
    idx = lambda shift: pl.BlockSpec((1, 1, tm), lambda i: (jnp.minimum(i + shift, steps - 1), 0, 0),
                                     memory_space=pltpu.SMEM)
    d0, d1 = dest[:, 0].reshape(-1, 1, tm), dest[:, 1].reshape(-1, 1, tm)
    return pl.pallas_call(
        functools.partial(_combine_kernel, tm=tm),
        out_shape=(jax.ShapeDtypeStruct((t, D_MODEL), F32), jax.ShapeDtypeStruct((t, D_MODEL), BF16)),
        grid=(steps,),
        in_specs=[idx(0), idx(0), idx(1), idx(1), pl.BlockSpec(memory_space=pl.ANY),
                  pl.BlockSpec((tm, LANES), lambda i: (i, 0)),
                  row, vec(g), vec(b)],
        out_specs=(row, row),
        scratch_shapes=[pltpu.VMEM((2, TOP_K, tm * ROW_SLABS, LANES), F32), pltpu.SemaphoreType.DMA((2, TOP_K))],
        compiler_params=_cparams(("arbitrary",)),
        name="moe_combine_ln",
    )(d0, d1, d0, d1, ys, route, x, g, b)


def _moe_ln(x, x_slab, route, w13, w2, g, b, tm=512, tf=1792):
    src, dest, tile_expert, tile_used = _dispatch_plan(route, tm)
    ys = _moe_ffn(tile_expert, tile_used, src.reshape(-1, 1, tm), x_slab, w13, w2, tm, tf)
    return _combine_ln(dest, ys, route, x, g, b)


def _pad_heads(w, heads, width, padded, axis):
    shape = list(w.shape)
    shape[axis:axis + 1] = [heads, width]
    w = w.reshape(shape)
    pad = [(0, 0)] * w.ndim
    pad[axis + 1] = (0, padded - width)
    w = jnp.pad(w, pad)
    shape[axis:axis + 2] = [heads * padded]
    return w.reshape(shape)


def _layout_w_in(w):
    bounds = np.cumsum([0, 256, 256, 256, 384, 384, 384, 192, 192, 384, 384, 16])
    qa, ka, va, qb, kb, vb, qc, kc, vc, gc, rc = [w[:, bounds[i]:bounds[i + 1]] for i in range(11)]
    rc = jnp.pad(rc, ((0, 0), (0, LANES - C_RANK)))
    qc = _pad_heads(qc, C_HEADS, C_K, C_KP, 1)
    kc = _pad_heads(kc, C_HEADS, C_K, C_KP, 1)
    vc = _pad_heads(vc, C_HEADS, C_V, C_VP, 1)
    gc = _pad_heads(gc, C_HEADS, C_V, C_VP, 1)
    return jnp.concatenate([qa, ka, qb, kb, va, vb, rc, qc, kc, vc, gc], axis=1).astype(BF16)


def _rope_tables(positions, dim):
    rot = dim // ROPE_FRAC
    inv_freq = ROPE_THETA ** (-jnp.arange(0, rot, 2, dtype=F32) / rot)
    ang = positions.astype(F32).reshape(-1, 1) * inv_freq
    cos, sin = jnp.cos(ang), jnp.sin(ang)
    t = ang.shape[0]
    c = jnp.concatenate([cos, cos, jnp.ones((t, dim - rot), F32)], axis=1)
    s = jnp.concatenate([-sin, sin, jnp.zeros((t, dim - rot), F32)], axis=1)
    reps = LANES // dim
    return jnp.tile(c, (1, reps)), jnp.tile(s, (1, reps))


def kernel(x, mem, positions, w_in, lam_q1, lam_k1, lam_q2, lam_k2, diff_norm_g, gla_wa2, gla_ba, gla_norm_g,
           w_out, ln_mix_g, ln_mix_b, xa_wq, xa_wkv, xa_wo, ln_xa_g, ln_xa_b, ffd_w13, ffd_w2, moe_router,
           moe_w13, moe_w2, ln_ffn_g, ln_ffn_b):
    bsz, seq, _ = x.shape
    t = bsz * seq
    ca, sa = _rope_tables(positions, A_QK)
    cb, sb = _rope_tables(positions, B_DIM)
    tabs = (ca, sa, cb, sb)
    memb = mem.reshape(-1, D_MODEL).astype(BF16)
    mem_len = mem.shape[1]
    row = lambda v: v.reshape(1, -1).astype(F32)

    xf = x.reshape(t, D_MODEL)
    xb = xf.astype(BF16)
    for l in range(DEPTH):
        lam_init = 0.8 - 0.6 * math.exp(-0.3 * l)
        lam = (jnp.exp(jnp.sum(lam_q1[l] * lam_k1[l])) - jnp.exp(jnp.sum(lam_q2[l] * lam_k2[l]))
               + lam_init).reshape(1, 1).astype(F32)
        h = _inproj(xb, _layout_w_in(w_in[l]), tabs)
        h3 = h.reshape(bsz, seq, H_COLS)
        oa = _diff_attention(h3, lam, jnp.tile(row(diff_norm_g[l]), (1, 2)), lam_init)
        ob = _dilated_attention(h3)
        wa = _pad_heads(gla_wa2[l], C_HEADS, C_K, C_KP, 1)
        wa = jnp.pad(wa, ((0, LANES - C_RANK), (0, 0))).reshape(LANES, 2, LANES).transpose(1, 0, 2).astype(BF16)
        ba = _pad_heads(gla_ba[l], C_HEADS, C_K, C_KP, 0).reshape(2, 1, LANES).astype(F32)
        gn = jnp.pad(row(gla_norm_g[l]), ((0, 0), (0, C_VP - C_V)))
        oc = _gla(h3, wa, ba, gn)
        wo = w_out[l]
        wo_a = wo[:A_HEADS * A_V].astype(BF16)
        wo_b = wo[A_HEADS * A_V:A_HEADS * A_V + B_HEADS * B_DIM].astype(BF16)
        wo_c = _pad_heads(wo[A_HEADS * A_V + B_HEADS * B_DIM:], C_HEADS, C_V, C_VP, 0).astype(BF16)
        kv = _matmul(memb, xa_wkv[l].astype(BF16), BF16).reshape(bsz, mem_len, 2 * D_MODEL)
        moe = l % 2 == 1
        rw = jnp.pad(moe_router[l // 2], ((0, 0), (0, LANES - N_EXPERTS))) if moe else None
        outs = _mix_xattn_ln(oa, ob, oc, (wo_a, wo_b, wo_c), xf.reshape(bsz, seq, D_MODEL),
                             (row(ln_mix_g[l]), row(ln_mix_b[l])), xa_wq[l].astype(BF16), kv, xa_wo[l].astype(BF16),
                             (row(ln_xa_g[l]), row(ln_xa_b[l])), router_w=rw)
        xf, xb = outs[0].reshape(t, D_MODEL), outs[1].reshape(t, D_MODEL)
        if moe:
            xf, xb = _moe_ln(xf, outs[2], outs[3].reshape(t, LANES), moe_w13[l // 2].astype(BF16),
                             moe_w2[l // 2].astype(BF16), row(ln_ffn_g[l]), row(ln_ffn_b[l]))
        else:
            xf, xb = _ffn_ln(xb, xf, ffd_w13[l // 2].astype(BF16), ffd_w2[l // 2].astype(BF16),
                             row(ln_ffn_g[l]), row(ln_ffn_b[l]))
    return xf.reshape(bsz, seq, D_MODEL)
```

```python
import functools
import math

import numpy as np
import jax
import jax.numpy as jnp
from jax import lax
from jax.experimental import pallas as pl
from jax.experimental.pallas import tpu as pltpu

F32 = jnp.float32
BF16 = jnp.bfloat16

LANES = 128
VMEM_LIMIT_BYTES = 56 * 1024 * 1024

D_MODEL = 1024
DEPTH = 2
ROPE_THETA = 500000.0
ROPE_FRAC = 4
LN_EPS = 1e-5
NEG_INF = -1e30
ALPHA = (2 * DEPTH) ** 0.25
LOG2_E = math.log2(math.e)

A_HEADS, A_QK, A_V = 4, 32, 64
B_HEADS, B_DIM = 6, 64
B_SPAN = 128
B_DILATIONS = (1, 4, 16)
C_HEADS, C_K, C_V = 4, 48, 96
C_KP, C_VP = 64, 128
C_RANK, C_TAU, C_CHUNK = 16, 16.0, 16
XA_HEADS, XA_DIM = 4, 256
N_EXPERTS, TOP_K = 8, 2

QA, KA, QB, KB, VA, VB, RC, QC, KC, VC, GC, H_COLS = (
    0, 256, 512, 896, 1280, 1536, 1920, 2048, 2304, 2560, 3072, 3584)


def _cparams(sem):
    return pltpu.CompilerParams(dimension_semantics=sem, vmem_limit_bytes=VMEM_LIMIT_BYTES)


def _layer_norm(y, g, b):
    mu = jnp.mean(y, axis=-1, keepdims=True)
    yc = y - mu
    var = jnp.mean(yc * yc, axis=-1, keepdims=True)
    return yc * lax.rsqrt(var + LN_EPS) * g + b


def _dot_nt(a, b):
    return lax.dot_general(a, b, (((1,), (1,)), ((), ())), preferred_element_type=F32)


def _inproj_kernel(x_ref, w_ref, ca_ref, sa_ref, cb_ref, sb_ref, o_ref):
    x = x_ref[...]
    lane = lax.broadcasted_iota(jnp.int32, (x.shape[0], LANES), 1)

    def rope_cols(lo, k_lo, hi, c_ref, s_ref, half, period):
        acc = jnp.dot(x, w_ref[:, lo:hi], preferred_element_type=F32)
        c, s = c_ref[...], s_ref[...]
        first = lane % period < half
        for k in range((hi - lo) // LANES):
            a = acc[:, k * LANES:(k + 1) * LANES]
            up = pltpu.roll(a, LANES - half, 1)
            dn = pltpu.roll(a, half, 1)
            r = a * c + jnp.where(first, up, dn) * s
            if lo + k * LANES < k_lo:
                r = r * (period ** -0.5 * LOG2_E)
            o_ref[:, lo + k * LANES:lo + (k + 1) * LANES] = r.astype(o_ref.dtype)

    rope_cols(QA, KA, QB, ca_ref, sa_ref, A_QK // ROPE_FRAC // 2, A_QK)
    rope_cols(QB, KB, VA, cb_ref, sb_ref, B_DIM // ROPE_FRAC // 2, B_DIM)
    o_ref[:, VA:] = jnp.dot(x, w_ref[:, VA:], preferred_element_type=F32).astype(o_ref.dtype)


def _inproj(xb, w, tabs, tm=512):
    t = xb.shape[0]
    tab_spec = pl.BlockSpec((tm, LANES), lambda i: (i, 0))
    return pl.pallas_call(
        _inproj_kernel,
        out_shape=jax.ShapeDtypeStruct((t, H_COLS), BF16),
        grid=(t // tm,),
        in_specs=[pl.BlockSpec((tm, D_MODEL), lambda i: (i, 0)),
                  pl.BlockSpec((D_MODEL, H_COLS), lambda i: (0, 0)),
                  tab_spec, tab_spec, tab_spec, tab_spec],
        out_specs=pl.BlockSpec((tm, H_COLS), lambda i: (i, 0)),
        compiler_params=_cparams(("parallel",)),
        name="inproj_rope",
    )(xb, w, *tabs)


def _diff_kernel(lam_ref, q_ref, k_ref, v_ref, g_ref, o_ref, *, seq, tq, out_scale):
    lane = lax.broadcasted_iota(jnp.int32, (tq, LANES), 1)
    row = lax.broadcasted_iota(jnp.int32, (tq, tq), 0)
    col = lax.broadcasted_iota(jnp.int32, (tq, tq), 1)
    causal = col <= row
    lo = lane < A_V
    lam = lam_ref[0, 0]

    for i in range(seq // tq):
        past = i * tq
        q = q_ref[0, past:past + tq, :]
        k_d = k_ref[0, past:past + tq, :]
        v_d = v_ref[0, past:past + tq, :]

        def attend(c):
            qm = jnp.where(lane // A_QK == c, q, jnp.zeros_like(q))
            s_d = jnp.where(causal, _dot_nt(qm, k_d), NEG_INF)
            m = jnp.max(s_d, axis=-1, keepdims=True)
            if past:
                s_p = _dot_nt(qm, k_ref[0, :past, :])
                m = jnp.maximum(m, jnp.max(s_p, axis=-1, keepdims=True))
            p_d = jnp.exp2(s_d - m)
            l = jnp.sum(p_d, axis=-1, keepdims=True)
            acc = jnp.dot(p_d.astype(BF16), v_d, preferred_element_type=F32)
            if past:
                p_p = jnp.exp2(s_p - m)
                l = l + jnp.sum(p_p, axis=-1, keepdims=True)
                acc = acc + jnp.dot(p_p.astype(BF16), v_ref[0, :past, :], preferred_element_type=F32)
            return acc / l

        o_h0 = attend(0) - lam * attend(1)
        o_h1 = attend(2) - lam * attend(3)
        o = jnp.where(lo, o_h0, o_h1)
        sq = o * o
        s_lo = jnp.sum(jnp.where(lo, sq, 0.0), axis=-1, keepdims=True)
        s_all = jnp.sum(sq, axis=-1, keepdims=True)
        ms = jnp.where(lo, s_lo, s_all - s_lo) * (1.0 / A_V)
        o_ref[0, past:past + tq, :] = (o * lax.rsqrt(ms + LN_EPS) * g_ref[...] * out_scale).astype(o_ref.dtype)


def _diff_attention(h3, lam, g, lam_init, tq=512):
    bsz, seq, _ = h3.shape
    kern = functools.partial(_diff_kernel, seq=seq, tq=tq, out_scale=1.0 - lam_init)
    blk = lambda off: pl.BlockSpec((1, seq, LANES), lambda b, p: (b, 0, off // LANES + p))
    return pl.pallas_call(
        kern,
        out_shape=jax.ShapeDtypeStruct((bsz, seq, A_HEADS * A_V), BF16),
        grid=(bsz, A_HEADS // 2),
        in_specs=[pl.BlockSpec(memory_space=pltpu.SMEM), blk(QA), blk(KA), blk(VA),
                  pl.BlockSpec((1, LANES), lambda b, p: (0, 0))],
        out_specs=pl.BlockSpec((1, seq, LANES), lambda b, p: (b, 0, p)),
        compiler_params=_cparams(("parallel", "parallel")),
        name="diff_attention",
    )(lam, h3, h3, h3, g)


def _dil_kernel(q_ref, k_ref, v_ref, o_ref, qf, kf, vf, qs, ks, vs, m_s, l_s, a_s, *, seq):
    qf[...] = q_ref[0].astype(F32)
    kf[...] = k_ref[0].astype(F32)
    vf[...] = v_ref[0].astype(F32)
    span = B_SPAN
    lo = lax.broadcasted_iota(jnp.int32, (span, LANES), 1) < B_DIM
    own = ((lax.broadcasted_iota(jnp.int32, (2 * span, LANES), 1) < B_DIM)
           == (lax.broadcasted_iota(jnp.int32, (2 * span, LANES), 0) < span))
    qi = lax.broadcasted_iota(jnp.int32, (2 * span, 2 * span), 0) % span
    ki = lax.broadcasted_iota(jnp.int32, (2 * span, 2 * span), 1)
    band = (ki >= qi) & (ki <= qi + span)
    first = (lax.broadcasted_iota(jnp.int32, (2 * span, span), 1)
             <= lax.broadcasted_iota(jnp.int32, (2 * span, span), 0) % span)

    for g, r in enumerate(B_DILATIONS):
        sub = seq // r
        for rho in range(r):
            rows = pl.ds(rho, sub, stride=r) if r > 1 else pl.ds(0, sub)
            qs[rho * sub:(rho + 1) * sub, :] = qf[rows, :].astype(BF16)
            ks[rho * sub:(rho + 1) * sub, :] = kf[rows, :].astype(BF16)
            vs[rho * sub:(rho + 1) * sub, :] = vf[rows, :].astype(BF16)
        for rho in range(r):
            for n in range(sub // span):
                base = rho * sub + n * span
                qb = qs[base:base + span, :]
                lhs = jnp.where(own, jnp.concatenate([qb, qb], axis=0), jnp.zeros((), BF16))
                start = base - span if n else base
                s = jnp.where(band if n else first, _dot_nt(lhs, ks[start:base + span, :]), NEG_INF)
                m = jnp.max(s, axis=-1, keepdims=True)
                p = jnp.exp2(s - m)
                l = jnp.sum(p, axis=-1, keepdims=True)
                a = jnp.dot(p.astype(BF16), vs[start:base + span, :], preferred_element_type=F32)
                m = jnp.broadcast_to(m, (2 * span, LANES))
                l = jnp.broadcast_to(l, (2 * span, LANES))
                m_g = jnp.where(lo, m[:span], m[span:])
                l_g = jnp.where(lo, l[:span], l[span:])
                a_g = jnp.where(lo, a[:span], a[span:])
                nat = pl.ds(rho + r * span * n, span, stride=r) if r > 1 else pl.ds(n * span, span)
                if g == 0:
                    m_s[nat, :] = m_g
                    l_s[nat, :] = l_g
                    a_s[nat, :] = a_g
                else:
                    m_o = m_s[nat, :]
                    m_n = jnp.maximum(m_o, m_g)
                    w_o = jnp.exp2(m_o - m_n)
                    w_g = jnp.exp2(m_g - m_n)
                    m_s[nat, :] = m_n
                    l_s[nat, :] = w_o * l_s[nat, :] + w_g * l_g
                    a_s[nat, :] = w_o * a_s[nat, :] + w_g * a_g

    o_ref[0] = (a_s[...] / l_s[...]).astype(o_ref.dtype)


def _dilated_attention(h3):
    bsz, seq, _ = h3.shape
    blk = lambda off: pl.BlockSpec((1, seq, LANES), lambda b, p: (b, 0, off // LANES + p))
    return pl.pallas_call(
        functools.partial(_dil_kernel, seq=seq),
        out_shape=jax.ShapeDtypeStruct((bsz, seq, B_HEADS * B_DIM), BF16),
        grid=(bsz, B_HEADS // 2),
        in_specs=[blk(QB), blk(KB), blk(VB)],
        out_specs=pl.BlockSpec((1, seq, LANES), lambda b, p: (b, 0, p)),
        scratch_shapes=([pltpu.VMEM((seq, LANES), F32)] * 3 + [pltpu.VMEM((seq, LANES), BF16)] * 3
                        + [pltpu.VMEM((seq, LANES), F32)] * 3),
        compiler_params=_cparams(("parallel", "parallel")),
        name="dilated_attention",
    )(h3, h3, h3)


def _gla_kernel(q_ref, k_ref, v_ref, gate_ref, rc_ref, wa_ref, ba_ref, gn_ref, rcat_ref, o_ref, st, *, tb):
    t = pl.program_id(2)

    @pl.when(t == 0)
    def _():
        st[...] = jnp.zeros_like(st)

    ch, half, nc = C_CHUNK, C_CHUNK // 2, tb // C_CHUNK
    z = jnp.dot(rc_ref[0], wa_ref[0], preferred_element_type=F32) + ba_ref[0]
    la = (jnp.minimum(z, 0.0) - jnp.log(1.0 + jnp.exp(-jnp.abs(z)))) * (LOG2_E / C_TAU)
    ri = lax.broadcasted_iota(jnp.int32, (LANES, LANES), 0)
    ci = lax.broadcasted_iota(jnp.int32, (LANES, LANES), 1)
    same = (ri // ch) == (ci // ch)
    tri = jnp.where(same & (ci <= ri), 1.0, 0.0).astype(F32)
    sums = lambda m: jnp.concatenate(
        [jnp.dot(m, la[r:r + LANES], preferred_element_type=F32, precision=lax.Precision.HIGHEST)
         for r in range(0, tb, LANES)], axis=0)
    b = sums(tri)
    bl = jnp.broadcast_to(b.reshape(nc, ch, LANES)[:, ch - 1:ch, :], (nc, ch, LANES)).reshape(tb, LANES)
    qs = q_ref[0].astype(F32) * (C_K ** -0.5)
    kf = k_ref[0].astype(F32)
    vb = v_ref[0]
    qt = (qs * jnp.exp2(b)).astype(BF16)
    kt = kf * jnp.exp2(bl - b)
    gl = jnp.exp2(bl)
    head0 = lax.broadcasted_iota(jnp.int32, (tb, LANES), 1) < C_KP
    kt_h = (jnp.where(head0, kt, 0.0).astype(BF16), jnp.where(head0, 0.0, kt).astype(BF16))

    def chunk_row(x, j, rows):
        xj = x.reshape(nc, ch, LANES)[:, j:j + 1, :]
        return jnp.broadcast_to(xj, (nc, rows, LANES)).reshape(nc * rows, LANES)

    def upper(x):
        return x.reshape(nc, 2, half, LANES)[:, 1].reshape(nc * half, LANES)

    rowi = lax.broadcasted_iota(jnp.int32, (tb, LANES), 0) % ch
    rowu = lax.broadcasted_iota(jnp.int32, (nc * half, LANES), 0) % half + half
    b_u, qs_u = upper(b), upper(qs)
    terms = []
    for j in range(ch):
        if j < half:
            d = b - chunk_row(b, j, ch)
            e = jnp.exp2(jnp.where(rowi >= j, d, NEG_INF) if j else d)
            term = qs * e * chunk_row(kf, j, ch)
        else:
            e = jnp.exp2(jnp.where(rowu >= j, b_u - chunk_row(b, j, half), NEG_INF))
            tu = (qs_u * e * chunk_row(kf, j, half)).reshape(nc, 1, half, LANES)
            term = jnp.concatenate([jnp.zeros_like(tu), tu], axis=1).reshape(tb, LANES)
        terms.append(term.astype(BF16))
    s_tiled = jnp.dot(jnp.concatenate(terms, axis=1), rcat_ref[...], preferred_element_type=F32)
    rr = lax.broadcasted_iota(jnp.int32, (tb, 2 * LANES), 0)
    cc = lax.broadcasted_iota(jnp.int32, (tb, 2 * LANES), 1)
    m = jnp.where((rr % LANES) // ch == (cc % LANES) // ch, s_tiled, 0.0).astype(BF16)
    o_intra = jnp.concatenate(
        [jnp.concatenate([jnp.dot(m[r:r + LANES, h * LANES:(h + 1) * LANES], vb[r:r + LANES, h * C_VP:(h + 1) * C_VP],
                                  preferred_element_type=F32) for h in range(2)], axis=1)
         for r in range(0, tb, LANES)], axis=0)

    chunks = [slice(c * ch, (c + 1) * ch) for c in range(nc)]
    uts = [jnp.concatenate(
        [lax.dot_general(vb[rows, h * C_VP:(h + 1) * C_VP], kt_h[h][rows], (((0,), (0,)), ((), ())),
                         preferred_element_type=F32) for h in range(2)], axis=0) for rows in chunks]
    s_prev = st[...]
    states = []
    for c in range(nc):
        states.append(s_prev.astype(BF16))
        s_prev = gl[c * ch:c * ch + 1, :] * s_prev + uts[c]
    st[...] = s_prev
    o_inter = [_dot_nt(qt[rows], states[c]) for c, rows in enumerate(chunks)]

    o = o_intra + jnp.concatenate(o_inter, axis=0)
    gate = gate_ref[0].astype(F32)
    outs = []
    for hh in range(2):
        oh = o[:, hh * C_VP:(hh + 1) * C_VP]
        ms = jnp.sum(oh * oh, axis=-1, keepdims=True) * (1.0 / C_V)
        gh = gate[:, hh * C_VP:(hh + 1) * C_VP]
        outs.append(oh * lax.rsqrt(ms + LN_EPS) * gn_ref[...] * (gh * jax.nn.sigmoid(gh)))
    o_ref[0] = jnp.concatenate(outs, axis=1).astype(o_ref.dtype)


def _gla_reduce_matrix():
    r = np.arange(C_CHUNK * LANES)[:, None]
    c = np.arange(2 * LANES)[None, :]
    hit = ((r % LANES) // C_KP == c // LANES) & (r // LANES == (c % LANES) % C_CHUNK)
    return jnp.asarray(hit, dtype=BF16)


def _gla(h3, wa, ba, gn, tb=1024):
    bsz, seq, _ = h3.shape
    return pl.pallas_call(
        functools.partial(_gla_kernel, tb=tb),
        out_shape=jax.ShapeDtypeStruct((bsz, seq, C_HEADS * C_VP), BF16),
        grid=(bsz, C_HEADS // 2, seq // tb),
        in_specs=[pl.BlockSpec((1, tb, LANES), lambda b, p, t: (b, t, QC // LANES + p)),
                  pl.BlockSpec((1, tb, LANES), lambda b, p, t: (b, t, KC // LANES + p)),
                  pl.BlockSpec((1, tb, 2 * C_VP), lambda b, p, t: (b, t, VC // (2 * C_VP) + p)),
                  pl.BlockSpec((1, tb, 2 * C_VP), lambda b, p, t: (b, t, GC // (2 * C_VP) + p)),
                  pl.BlockSpec((1, tb, LANES), lambda b, p, t: (b, t, RC // LANES)),
                  pl.BlockSpec((1, LANES, LANES), lambda b, p, t: (p, 0, 0)),
                  pl.BlockSpec((1, 1, LANES), lambda b, p, t: (p, 0, 0)),
                  pl.BlockSpec((1, C_VP), lambda b, p, t: (0, 0)),
                  pl.BlockSpec((C_CHUNK * LANES, 2 * LANES), lambda b, p, t: (0, 0))],
        out_specs=pl.BlockSpec((1, tb, 2 * C_VP), lambda b, p, t: (b, t, p)),
        scratch_shapes=[pltpu.VMEM((2 * C_VP, LANES), F32)],
        compiler_params=_cparams(("parallel", "parallel", "arbitrary")),
        name="gla",
    )(h3, h3, h3, h3, h3, wa, ba, gn, _gla_reduce_matrix())


def _matmul_kernel(a_ref, w_ref, o_ref):
    o_ref[...] = jnp.dot(a_ref[...], w_ref[...], preferred_element_type=F32).astype(o_ref.dtype)


def _matmul(a, w, out_dtype, tm=512):
    m, k = a.shape
    n = w.shape[1]
    tm = min(tm, m)
    return pl.pallas_call(
        _matmul_kernel,
        out_shape=jax.ShapeDtypeStruct((m, n), out_dtype),
        grid=(m // tm,),
        in_specs=[pl.BlockSpec((tm, k), lambda i: (i, 0)), pl.BlockSpec((k, n), lambda i: (0, 0))],
        out_specs=pl.BlockSpec((tm, n), lambda i: (i, 0)),
        compiler_params=_cparams(("parallel",)),
        name="matmul",
    )(a, w)


def _route_top2(x, xh, rwh, rwl):
    xl = (x - xh.astype(F32)).astype(BF16)
    logits = (jnp.dot(xh, rwh, preferred_element_type=F32) + jnp.dot(xh, rwl, preferred_element_type=F32)
              + jnp.dot(xl, rwh, preferred_element_type=F32))
    lane = lax.broadcasted_iota(jnp.int32, logits.shape, 1)
    logits = jnp.where(lane < N_EXPERTS, logits, -jnp.inf)
    v1 = jnp.max(logits, axis=-1, keepdims=True)
    i1 = jnp.min(jnp.where(logits == v1, lane, LANES), axis=-1, keepdims=True)
    rest = jnp.where(lane == i1, -jnp.inf, logits)
    v2 = jnp.max(rest, axis=-1, keepdims=True)
    i2 = jnp.min(jnp.where(rest == v2, lane, LANES), axis=-1, keepdims=True)
    e2 = jnp.exp(v2 - v1)
    w1 = 1.0 / (1.0 + e2)
    w2 = e2 / (1.0 + e2)
    return (jnp.where(lane == 0, w1, 0.0) + jnp.where(lane == 1, w2, 0.0)
            + jnp.where(lane == 2, i1.astype(F32), 0.0) + jnp.where(lane == 3, i2.astype(F32), 0.0))


def _mix_xattn_kernel(oa_ref, ob_ref, oc_ref, wa_ref, wb_ref, wc_ref, x_ref, g1_ref, b1_ref,
                      wq_ref, k_ref, v_ref, wo_ref, g2_ref, b2_ref, *rest, route):
    if route:
        rwh_ref, rwl_ref, y_ref, yb_ref, ys_ref, route_ref = rest
    else:
        y_ref, yb_ref = rest
    acc = jnp.dot(oa_ref[0], wa_ref[...], preferred_element_type=F32)
    acc += jnp.dot(ob_ref[0], wb_ref[...], preferred_element_type=F32)
    acc += jnp.dot(oc_ref[0], wc_ref[...], preferred_element_type=F32)
    x1 = _layer_norm(ALPHA * x_ref[0] + acc, g1_ref[...], b1_ref[...])
    q = jnp.dot(x1.astype(BF16), wq_ref[...], preferred_element_type=F32).astype(BF16)
    outs = []
    for h in range(XA_HEADS):
        cols = slice(h * XA_DIM, (h + 1) * XA_DIM)
        s = _dot_nt(q[:, cols], k_ref[0, :, cols]) * (XA_DIM ** -0.5)
        m = jnp.max(s, axis=-1, keepdims=True)
        p = jnp.exp(s - m)
        l = jnp.sum(p, axis=-1, keepdims=True)
        o = jnp.dot(p.astype(BF16), v_ref[0, :, cols], preferred_element_type=F32)
        outs.append((o / l).astype(BF16))
    xa = jnp.dot(jnp.concatenate(outs, axis=1), wo_ref[...], preferred_element_type=F32)
    y = _layer_norm(ALPHA * x1 + xa, g2_ref[...], b2_ref[...])
    yb = y.astype(BF16)
    y_ref[0] = y
    yb_ref[0] = yb
    if route:
        route_ref[0] = _route_top2(y, yb, rwh_ref[...], rwl_ref[...])
        for s in range(ROW_SLABS):
            _slab(ys_ref, s, y.shape[0])[...] = y[:, s * LANES:(s + 1) * LANES]


def _mix_xattn_ln(oa, ob, oc, wo_parts, x3, ln1, wq, kv3, wo, ln2, router_w=None, tm=512):
    bsz, seq, _ = x3.shape
    mem_len = kv3.shape[1]
    route = router_w is not None
    row = lambda n: pl.BlockSpec((1, tm, n), lambda bi, i: (bi, i, 0))
    full = lambda a: pl.BlockSpec(a.shape, lambda bi, i: (0,) * a.ndim)
    steps = seq // tm
    ins = [oa, ob, oc, *wo_parts, x3, *ln1, wq, kv3, kv3, wo, *ln2]
    in_specs = [row(oa.shape[2]), row(ob.shape[2]), row(oc.shape[2]), *[full(w) for w in wo_parts],
                row(D_MODEL), *[full(v) for v in ln1], full(wq),
                pl.BlockSpec((1, mem_len, D_MODEL), lambda bi, i: (bi, 0, 0)),
                pl.BlockSpec((1, mem_len, D_MODEL), lambda bi, i: (bi, 0, 1)),
                full(wo), *[full(v) for v in ln2]]
    out_shape = [jax.ShapeDtypeStruct((bsz, seq, D_MODEL), F32), jax.ShapeDtypeStruct((bsz, seq, D_MODEL), BF16)]
    out_specs = [row(D_MODEL), row(D_MODEL)]
    if route:
        rwh = router_w.astype(BF16)
        rwl = (router_w - rwh.astype(F32)).astype(BF16)
        ins += [rwh, rwl]
        in_specs += [full(rwh), full(rwl)]
        out_shape += [jax.ShapeDtypeStruct((bsz * seq * ROW_SLABS, LANES), F32),
                      jax.ShapeDtypeStruct((bsz, seq, LANES), F32)]
        out_specs += [pl.BlockSpec((tm * ROW_SLABS, LANES), lambda bi, i: (bi * steps + i, 0)), row(LANES)]
    return pl.pallas_call(
        functools.partial(_mix_xattn_kernel, route=route),
        out_shape=tuple(out_shape),
        grid=(bsz, steps),
        in_specs=in_specs,
        out_specs=tuple(out_specs),
        compiler_params=_cparams(("parallel", "arbitrary")),
        name="mix_xattn_ln",
    )(*ins)


def _swiglu_chunk(xb, w1, w3, w2):
    hg = jnp.dot(xb, w1, preferred_element_type=F32)
    hu = jnp.dot(xb, w3, preferred_element_type=F32)
    a = hg * jax.nn.sigmoid(hg) * hu
    return jnp.dot(a.astype(BF16), w2, preferred_element_type=F32)


def _ffn_kernel(xb_ref, x_ref, w1_ref, w3_ref, w2_ref, g_ref, b_ref, y_ref, yb_ref):
    f = _swiglu_chunk(xb_ref[...], w1_ref[...], w3_ref[...], w2_ref[...])
    y = _layer_norm(ALPHA * x_ref[...] + f, g_ref[...], b_ref[...])
    y_ref[...] = y
    yb_ref[...] = y.astype(BF16)


def _ffn_ln(xb, x, w13, w2, g, b, tm=512):
    t = x.shape[0]
    f = w2.shape[0]
    row = pl.BlockSpec((tm, D_MODEL), lambda i: (i, 0))
    vec = lambda a: pl.BlockSpec(a.shape, lambda i: (0, 0))
    weight = lambda shape, col: pl.BlockSpec(shape, lambda i: (0, col), pipeline_mode=pl.Buffered(1))
    return pl.pallas_call(
        _ffn_kernel,
        out_shape=(jax.ShapeDtypeStruct((t, D_MODEL), F32), jax.ShapeDtypeStruct((t, D_MODEL), BF16)),
        grid=(t // tm,),
        in_specs=[row, row, weight((D_MODEL, f), 0), weight((D_MODEL, f), 1), weight((f, D_MODEL), 0),
                  vec(g), vec(b)],
        out_specs=(row, row),
        compiler_params=_cparams(("parallel",)),
        name="ffn_ln",
    )(xb, x, w13, w13, w2, g, b)


def _dispatch_plan(route, tm):
    t = route.shape[0]
    expert = route[:, 2:4].astype(jnp.int32).reshape(-1)
    onehot = (expert[:, None] == jnp.arange(N_EXPERTS, dtype=jnp.int32)[None, :]).astype(jnp.int32)
    csum = jnp.cumsum(onehot, axis=0)
    rank = jnp.sum((csum - 1) * onehot, axis=1)
    counts = csum[-1]
    padded = (counts + tm - 1) // tm * tm
    ends = jnp.cumsum(padded)
    dest = (ends - padded)[expert] + rank
    n_tiles = TOP_K * t // tm + N_EXPERTS
    src = jnp.zeros((n_tiles * tm,), jnp.int32).at[dest].set(
        jnp.arange(TOP_K * t, dtype=jnp.int32) // TOP_K, unique_indices=True, mode="promise_in_bounds")
    tile_start = jnp.arange(n_tiles, dtype=jnp.int32) * tm
    tile_expert = jnp.sum((tile_start[:, None] >= ends[None, :]).astype(jnp.int32), axis=1)
    tile_expert = jnp.minimum(tile_expert, N_EXPERTS - 1)
    tile_used = (tile_start < ends[-1]).astype(jnp.int32)
    return src, dest.reshape(t, TOP_K).astype(jnp.int32), tile_expert, tile_used


ROW_SLABS = D_MODEL // LANES


def _slab(ref, s, rows):
    return ref.at[pl.ds(s, rows, stride=ROW_SLABS), :]


def _row_tile(ref, r):
    return ref.at[pl.ds(r * ROW_SLABS, ROW_SLABS), :]


GATHER_PRIORITY = 1


def _moe_ffn_kernel(te_ref, tu_ref, src_ref, nxt_ref, x_hbm, w1_ref, w3_ref, w2_ref, y_ref,
                    xbuf, xs_ref, acc_ref, sem, *, tm):
    i = pl.program_id(0)
    j = pl.program_id(1)
    n_tiles = pl.num_programs(0)
    last = pl.num_programs(1) - 1
    slot = i % 2

    def start_rows(idx_ref, dst_slot):
        def issue(r, carry):
            pltpu.make_async_copy(_row_tile(x_hbm, idx_ref[0, 0, r]), _row_tile(xbuf.at[dst_slot], r),
                                  sem.at[dst_slot]).start(priority=GATHER_PRIORITY)
            return carry
        lax.fori_loop(0, tm, issue, 0, unroll=8)

    @pl.when(j == 0)
    def _():
        @pl.when((i == 0) & (tu_ref[0] > 0))
        def _():
            start_rows(src_ref, 0)

        @pl.when(tu_ref[i] > 0)
        def _():
            pltpu.make_async_copy(x_hbm.at[pl.ds(0, tm * ROW_SLABS), :], xbuf.at[slot], sem.at[slot]).wait()
            for s in range(ROW_SLABS):
                xs_ref[:, s * LANES:(s + 1) * LANES] = _slab(xbuf.at[slot], s, tm)[...].astype(xs_ref.dtype)

        @pl.when((i + 1 < n_tiles) & (tu_ref[jnp.minimum(i + 1, n_tiles - 1)] > 0))
        def _():
            start_rows(nxt_ref, 1 - slot)

        acc_ref[...] = jnp.zeros_like(acc_ref)

    @pl.when(tu_ref[i] > 0)
    def _():
        acc_ref[...] += _swiglu_chunk(xs_ref[...], w1_ref[0], w3_ref[0], w2_ref[0])

    @pl.when(j == last)
    def _():
        for s in range(ROW_SLABS):
            _slab(y_ref, s, tm)[...] = acc_ref[:, s * LANES:(s + 1) * LANES]


def _moe_ffn(tile_expert, tile_used, src, x3, w13, w2, tm, tf):
    n_tiles = src.shape[0]
    nf = w2.shape[1] // tf
    chunk = lambda i, j, tu: jnp.where(tu[i] > 0, j, nf - 1)
    idx = lambda shift: pl.BlockSpec((1, 1, tm), lambda i, j, te, tu: (jnp.minimum(i + shift, n_tiles - 1), 0, 0),
                                     memory_space=pltpu.SMEM)
    return pl.pallas_call(
        functools.partial(_moe_ffn_kernel, tm=tm),
        out_shape=jax.ShapeDtypeStruct((n_tiles * tm * ROW_SLABS, LANES), F32),
        grid_spec=pltpu.PrefetchScalarGridSpec(
            num_scalar_prefetch=2, grid=(n_tiles, nf),
            in_specs=[idx(0), idx(1), pl.BlockSpec(memory_space=pl.ANY),
                      pl.BlockSpec((1, D_MODEL, tf), lambda i, j, te, tu: (te[i], 0, chunk(i, j, tu))),
                      pl.BlockSpec((1, D_MODEL, tf), lambda i, j, te, tu: (te[i], 0, chunk(i, j, tu) + nf)),
                      pl.BlockSpec((1, tf, D_MODEL), lambda i, j, te, tu: (te[i], chunk(i, j, tu), 0))],
            out_specs=pl.BlockSpec((tm * ROW_SLABS, LANES), lambda i, j, te, tu: (i, 0)),
            scratch_shapes=[pltpu.VMEM((2, tm * ROW_SLABS, LANES), F32), pltpu.VMEM((tm, D_MODEL), BF16),
                            pltpu.VMEM((tm, D_MODEL), F32), pltpu.SemaphoreType.DMA((2,))]),
        compiler_params=_cparams(("arbitrary", "arbitrary")),
        name="moe_ffn",
    )(tile_expert, tile_used, src, src, x3, w13, w13, w2)


def _combine_kernel(d0_ref, d1_ref, n0_ref, n1_ref, y_hbm, route_ref, x_ref, g_ref, b_ref, o_ref, ob_ref,
                    buf, sem, *, tm):
    i = pl.program_id(0)
    slot = i % 2

    def start_rows(a_ref, b_ref, dst_slot):
        def issue(r, carry):
            pltpu.make_async_copy(_row_tile(y_hbm, a_ref[0, 0, r]), _row_tile(buf.at[dst_slot, 0], r),
                                  sem.at[dst_slot, 0]).start(priority=0)
            pltpu.make_async_copy(_row_tile(y_hbm, b_ref[0, 0, r]), _row_tile(buf.at[dst_slot, 1], r),
                                  sem.at[dst_slot, 1]).start(priority=1)
            return carry
        lax.fori_loop(0, tm, issue, 0, unroll=8)

    @pl.when(i == 0)
    def _():
        start_rows(d0_ref, d1_ref, 0)

    for k in range(TOP_K):
        pltpu.make_async_copy(y_hbm.at[pl.ds(0, tm * ROW_SLABS), :], buf.at[slot, k], sem.at[slot, k]).wait()

    @pl.when(i + 1 < pl.num_programs(0))
    def _():
        start_rows(n0_ref, n1_ref, 1 - slot)

    route = route_ref[...]
    w0, w1 = route[:, 0:1], route[:, 1:2]
    f = jnp.concatenate([w0 * _slab(buf.at[slot, 0], s, tm)[...] + w1 * _slab(buf.at[slot, 1], s, tm)[...]
                         for s in range(ROW_SLABS)], axis=1)
    y = _layer_norm(ALPHA * x_ref[...] + f, g_ref[...], b_ref[...])
    o_ref[...] = y
    ob_ref[...] = y.astype(BF16)


def _combine_ln(dest, ys, route, x, g, b, tm=512):
    t = x.shape[0]
    steps = t // tm
    row = pl.BlockSpec((tm, D_MODEL), lambda i: (i, 0))
    vec = lambda a: pl.BlockSpec(a.shape, lambda i: (0, 0))
    idx = lambda shift: pl.BlockSpec((1, 1, tm), lambda i: (jnp.minimum(i + shift, steps - 1), 0, 0),
                                     memory_space=pltpu.SMEM)
    d0, d1 = dest[:, 0].reshape(-1, 1, tm), dest[:, 1].reshape(-1, 1, tm)
    return pl.pallas_call(
        functools.partial(_combine_kernel, tm=tm),
        out_shape=(jax.ShapeDtypeStruct((t, D_MODEL), F32), jax.ShapeDtypeStruct((t, D_MODEL), BF16)),
        grid=(steps,),
        in_specs=[idx(0), idx(0), idx(1), idx(1), pl.BlockSpec(memory_space=pl.ANY),
                  pl.BlockSpec((tm, LANES), lambda i: (i, 0)),
                  row, vec(g), vec(b)],
        out_specs=(row, row),
        scratch_shapes=[pltpu.VMEM((2, TOP_K, tm * ROW_SLABS, LANES), F32), pltpu.SemaphoreType.DMA((2, TOP_K))],
        compiler_params=_cparams(("arbitrary",)),
        name="moe_combine_ln",
    )(d0, d1, d0, d1, ys, route, x, g, b)


def _moe_ln(x, x_slab, route, w13, w2, g, b, tm=512, tf=1792):
    src, dest, tile_expert, tile_used = _dispatch_plan(route, tm)
    ys = _moe_ffn(tile_expert, tile_used, src.reshape(-1, 1, tm), x_slab, w13, w2, tm, tf)
    return _combine_ln(dest, ys, route, x, g, b)


def _pad_heads(w, heads, width, padded, axis):
    shape = list(w.shape)
    shape[axis:axis + 1] = [heads, width]
    w = w.reshape(shape)
    pad = [(0, 0)] * w.ndim
    pad[axis + 1] = (0, padded - width)
    w = jnp.pad(w, pad)
    shape[axis:axis + 2] = [heads * padded]
    return w.reshape(shape)


def _layout_w_in(w):
    bounds = np.cumsum([0, 256, 256, 256, 384, 384, 384, 192, 192, 384, 384, 16])
    qa, ka, va, qb, kb, vb, qc, kc, vc, gc, rc = [w[:, bounds[i]:bounds[i + 1]] for i in range(11)]
    rc = jnp.pad(rc, ((0, 0), (0, LANES - C_RANK)))
    qc = _pad_heads(qc, C_HEADS, C_K, C_KP, 1)
    kc = _pad_heads(kc, C_HEADS, C_K, C_KP, 1)
    vc = _pad_heads(vc, C_HEADS, C_V, C_VP, 1)
    gc = _pad_heads(gc, C_HEADS, C_V, C_VP, 1)
    return jnp.concatenate([qa, ka, qb, kb, va, vb, rc, qc, kc, vc, gc], axis=1).astype(BF16)


def _rope_tables(positions, dim):
    rot = dim // ROPE_FRAC
    inv_freq = ROPE_THETA ** (-jnp.arange(0, rot, 2, dtype=F32) / rot)
    ang = positions.astype(F32).reshape(-1, 1) * inv_freq
    cos, sin = jnp.cos(ang), jnp.sin(ang)
    t = ang.shape[0]
    c = jnp.concatenate([cos, cos, jnp.ones((t, dim - rot), F32)], axis=1)
    s = jnp.concatenate([-sin, sin, jnp.zeros((t, dim - rot), F32)], axis=1)
    reps = LANES // dim
    return jnp.tile(c, (1, reps)), jnp.tile(s, (1, reps))


def kernel(x, mem, positions, w_in, lam_q1, lam_k1, lam_q2, lam_k2, diff_norm_g, gla_wa2, gla_ba, gla_norm_g,
           w_out, ln_mix_g, ln_mix_b, xa_wq, xa_wkv, xa_wo, ln_xa_g, ln_xa_b, ffd_w13, ffd_w2, moe_router,
           moe_w13, moe_w2, ln_ffn_g, ln_ffn_b):
    bsz, seq, _ = x.shape
    t = bsz * seq
    ca, sa = _rope_tables(positions, A_QK)
    cb, sb = _rope_tables(positions, B_DIM)
    tabs = (ca, sa, cb, sb)
    memb = mem.reshape(-1, D_MODEL).astype(BF16)
    mem_len = mem.shape[1]
    row = lambda v: v.reshape(1, -1).astype(F32)

    xf = x.reshape(t, D_MODEL)
    xb = xf.astype(BF16)
    for l in range(DEPTH):
        lam_init = 0.8 - 0.6 * math.exp(-0.3 * l)
        lam = (jnp.exp(jnp.sum(lam_q1[l] * lam_k1[l])) - jnp.exp(jnp.sum(lam_q2[l] * lam_k2[l]))
               + lam_init).reshape(1, 1).astype(F32)
        h = _inproj(xb, _layout_w_in(w_in[l]), tabs)
        h3 = h.reshape(bsz, seq, H_COLS)
        oa = _diff_attention(h3, lam, jnp.tile(row(diff_norm_g[l]), (1, 2)), lam_init)
        ob = _dilated_attention(h3)
        wa = _pad_heads(gla_wa2[l], C_HEADS, C_K, C_KP, 1)
        wa = jnp.pad(wa, ((0, LANES - C_RANK), (0, 0))).reshape(LANES, 2, LANES).transpose(1, 0, 2).astype(BF16)
        ba = _pad_heads(gla_ba[l], C_HEADS, C_K, C_KP, 0).reshape(2, 1, LANES).astype(F32)
        gn = jnp.pad(row(gla_norm_g[l]), ((0, 0), (0, C_VP - C_V)))
        oc = _gla(h3, wa, ba, gn)
        wo = w_out[l]
        wo_a = wo[:A_HEADS * A_V].astype(BF16)
        wo_b = wo[A_HEADS * A_V:A_HEADS * A_V + B_HEADS * B_DIM].astype(BF16)
        wo_c = _pad_heads(wo[A_HEADS * A_V + B_HEADS * B_DIM:], C_HEADS, C_V, C_VP, 0).astype(BF16)
        kv = _matmul(memb, xa_wkv[l].astype(BF16), BF16).reshape(bsz, mem_len, 2 * D_MODEL)
        moe = l % 2 == 1
        rw = jnp.pad(moe_router[l // 2], ((0, 0), (0, LANES - N_EXPERTS))) if moe else None
        outs = _mix_xattn_ln(oa, ob, oc, (wo_a, wo_b, wo_c), xf.reshape(bsz, seq, D_MODEL),
                             (row(ln_mix_g[l]), row(ln_mix_b[l])), xa_wq[l].astype(BF16), kv, xa_wo[l].astype(BF16),
                             (row(ln_xa_g[l]), row(ln_xa_b[l])), router_w=rw)
        xf, xb = outs[0].reshape(t, D_MODEL), outs[1].reshape(t, D_MODEL)
        if moe:
            xf, xb = _moe_ln(xf, outs[2], outs[3].reshape(t, LANES), moe_w13[l // 2].astype(BF16),
                             moe_w2[l // 2].astype(BF16), row(ln_ffn_g[l]), row(ln_ffn_b[l]))
        else:
            xf, xb = _ffn_ln(xb, xf, ffd_w13[l // 2].astype(BF16), ffd_w2[l // 2].astype(BF16),
                             row(ln_ffn_g[l]), row(ln_ffn_b[l]))
    return xf.reshape(bsz, seq, D_MODEL)
```
